```python
import math
import jax, jax.numpy as jnp
from jax import lax
import numpy as np

D_MODEL = 1024
BATCH = 32
SEQ = 256
DEPTH = 2
DEC_BATCH = 8
DEC_SEQ = 1024
PAST_LEN = 256

GRID_W = 64
HEAD_DIM = 64
N_Q_HEADS = 8
N_KV_HEADS = 2
Q_PER_KV = N_Q_HEADS // N_KV_HEADS
ATTN_WIDTH = N_Q_HEADS * HEAD_DIM
KV_WIDTH = N_KV_HEADS * HEAD_DIM
FOURIER_WIDTH = D_MODEL // 4
FOURIER_GROUPS = 4
FOURIER_GROUP_DIM = FOURIER_WIDTH // FOURIER_GROUPS
HYENA_WIDTH = D_MODEL // 4
HYENA_ORDER = 2
HYENA_PROJ = (HYENA_ORDER + 1) * HYENA_WIDTH
SHORT_CONV = 3
HYENA_EMB_DIM = 33
HYENA_BANDS = (HYENA_EMB_DIM - 1) // 2
HYENA_FILTER_WIDTH = 64
HYENA_MIN_DECAY = 3.07
HYENA_MAX_DECAY = 15.35
MIX_WIDTH = FOURIER_WIDTH + HYENA_WIDTH + ATTN_WIDTH
IN_WIDTH = FOURIER_WIDTH + HYENA_PROJ + ATTN_WIDTH + 2 * KV_WIDTH
WINDOW = 128
BLOCK = 128
ROPE_BASE = 10000.0
D_FF = 2816
N_SUB = 3
RMS_EPS = 1e-6

kernel_name = 'hybrid_prefix_dit_step'

F32 = jnp.float32


def rmsnorm(x, g):
    xf = x.astype(F32)
    y = xf * lax.rsqrt(jnp.mean(xf * xf, axis=-1, keepdims=True) + RMS_EPS)
    return (y * g.astype(F32)).astype(x.dtype)


def modulation(cvec, w, b):
    m = jax.nn.silu(cvec) @ w + b
    return m.reshape(m.shape[0], 1, N_SUB, 3, D_MODEL)


def swiglu(h, wg, wu, wd):
    return (jax.nn.silu(h @ wg) * (h @ wu)) @ wd


def fourier_mix(u):
    b, n, _ = u.shape
    uf = u.astype(F32).reshape(b, n, FOURIER_GROUPS, FOURIER_GROUP_DIM)
    y = jnp.fft.fft2(uf, axes=(1, 3), norm='ortho').real
    return y.reshape(b, n, FOURIER_WIDTH).astype(u.dtype)


def short_conv(x, w):
    n = x.shape[1]
    pad = SHORT_CONV // 2
    xp = jnp.pad(x, ((0, 0), (pad, pad), (0, 0)))
    return sum(xp[:, i:i + n] * w[i] for i in range(SHORT_CONV))


def hyena_filters(n, w1, b1, w2, b2, w3, freq, decay):
    d = jnp.arange(n, dtype=F32)
    t = jnp.linspace(0.0, 1.0, n, dtype=F32)[:, None]
    f = jnp.linspace(1e-4, HYENA_BANDS - 1, HYENA_BANDS, dtype=F32)
    ang = (2.0 * math.pi / n) * d[:, None] * f[None, :]
    feats = jnp.concatenate([t, jnp.cos(ang), -jnp.sin(ang)], axis=-1)
    fr = freq.astype(F32)
    h = jnp.sin(fr * (feats @ w1.astype(F32) + b1.astype(F32)))
    h = jnp.sin(fr * (h @ w2.astype(F32) + b2.astype(F32)))
    h = (h @ w3.astype(F32)).reshape(n, 2, HYENA_ORDER, HYENA_WIDTH)
    window = jnp.exp(-t[:, :, None] * jnp.abs(decay.astype(F32))[None])
    h = h * window[:, None]
    fwd, bwd = h[:, 0], h[:, 1]
    return jnp.concatenate([fwd, jnp.zeros_like(fwd[:1]), bwd[:0:-1]], axis=0)


def fft_long_conv(u, k, bias):
    n = u.shape[1]
    uf = jnp.fft.rfft(u.astype(F32), n=2 * n, axis=1)
    kf = jnp.fft.rfft(k, n=2 * n, axis=0)
    y = jnp.fft.irfft(uf * kf[None], n=2 * n, axis=1)[:, :n] / (2 * n)
    return (y + u.astype(F32) * bias.astype(F32)).astype(u.dtype)


def hyena_mix(u, conv_w, w1, b1, w2, b2, w3, freq, decay, bias):
    z = short_conv(u, conv_w)
    v, g1, g2 = jnp.split(z, HYENA_ORDER + 1, axis=-1)
    k = hyena_filters(u.shape[1], w1, b1, w2, b2, w3, freq, decay)
    z = g1 * fft_long_conv(v, k[:, 0], bias[0])
    return g2 * fft_long_conv(z, k[:, 1], bias[1])


def rope_half(x, ang):
    c = jnp.cos(ang)[None, :, None, :]
    s = jnp.sin(ang)[None, :, None, :]
    x1, x2 = jnp.split(x.astype(F32), 2, axis=-1)
    return jnp.concatenate([x1 * c - x2 * s, x2 * c + x1 * s], axis=-1)


def axial_rope(x):
    n = x.shape[1]
    rows = n // GRID_W
    row = jnp.repeat(jnp.arange(rows, dtype=F32), GRID_W)
    col = jnp.tile(jnp.arange(GRID_W, dtype=F32), rows)
    half = HEAD_DIM // 2
    inv = ROPE_BASE ** (-jnp.arange(0, half, 2, dtype=F32) / half)
    xr, xc = jnp.split(x, 2, axis=-1)
    out = jnp.concatenate([rope_half(xr, row[:, None] * inv[None]),
                           rope_half(xc, col[:, None] * inv[None])], axis=-1)
    return out.astype(x.dtype)


def latent_attention(q, k, v, ck, cv, sink):
    b, n = q.shape[:2]
    nb = n // BLOCK
    c_len = ck.shape[1]
    scale = HEAD_DIM ** -0.5
    qb = q.reshape(b, nb, BLOCK, N_KV_HEADS, Q_PER_KV, HEAD_DIM)
    pad = ((0, 0), (BLOCK, BLOCK), (0, 0), (0, 0))
    kb = jnp.pad(k, pad).reshape(b, nb + 2, BLOCK, N_KV_HEADS, HEAD_DIM)
    vb = jnp.pad(v, pad).reshape(b, nb + 2, BLOCK, N_KV_HEADS, HEAD_DIM)
    band = lambda t: jnp.concatenate([t[:, :-2], t[:, 1:-1], t[:, 2:]], axis=2)
    kband, vband = band(kb), band(vb)
    qi = jnp.arange(BLOCK)
    kj = jnp.arange(3 * BLOCK)
    blk = jnp.arange(nb)
    rel = kj[None, :] - BLOCK - qi[:, None]
    kpos = (blk[:, None] - 1) * BLOCK + kj[None, :]
    valid = (jnp.abs(rel) <= WINDOW)[None] & ((kpos >= 0) & (kpos < n))[:, None, :]
    s_loc = jnp.einsum('bnqhgd,bnshd->bnhgqs', qb, kband, preferred_element_type=F32) * scale
    s_loc = jnp.where(valid[None, :, None, None], s_loc, -jnp.inf)
    s_ctx = jnp.einsum('bnqhgd,bchd->bnhgqc', qb, ck, preferred_element_type=F32) * scale
    s_sink = jnp.broadcast_to(sink.astype(F32).reshape(N_KV_HEADS, Q_PER_KV)[None, None, :, :, None, None],
                              s_loc.shape[:-1] + (1,))
    p = jax.nn.softmax(jnp.concatenate([s_loc, s_ctx, s_sink], axis=-1), axis=-1)
    s_w = 3 * BLOCK
    o = (jnp.einsum('bnhgqs,bnshd->bnqhgd', p[..., :s_w].astype(v.dtype), vband)
         + jnp.einsum('bnhgqc,bchd->bnqhgd', p[..., s_w:s_w + c_len].astype(cv.dtype), cv))
    return o.reshape(b, n, ATTN_WIDTH)


def context_attention(q, k, v, sink):
    b, c_len = q.shape[:2]
    nq = c_len // BLOCK
    scale = HEAD_DIM ** -0.5
    qb = q.reshape(b, nq, BLOCK, N_KV_HEADS, Q_PER_KV, HEAD_DIM).transpose(1, 0, 2, 3, 4, 5)
    sink_col = sink.astype(F32).reshape(N_KV_HEADS, Q_PER_KV)

    def one_block(qblk):
        s = jnp.einsum('bqhgd,bchd->bhgqc', qblk, k, preferred_element_type=F32) * scale
        s_sink = jnp.broadcast_to(sink_col[None, :, :, None, None], s.shape[:-1] + (1,))
        p = jax.nn.softmax(jnp.concatenate([s, s_sink], axis=-1), axis=-1)[..., :c_len]
        return jnp.einsum('bhgqc,bchd->bqhgd', p.astype(v.dtype), v)

    o = lax.map(one_block, qb)
    return o.transpose(1, 0, 2, 3, 4, 5).reshape(b, c_len, ATTN_WIDTH)


def trunk_layer(x, m, latent, ctx_k, ctx_v, g, wg, wu, wd, w_in, w_out, conv_w,
                f_w1, f_b1, f_w2, f_b2, f_w3, f_freq, decay, hbias, sink):
    b, n, _ = x.shape
    h = rmsnorm(x, g[0]) * (1 + m[:, :, 0, 1]) + m[:, :, 0, 0]
    x = x + 0.5 * m[:, :, 0, 2] * rmsnorm(swiglu(h, wg[0], wu[0], wd[0]), g[1])
    h = rmsnorm(x, g[2]) * (1 + m[:, :, 1, 1]) + m[:, :, 1, 0]
    p = h @ w_in
    o1 = FOURIER_WIDTH
    o2 = o1 + HYENA_PROJ
    o3 = o2 + ATTN_WIDTH
    o4 = o3 + KV_WIDTH
    y_f = fourier_mix(p[..., :o1])
    y_h = hyena_mix(p[..., o1:o2], conv_w, f_w1, f_b1, f_w2, f_b2, f_w3, f_freq, decay, hbias)
    q = p[..., o2:o3].reshape(b, n, N_Q_HEADS, HEAD_DIM)
    k = p[..., o3:o4].reshape(b, n, N_KV_HEADS, HEAD_DIM)
    v = p[..., o4:].reshape(b, n, N_KV_HEADS, HEAD_DIM)
    if latent:
        y_a = latent_attention(axial_rope(q), axial_rope(k), v, ctx_k, ctx_v, sink)
    else:
        y_a = context_attention(q, k, v, sink)
    y = jnp.concatenate([y_f, y_h, y_a], axis=-1) @ w_out
    x = x + m[:, :, 1, 2] * rmsnorm(y, g[3])
    h = rmsnorm(x, g[4]) * (1 + m[:, :, 2, 1]) + m[:, :, 2, 0]
    x = x + 0.5 * m[:, :, 2, 2] * rmsnorm(swiglu(h, wg[1], wu[1], wd[1]), g[5])
    return x, k, v


def setup_inputs(seed: int = 0) -> dict:
    key = jax.random.key(seed)
    ks = jax.random.split(key, 24)
    nrm = lambda k, shape, s: jax.random.normal(k, shape, F32) * s
    kv_cache_shape = (DEC_BATCH, DEPTH, PAST_LEN, N_KV_HEADS, HEAD_DIM)
    return {
        'x_prompt': nrm(ks[0], (BATCH, SEQ, D_MODEL), 1.0),
        'x_sample': nrm(ks[1], (DEC_BATCH, DEC_SEQ, D_MODEL), 1.0),
        'cache_k': nrm(ks[2], kv_cache_shape, 1.0),
        'cache_v': nrm(ks[3], kv_cache_shape, 1.0),
        'c': nrm(ks[4], (DEC_BATCH, D_MODEL), 1.0),
        'c_ctx': nrm(ks[5], (D_MODEL,), 1.0),
        'w_mod': nrm(ks[6], (DEPTH, D_MODEL, 3 * N_SUB * D_MODEL), 0.5 * D_MODEL ** -0.5),
        'b_mod': nrm(ks[7], (DEPTH, 3 * N_SUB * D_MODEL), 0.02),
        'norm_g': 1.0 + nrm(ks[8], (DEPTH, 2 * N_SUB, D_MODEL), 0.05),
        'ffn_w_gate': nrm(ks[9], (DEPTH, 2, D_MODEL, D_FF), D_MODEL ** -0.5),
        'ffn_w_up': nrm(ks[10], (DEPTH, 2, D_MODEL, D_FF), D_MODEL ** -0.5),
        'ffn_w_down': nrm(ks[11], (DEPTH, 2, D_FF, D_MODEL), D_FF ** -0.5),
        'w_in': nrm(ks[12], (DEPTH, D_MODEL, IN_WIDTH), D_MODEL ** -0.5),
        'w_out': nrm(ks[13], (DEPTH, MIX_WIDTH, D_MODEL), MIX_WIDTH ** -0.5),
        'hyena_conv_w': nrm(ks[14], (DEPTH, SHORT_CONV, HYENA_PROJ), SHORT_CONV ** -0.5),
        'hyena_f_w1': nrm(ks[15], (DEPTH, HYENA_EMB_DIM, HYENA_FILTER_WIDTH), HYENA_EMB_DIM ** -0.5),
        'hyena_f_b1': nrm(ks[16], (DEPTH, HYENA_FILTER_WIDTH), 0.02),
        'hyena_f_w2': nrm(ks[17], (DEPTH, HYENA_FILTER_WIDTH, HYENA_FILTER_WIDTH), HYENA_FILTER_WIDTH ** -0.5),
        'hyena_f_b2': nrm(ks[18], (DEPTH, HYENA_FILTER_WIDTH), 0.02),
        'hyena_f_w3': nrm(ks[19], (DEPTH, HYENA_FILTER_WIDTH, 2 * HYENA_ORDER * HYENA_WIDTH), HYENA_FILTER_WIDTH ** -0.5),
        'hyena_f_freq': 1.0 + nrm(ks[20], (DEPTH, HYENA_FILTER_WIDTH), 0.05),
        'hyena_decay': jnp.linspace(HYENA_MIN_DECAY, HYENA_MAX_DECAY, HYENA_WIDTH, dtype=F32)[None, None, :]
                       + nrm(ks[21], (DEPTH, HYENA_ORDER, HYENA_WIDTH), 0.1),
        'hyena_bias': nrm(ks[22], (DEPTH, HYENA_ORDER, HYENA_WIDTH), 1.0),
        'attn_sink': nrm(ks[23], (DEPTH, N_Q_HEADS), 0.5),
    }


def reference(x_prompt, x_sample, cache_k, cache_v, c, c_ctx, w_mod, b_mod, norm_g,
              ffn_w_gate, ffn_w_up, ffn_w_down, w_in, w_out, hyena_conv_w,
              hyena_f_w1, hyena_f_b1, hyena_f_w2, hyena_f_b2, hyena_f_w3, hyena_f_freq,
              hyena_decay, hyena_bias, attn_sink):
    def layer_params(l):
        return (norm_g[l], ffn_w_gate[l], ffn_w_up[l], ffn_w_down[l], w_in[l], w_out[l],
                hyena_conv_w[l], hyena_f_w1[l], hyena_f_b1[l], hyena_f_w2[l], hyena_f_b2[l],
                hyena_f_w3[l], hyena_f_freq[l], hyena_decay[l], hyena_bias[l], attn_sink[l])

    xp = x_prompt
    new_k, new_v = [], []
    for l in range(DEPTH):
        m_ctx = modulation(c_ctx[None], w_mod[l], b_mod[l])
        xp, k_l, v_l = trunk_layer(xp, m_ctx, False, None, None, *layer_params(l))
        new_k.append(k_l)
        new_v.append(v_l)

    xs = x_sample
    for l in range(DEPTH):
        m_lat = modulation(c, w_mod[l], b_mod[l])
        xs, _, _ = trunk_layer(xs, m_lat, True, cache_k[:, l], cache_v[:, l], *layer_params(l))

    return (xp, xs, jnp.stack(new_k, axis=1), jnp.stack(new_v, axis=1))
```

```python
import functools
import math

import numpy as np
import jax
import jax.numpy as jnp
from jax import lax
from jax.experimental import pallas as pl
from jax.experimental.pallas import tpu as pltpu

F32 = jnp.float32
BF16 = jnp.bfloat16

D_MODEL = 1024
DEPTH = 2
GRID_W = 64
HEAD_DIM = 64
N_Q_HEADS = 8
N_KV_HEADS = 2
ATTN_WIDTH = N_Q_HEADS * HEAD_DIM
KV_WIDTH = N_KV_HEADS * HEAD_DIM
FOURIER_WIDTH = 256
FOURIER_GROUPS = 4
FOURIER_GROUP_DIM = 64
HYENA_WIDTH = 256
HYENA_ORDER = 2
HYENA_PROJ = 3 * HYENA_WIDTH
HYENA_EMB_DIM = 33
HYENA_BANDS = 16
HYENA_FILTER_WIDTH = 64
FH_WIDTH = FOURIER_WIDTH + HYENA_PROJ
ATT_WIDTH = ATTN_WIDTH + 2 * KV_WIDTH
IN_WIDTH = FH_WIDTH + ATT_WIDTH
BLOCK = 128
WINDOW = 128
ROPE_BASE = 10000.0
D_FF = 2816
N_SUB = 3
RMS_EPS = 1e-6

LANES = 128
MOD_ROWS = 16
VMEM_LIMIT = 56 * 1024 * 1024
NEG_BIG = -1e30


def _cparams(*sem):
    return pltpu.CompilerParams(dimension_semantics=sem, vmem_limit_bytes=VMEM_LIMIT)


def _rms(x, g):
    return x * lax.rsqrt(jnp.mean(x * x, axis=-1, keepdims=True) + RMS_EPS) * g


def _dot(a, b):
    return jnp.dot(a, b, preferred_element_type=F32)


def _dot_nt(a, b):
    return lax.dot_general(a, b, (((1,), (1,)), ((), ())), preferred_element_type=F32)


@functools.lru_cache(maxsize=None)
def _hyena_dft(n):
    f = np.arange(n, dtype=np.int64)[:, None]
    s = np.arange(n, dtype=np.int64)[None, :]
    ang = np.pi * ((f * s) % (2 * n)).astype(np.float64) / n
    c = np.cos(ang)
    sn = np.sin(ang)
    sn[0, :] = 1.0 - 2.0 * (np.arange(n) % 2)
    fw = np.concatenate([c, sn], axis=0).astype(np.float32)
    iw = np.concatenate([c, sn.T], axis=1).astype(np.float32)
    return fw, iw


@functools.lru_cache(maxsize=None)
def _fourier_tables(n):
    f = np.arange(n, dtype=np.int64)[:, None]
    s = np.arange(n, dtype=np.int64)[None, :]
    ang = 2.0 * np.pi * ((f * s) % n).astype(np.float64) / n
    pos = np.concatenate([np.cos(ang), -np.sin(ang)], axis=1) / math.sqrt(n)
    gd = FOURIER_GROUP_DIM
    a = np.arange(gd, dtype=np.int64)
    ang_c = 2.0 * np.pi * ((a[:, None] * a[None, :]) % gd).astype(np.float64) / gd
    bc = np.kron(np.eye(FOURIER_GROUPS), np.cos(ang_c)) / math.sqrt(gd)
    bs = np.kron(np.eye(FOURIER_GROUPS), np.sin(ang_c)) / math.sqrt(gd)
    chan = np.concatenate([bc, bs], axis=1)
    return pos.astype(np.float32), chan.astype(np.float32)


@functools.lru_cache(maxsize=None)
def _rope_tables(n):
    half = HEAD_DIM // 2
    inv = ROPE_BASE ** (-np.arange(0, half, 2, dtype=np.float64) / half)
    t = np.arange(n)
    row = (t // GRID_W).astype(np.float64)
    col = (t % GRID_W).astype(np.float64)
    lane = np.arange(LANES)
    d = lane % HEAD_DIM
    pos = np.where((d // half)[None, :] == 0, row[:, None], col[:, None])
    ang = pos * inv[d % (half // 2)][None, :]
    sign = np.where((d % half) < half // 2, -1.0, 1.0)[None, :]
    return np.cos(ang).astype(np.float32), (np.sin(ang) * sign).astype(np.float32)


def _hyena_feats(n):
    d = jnp.arange(n, dtype=F32)
    t = jnp.linspace(0.0, 1.0, n, dtype=F32)[:, None]
    f = jnp.linspace(1e-4, HYENA_BANDS - 1, HYENA_BANDS, dtype=F32)
    ang = (2.0 * math.pi / n) * d[:, None] * f[None, :]
    feats = jnp.concatenate([t, jnp.cos(ang), -jnp.sin(ang)], axis=-1)
    return jnp.pad(feats, ((0, 0), (0, LANES - HYENA_EMB_DIM))), t


def _mod_kernel(c_ref, w_ref, b_ref, o_ref):
    c = c_ref[...]
    s = c / (1.0 + jnp.exp(-c))
    o_ref[0] = _dot(s.astype(BF16), w_ref[0].astype(BF16)) + b_ref[0]


def _modulation(cvecs, w_mod, b_mod):
    depth, _, width = w_mod.shape
    tn = 1536
    return pl.pallas_call(
        _mod_kernel,
        grid=(depth, width // tn),
        in_specs=[
            pl.BlockSpec((MOD_ROWS, D_MODEL), lambda l, j: (0, 0)),
            pl.BlockSpec((1, D_MODEL, tn), lambda l, j: (l, 0, j)),
            pl.BlockSpec((1, 1, tn), lambda l, j: (l, 0, j)),
        ],
        out_specs=pl.BlockSpec((1, MOD_ROWS, tn), lambda l, j: (l, 0, j)),
        out_shape=jax.ShapeDtypeStruct((depth, MOD_ROWS, width), F32),
        compiler_params=_cparams("arbitrary", "arbitrary"),
        name="modulation",
    )(cvecs, w_mod, b_mod.reshape(depth, 1, width))


def _ffn_kernel(x_ref, mod_ref, g_ref, wg_ref, wu_ref, wd_ref, o_ref, h_scr, acc_scr):
    j = pl.program_id(1)

    @pl.when(j == 0)
    def _():
        h = _rms(x_ref[...], g_ref[0:1]) * (1.0 + mod_ref[0, 1:2]) + mod_ref[0, 0:1]
        h_scr[...] = h.astype(BF16)
        acc_scr[...] = jnp.zeros_like(acc_scr)

    h = h_scr[...]
    gate = _dot(h, wg_ref[...])
    up = _dot(h, wu_ref[...])
    act = (gate / (1.0 + jnp.exp(-gate))) * up
    acc_scr[...] += _dot(act.astype(BF16), wd_ref[...])

    @pl.when(j == pl.num_programs(1) - 1)
    def _():
        y = _rms(acc_scr[...], g_ref[1:2])
        o_ref[...] = x_ref[...] + (0.5 * mod_ref[0, 2:3]) * y


def _ffn(x, mod, g, wg, wu, wd, rows_per_req, tm=512, tf=1408):
    t = x.shape[0]
    tiles_per_req = rows_per_req // tm
    return pl.pallas_call(
        _ffn_kernel,
        grid=(t // tm, D_FF // tf),
        in_specs=[
            pl.BlockSpec((tm, D_MODEL), lambda i, j: (i, 0)),
            pl.BlockSpec((1, 3, D_MODEL), lambda i, j: (i // tiles_per_req, 0, 0)),
            pl.BlockSpec((2, D_MODEL), lambda i, j: (0, 0)),
            pl.BlockSpec((D_MODEL, tf), lambda i, j: (0, j)),
            pl.BlockSpec((D_MODEL, tf), lambda i, j: (0, j)),
            pl.BlockSpec((tf, D_MODEL), lambda i, j: (j, 0)),
        ],
        out_specs=pl.BlockSpec((tm, D_MODEL), lambda i, j: (i, 0)),
        out_shape=jax.ShapeDtypeStruct((t, D_MODEL), F32),
        scratch_shapes=[pltpu.VMEM((tm, D_MODEL), BF16), pltpu.VMEM((tm, D_MODEL), F32)],
        compiler_params=_cparams("parallel", "arbitrary"),
        name="ffn",
    )(x, mod, g, wg, wu, wd)


def _inproj_kernel(x_ref, mod_ref, g_ref, w_ref, fh_ref, att_ref):
    h = _rms(x_ref[...], g_ref[...]) * (1.0 + mod_ref[0, 1:2]) + mod_ref[0, 0:1]
    p = _dot(h.astype(BF16), w_ref[...])
    fh_ref[...] = p[:, :FH_WIDTH]
    att_ref[...] = p[:, FH_WIDTH:]


def _in_proj(x, mod, g, w_in, rows_per_req, tm=512):
    t = x.shape[0]
    tiles_per_req = rows_per_req // tm
    return pl.pallas_call(
        _inproj_kernel,
        grid=(t // tm,),
        in_specs=[
            pl.BlockSpec((tm, D_MODEL), lambda i: (i, 0)),
            pl.BlockSpec((1, 3, D_MODEL), lambda i: (i // tiles_per_req, 0, 0)),
            pl.BlockSpec((1, D_MODEL), lambda i: (0, 0)),
            pl.BlockSpec((D_MODEL, IN_WIDTH), lambda i: (0, 0)),
        ],
        out_specs=[
            pl.BlockSpec((tm, FH_WIDTH), lambda i: (i, 0)),
            pl.BlockSpec((tm, ATT_WIDTH), lambda i: (i, 0)),
        ],
        out_shape=[jax.ShapeDtypeStruct((t, FH_WIDTH), F32),
                   jax.ShapeDtypeStruct((t, ATT_WIDTH), F32)],
        compiler_params=_cparams("parallel"),
        name="in_proj",
    )(x, mod, g, w_in)


def _outproj_kernel(x_ref, fh_ref, att_ref, mod_ref, g_ref, w_ref, o_ref):
    half = FOURIER_WIDTH + HYENA_WIDTH
    y = _dot(fh_ref[...], w_ref[:half]) + _dot(att_ref[...], w_ref[half:])
    o_ref[...] = x_ref[...] + mod_ref[0, 2:3] * _rms(y, g_ref[...])


def _out_proj(x, y_fh, y_att, mod, g, w_out, rows_per_req, tm=512):
    t = x.shape[0]
    tiles_per_req = rows_per_req // tm
    half = FOURIER_WIDTH + HYENA_WIDTH
    return pl.pallas_call(
        _outproj_kernel,
        grid=(t // tm,),
        in_specs=[
            pl.BlockSpec((tm, D_MODEL), lambda i: (i, 0)),
            pl.BlockSpec((tm, half), lambda i: (i, 0)),
            pl.BlockSpec((tm, ATTN_WIDTH), lambda i: (i, 0)),
            pl.BlockSpec((1, 3, D_MODEL), lambda i: (i // tiles_per_req, 0, 0)),
            pl.BlockSpec((1, D_MODEL), lambda i: (0, 0)),
            pl.BlockSpec((D_MODEL, D_MODEL), lambda i: (0, 0)),
        ],
        out_specs=pl.BlockSpec((tm, D_MODEL), lambda i: (i, 0)),
        out_shape=jax.ShapeDtypeStruct((t, D_MODEL), F32),
        compiler_params=_cparams("parallel"),
        name="out_proj",
    )(x, y_fh, y_att, mod, g, w_out)


def _filter_kernel(feats_ref, t_ref, w1_ref, b1_ref, w2_ref, b2_ref, w3_ref, fr_ref, decay_ref,
                   fw_ref, o_ref, *, n):
    fr = fr_ref[...]
    h = jnp.sin(fr * (_dot(feats_ref[...], w1_ref[...]) + b1_ref[...]))
    h = jnp.sin(fr * (_dot(h, w2_ref[...]) + b2_ref[...]))
    h = _dot(h, w3_ref[...])
    window = jnp.exp(-t_ref[...] * jnp.abs(decay_ref[...]))
    width = HYENA_ORDER * HYENA_WIDTH
    row = lax.broadcasted_iota(jnp.int32, (n, width), 0)
    fwd = h[:, :width] * window
    bwd = jnp.where(row == 0, 0.0, h[:, width:] * window)
    even = fwd + bwd
    odd = fwd - bwd
    k_re = _dot(fw_ref[:n], even.astype(BF16))
    k_sn = _dot(fw_ref[n:], odd.astype(BF16))
    sign = (1 - 2 * (row % 2)).astype(F32)
    k_ny = jnp.sum(even * sign, axis=0, keepdims=True)
    s0 = 1.0 / (4.0 * n * n)
    scale = jnp.where(row == 0, s0, 2.0 * s0)
    a = scale * k_re
    o_ref[0] = a
    o_ref[1] = jnp.where(row == 0, 0.0, scale * k_sn)
    o_ref[2] = jnp.where(row == 0, s0 * k_ny, a)


def _filter_spectrum(n, fw, w1, b1, w2, b2, w3, freq, decay):
    feats, t = _hyena_feats(n)
    pad_w = LANES - HYENA_FILTER_WIDTH
    w1p = jnp.pad(w1, ((0, LANES - HYENA_EMB_DIM), (0, pad_w)))
    w2p = jnp.pad(w2, ((0, pad_w), (0, pad_w)))
    w3p = jnp.pad(w3, ((0, pad_w), (0, 0)))
    padv = lambda v: jnp.pad(v, (0, pad_w)).reshape(1, LANES)
    width = HYENA_ORDER * HYENA_WIDTH
    args = (feats, t, w1p, padv(b1), w2p, padv(b2), w3p, padv(freq), decay.reshape(1, width), fw)
    return pl.pallas_call(
        functools.partial(_filter_kernel, n=n),
        out_shape=jax.ShapeDtypeStruct((3, n, width), F32),
        compiler_params=pltpu.CompilerParams(vmem_limit_bytes=VMEM_LIMIT),
        name="hyena_filter",
    )(*args)


def _fh_kernel(p_ref, pos_ref, chan_ref, fw_ref, iw_ref, coef_ref, convw_ref, bias_ref, o_ref,
               st_scr, *, n):
    u = p_ref[0, :, :FOURIER_WIDTH].astype(BF16)
    t = _dot(u, chan_ref[...])
    st_scr[:n] = t[:, :FOURIER_WIDTH].astype(BF16)
    st_scr[n:] = t[:, FOURIER_WIDTH:].astype(BF16)
    o_ref[0, :, :FOURIER_WIDTH] = _dot(pos_ref[...], st_scr[...]).astype(o_ref.dtype)

    z = p_ref[0, :, FOURIER_WIDTH:]
    row = lax.broadcasted_iota(jnp.int32, z.shape, 0)
    z_prev = jnp.where(row == 0, 0.0, pltpu.roll(z, 1, axis=0))
    z_next = jnp.where(row == n - 1, 0.0, pltpu.roll(z, n - 1, axis=0))
    z = z_prev * convw_ref[0:1] + z * convw_ref[1:2] + z_next * convw_ref[2:3]
    c = HYENA_WIDTH

    def long_conv(v, order):
        cols = slice(order * c, (order + 1) * c)
        uf = _dot(fw_ref[...], v.astype(BF16))
        u_re, u_sn = uf[:n], uf[n:]
        a, b, a2 = coef_ref[0, :, cols], coef_ref[1, :, cols], coef_ref[2, :, cols]
        st_scr[:n] = (u_re * a - u_sn * b).astype(BF16)
        st_scr[n:] = (u_re * b + u_sn * a2).astype(BF16)
        return _dot(iw_ref[...], st_scr[...]) + v * bias_ref[order:order + 1]

    y = z[:, c:2 * c] * long_conv(z[:, :c], 0)
    y = z[:, 2 * c:] * long_conv(y, 1)
    o_ref[0, :, FOURIER_WIDTH:] = y.astype(o_ref.dtype)


def _fh_mix(p_fh, n, pos, chan, fw, iw, coef, conv_w, hbias):
    b = p_fh.shape[0]
    width = FOURIER_WIDTH + HYENA_WIDTH
    const = lambda shape: pl.BlockSpec(shape, lambda i: (0,) * len(shape))
    return pl.pallas_call(
        functools.partial(_fh_kernel, n=n),
        grid=(b,),
        in_specs=[
            pl.BlockSpec((1, n, FH_WIDTH), lambda i: (i, 0, 0)),
            const((n, 2 * n)), const((FOURIER_WIDTH, 2 * FOURIER_WIDTH)),
            const((2 * n, n)), const((n, 2 * n)),
            const((3, n, HYENA_ORDER * HYENA_WIDTH)),
            const((3, HYENA_PROJ)), const((HYENA_ORDER, HYENA_WIDTH)),
        ],
        out_specs=pl.BlockSpec((1, n, width), lambda i: (i, 0, 0)),
        out_shape=jax.ShapeDtypeStruct((b, n, width), BF16),
        scratch_shapes=[pltpu.VMEM((2 * n, HYENA_WIDTH), BF16)],
        compiler_params=_cparams("parallel"),
        name="fourier_hyena",
    )(p_fh, pos, chan, fw, iw, coef, conv_w, hbias)


def _head_slots(x, kv_head, lo):
    xr = pltpu.roll(x, HEAD_DIM, axis=1)
    if kv_head == 0:
        return jnp.where(lo, x, 0.0), jnp.where(lo, 0.0, xr)
    return jnp.where(lo, xr, 0.0), jnp.where(lo, 0.0, x)


def _ctx_attn_kernel(sink_ref, p_ref, o_ref, *, n):
    lo = lax.broadcasted_iota(jnp.int32, (n, LANES), 1) < HEAD_DIM
    k = p_ref[0, :, ATTN_WIDTH:ATTN_WIDTH + KV_WIDTH]
    v = p_ref[0, :, ATTN_WIDTH + KV_WIDTH:]
    for kv_head in range(N_KV_HEADS):
        k_slots = [s.astype(BF16) for s in _head_slots(k, kv_head, lo)]
        v_slots = [s.astype(BF16) for s in _head_slots(v, kv_head, lo)]
        for pair in range(2):
            tile = kv_head * 2 + pair
            q = (p_ref[0, :, tile * LANES:(tile + 1) * LANES] * HEAD_DIM ** -0.5).astype(BF16)
            acc = None
            for slot in range(2):
                sink = sink_ref[2 * tile + slot]
                s = _dot_nt(q, k_slots[slot])
                m = jnp.maximum(jnp.max(s, axis=-1, keepdims=True), sink)
                e = jnp.exp(s - m)
                den = jnp.sum(e, axis=-1, keepdims=True) + jnp.exp(sink - m)
                o = _dot(e.astype(BF16), v_slots[slot]) * (1.0 / den)
                acc = o if acc is None else acc + o
            o_ref[0, :, tile * LANES:(tile + 1) * LANES] = acc.astype(o_ref.dtype)


def _ctx_attention(p_att, n, sink):
    b = p_att.shape[0]
    return pl.pallas_call(
        functools.partial(_ctx_attn_kernel, n=n),
        grid=(b,),
        in_specs=[
            pl.BlockSpec(memory_space=pltpu.SMEM),
            pl.BlockSpec((1, n, ATT_WIDTH), lambda i: (i, 0, 0)),
        ],
        out_specs=pl.BlockSpec((1, n, ATTN_WIDTH), lambda i: (i, 0, 0)),
        out_shape=jax.ShapeDtypeStruct((b, n, ATTN_WIDTH), BF16),
        compiler_params=_cparams("parallel"),
        name="ctx_attention",
    )(sink, p_att)


def _rope(x, cos, sin_signed):
    lane = lax.broadcasted_iota(jnp.int32, x.shape, 1)
    first = (lane % (HEAD_DIM // 2)) < HEAD_DIM // 4
    partner = jnp.where(first, pltpu.roll(x, LANES - HEAD_DIM // 4, axis=1),
                        pltpu.roll(x, HEAD_DIM // 4, axis=1))
    return x * cos + partner * sin_signed


def _lat_attn_kernel(sink_ref, p_ref, ck_ref, cv_ref, cos_ref, sin_ref, o_ref,
                     q_scr, k_scr, v_scr, ck_scr, cv_scr, *, n, c_len):
    nb = n // BLOCK
    cos, sin = cos_ref[...], sin_ref[...]
    lo = lax.broadcasted_iota(jnp.int32, (n, LANES), 1) < HEAD_DIM
    lo_c = lax.broadcasted_iota(jnp.int32, (c_len, LANES), 1) < HEAD_DIM

    for tile in range(ATTN_WIDTH // LANES):
        q = _rope(p_ref[0, :, tile * LANES:(tile + 1) * LANES], cos, sin) * HEAD_DIM ** -0.5
        q_scr[:, tile * LANES:(tile + 1) * LANES] = q.astype(BF16)
    k = _rope(p_ref[0, :, ATTN_WIDTH:ATTN_WIDTH + KV_WIDTH], cos, sin)
    v = p_ref[0, :, ATTN_WIDTH + KV_WIDTH:]
    zeros = jnp.zeros((BLOCK, LANES), BF16)
    for kv_head in range(N_KV_HEADS):
        for slot, (ks, vs, cks, cvs) in enumerate(zip(
                _head_slots(k, kv_head, lo), _head_slots(v, kv_head, lo),
                _head_slots(ck_ref[0, 0], kv_head, lo_c), _head_slots(cv_ref[0, 0], kv_head, lo_c))):
            idx = kv_head * 2 + slot
            k_scr[idx, :BLOCK] = zeros
            k_scr[idx, BLOCK + n:] = zeros
            v_scr[idx, :BLOCK] = zeros
            v_scr[idx, BLOCK + n:] = zeros
            k_scr[idx, BLOCK:BLOCK + n] = ks.astype(BF16)
            v_scr[idx, BLOCK:BLOCK + n] = vs.astype(BF16)
            ck_scr[idx] = cks.astype(BF16)
            cv_scr[idx] = cvs.astype(BF16)

    qi = lax.broadcasted_iota(jnp.int32, (BLOCK, 3 * BLOCK), 0)
    kj = lax.broadcasted_iota(jnp.int32, (BLOCK, 3 * BLOCK), 1)
    rel = kj - BLOCK - qi
    in_window = jnp.abs(rel) <= WINDOW

    for tile in range(ATTN_WIDTH // LANES):
        kv_head = tile // 2

        def block_body(i, carry, tile=tile, kv_head=kv_head):
            r0 = pl.multiple_of(i * BLOCK, BLOCK)
            q = q_scr[pl.ds(r0, BLOCK), tile * LANES:(tile + 1) * LANES]
            kpos = kj + (i - 1) * BLOCK
            valid = in_window & (kpos >= 0) & (kpos < n)
            acc = None
            for slot in range(2):
                idx = kv_head * 2 + slot
                sink = sink_ref[2 * tile + slot]
                s_loc = _dot_nt(q, k_scr[idx, pl.ds(r0, 3 * BLOCK), :])
                s_loc = jnp.where(valid, s_loc, NEG_BIG)
                s_ctx = _dot_nt(q, ck_scr[idx])
                m = jnp.maximum(jnp.maximum(jnp.max(s_loc, axis=-1, keepdims=True),
                                            jnp.max(s_ctx, axis=-1, keepdims=True)), sink)
                e_loc = jnp.exp(s_loc - m)
                e_ctx = jnp.exp(s_ctx - m)
                den = (jnp.sum(e_loc, axis=-1, keepdims=True) + jnp.sum(e_ctx, axis=-1, keepdims=True)
                       + jnp.exp(sink - m))
                o = (_dot(e_loc.astype(BF16), v_scr[idx, pl.ds(r0, 3 * BLOCK), :])
                     + _dot(e_ctx.astype(BF16), cv_scr[idx])) * (1.0 / den)
                acc = o if acc is None else acc + o
            o_ref[0, pl.ds(r0, BLOCK), tile * LANES:(tile + 1) * LANES] = acc.astype(o_ref.dtype)
            return carry

        lax.fori_loop(0, nb, block_body, 0)


def _lat_attention(p_att, n, sink, cache_k, cache_v, layer, cos, sin):
    b = p_att.shape[0]
    c_len = cache_k.shape[2]
    slots = 2 * N_KV_HEADS
    return pl.pallas_call(
        functools.partial(_lat_attn_kernel, n=n, c_len=c_len),
        grid=(b,),
        in_specs=[
            pl.BlockSpec(memory_space=pltpu.SMEM),
            pl.BlockSpec((1, n, ATT_WIDTH), lambda i: (i, 0, 0)),
            pl.BlockSpec((1, 1, c_len, KV_WIDTH), lambda i: (i, layer, 0, 0)),
            pl.BlockSpec((1, 1, c_len, KV_WIDTH), lambda i: (i, layer, 0, 0)),
            pl.BlockSpec((n, LANES), lambda i: (0, 0)),
            pl.BlockSpec((n, LANES), lambda i: (0, 0)),
        ],
        out_specs=pl.BlockSpec((1, n, ATTN_WIDTH), lambda i: (i, 0, 0)),
        out_shape=jax.ShapeDtypeStruct((b, n, ATTN_WIDTH), BF16),
        scratch_shapes=[
            pltpu.VMEM((n, ATTN_WIDTH), BF16),
            pltpu.VMEM((slots, n + 2 * BLOCK, LANES), BF16),
            pltpu.VMEM((slots, n + 2 * BLOCK, LANES), BF16),
            pltpu.VMEM((slots, c_len, LANES), BF16),
            pltpu.VMEM((slots, c_len, LANES), BF16),
        ],
        compiler_params=_cparams("parallel"),
        name="latent_attention",
    )(sink, p_att, cache_k, cache_v, cos, sin)


def kernel(x_prompt, x_sample, cache_k, cache_v, c, c_ctx, w_mod, b_mod, norm_g, ffn_w_gate, ffn_w_up,
           ffn_w_down, w_in, w_out, hyena_conv_w, hyena_f_w1, hyena_f_b1, hyena_f_w2, hyena_f_b2,
           hyena_f_w3, hyena_f_freq, hyena_decay, hyena_bias, attn_sink):
    batch, seq, d = x_prompt.shape
    dec_batch, dec_seq, _ = x_sample.shape
    past_len = cache_k.shape[2]

    cvecs = jnp.concatenate([c_ctx[None], c], axis=0)
    cvecs = jnp.pad(cvecs, ((0, MOD_ROWS - cvecs.shape[0]), (0, 0)))
    mod = _modulation(cvecs, w_mod, b_mod).reshape(DEPTH, MOD_ROWS, N_SUB, 3, d)

    wg = ffn_w_gate.astype(BF16)
    wu = ffn_w_up.astype(BF16)
    wd = ffn_w_down.astype(BF16)
    w_in_b = w_in.astype(BF16)
    w_out_b = w_out.astype(BF16)
    ck = cache_k.reshape(dec_batch, DEPTH, past_len, KV_WIDTH)
    cv = cache_v.reshape(dec_batch, DEPTH, past_len, KV_WIDTH)
    rope_cos, rope_sin = (jnp.asarray(t) for t in _rope_tables(dec_seq))

    passes = []
    for n, nreq in ((seq, 1), (dec_seq, dec_batch)):
        fw, iw = (jnp.asarray(t).astype(BF16) for t in _hyena_dft(n))
        pos, chan = (jnp.asarray(t).astype(BF16) for t in _fourier_tables(n))
        passes.append((n, nreq, fw, iw, pos, chan))

    xs = [x_prompt.reshape(batch * seq, d), x_sample.reshape(dec_batch * dec_seq, d)]
    new_k, new_v = [], []
    for l in range(DEPTH):
        g = norm_g[l]
        for which, (n, nreq, fw, iw, pos, chan) in enumerate(passes):
            latent = which == 1
            x = xs[which]
            nb = x.shape[0] // n
            m = mod[l, 1:1 + nreq] if latent else mod[l, 0:1]
            rows_per_req = n if latent else x.shape[0]
            x = _ffn(x, m[:, 0], g[0:2], wg[l, 0], wu[l, 0], wd[l, 0], rows_per_req)
            p_fh, p_att = _in_proj(x, m[:, 1], g[2:3], w_in_b[l], rows_per_req)
            coef = _filter_spectrum(n, fw, hyena_f_w1[l], hyena_f_b1[l], hyena_f_w2[l], hyena_f_b2[l],
                                    hyena_f_w3[l], hyena_f_freq[l], hyena_decay[l])
            y_fh = _fh_mix(p_fh.reshape(nb, n, FH_WIDTH), n, pos, chan, fw, iw, coef,
                           hyena_conv_w[l], hyena_bias[l])
            p_att3 = p_att.reshape(nb, n, ATT_WIDTH)
            if latent:
                y_att = _lat_attention(p_att3, n, attn_sink[l], ck, cv, l, rope_cos, rope_sin)
            else:
                y_att = _ctx_attention(p_att3, n, attn_sink[l])
                new_k.append(p_att3[:, :, ATTN_WIDTH:ATTN_WIDTH + KV_WIDTH]
                             .reshape(nb, n, N_KV_HEADS, HEAD_DIM))
                new_v.append(p_att3[:, :, ATTN_WIDTH + KV_WIDTH:].reshape(nb, n, N_KV_HEADS, HEAD_DIM))
            x = _out_proj(x, y_fh.reshape(nb * n, -1), y_att.reshape(nb * n, -1), m[:, 1], g[3:4],
                          w_out_b[l], rows_per_req)
            x = _ffn(x, m[:, 2], g[4:6], wg[l, 1], wu[l, 1], wd[l, 1], rows_per_req)
            xs[which] = x

    return (xs[0].reshape(batch, seq, d), xs[1].reshape(dec_batch, dec_seq, d),
            jnp.stack(new_k, axis=1), jnp.stack(new_v, axis=1))
```

```python
import functools
import math

import numpy as np
import jax
import jax.numpy as jnp
from jax import lax
from jax.experimental import pallas as pl
from jax.experimental.pallas import tpu as pltpu

F32 = jnp.float32
BF16 = jnp.bfloat16

D_MODEL = 1024
DEPTH = 2
GRID_W = 64
HEAD_DIM = 64
N_Q_HEADS = 8
N_KV_HEADS = 2
ATTN_WIDTH = N_Q_HEADS * HEAD_DIM
KV_WIDTH = N_KV_HEADS * HEAD_DIM
FOURIER_WIDTH = 256
FOURIER_GROUPS = 4
FOURIER_GROUP_DIM = 64
HYENA_WIDTH = 256
HYENA_ORDER = 2
HYENA_PROJ = 3 * HYENA_WIDTH
HYENA_EMB_DIM = 33
HYENA_BANDS = 16
HYENA_FILTER_WIDTH = 64
FH_WIDTH = FOURIER_WIDTH + HYENA_PROJ
ATT_WIDTH = ATTN_WIDTH + 2 * KV_WIDTH
IN_WIDTH = FH_WIDTH + ATT_WIDTH
BLOCK = 128
WINDOW = 128
ROPE_BASE = 10000.0
D_FF = 2816
N_SUB = 3
RMS_EPS = 1e-6

LANES = 128
MOD_ROWS = 16
VMEM_LIMIT = 56 * 1024 * 1024
NEG_BIG = -1e30


def _cparams(*sem):
    return pltpu.CompilerParams(dimension_semantics=sem, vmem_limit_bytes=VMEM_LIMIT)


def _rms(x, g):
    return x * lax.rsqrt(jnp.mean(x * x, axis=-1, keepdims=True) + RMS_EPS) * g


def _dot(a, b):
    return jnp.dot(a, b, preferred_element_type=F32)


def _dot_nt(a, b):
    return lax.dot_general(a, b, (((1,), (1,)), ((), ())), preferred_element_type=F32)


@functools.lru_cache(maxsize=None)
def _hyena_dft(n):
    f = np.arange(n, dtype=np.int64)[:, None]
    s = np.arange(n, dtype=np.int64)[None, :]
    ang = np.pi * ((f * s) % (2 * n)).astype(np.float64) / n
    c = np.cos(ang)
    sn = np.sin(ang)
    sn[0, :] = 1.0 - 2.0 * (np.arange(n) % 2)
    fw = np.concatenate([c, sn], axis=0).astype(np.float32)
    iw = np.concatenate([c, sn.T], axis=1).astype(np.float32)
    return fw, iw


@functools.lru_cache(maxsize=None)
def _fourier_tables(n):
    f = np.arange(n, dtype=np.int64)[:, None]
    s = np.arange(n, dtype=np.int64)[None, :]
    ang = 2.0 * np.pi * ((f * s) % n).astype(np.float64) / n
    pos = np.concatenate([np.cos(ang), -np.sin(ang)], axis=1) / math.sqrt(n)
    gd = FOURIER_GROUP_DIM
    a = np.arange(gd, dtype=np.int64)
    ang_c = 2.0 * np.pi * ((a[:, None] * a[None, :]) % gd).astype(np.float64) / gd
    bc = np.kron(np.eye(FOURIER_GROUPS), np.cos(ang_c)) / math.sqrt(gd)
    bs = np.kron(np.eye(FOURIER_GROUPS), np.sin(ang_c)) / math.sqrt(gd)
    chan = np.concatenate([bc, bs], axis=1)
    return pos.astype(np.float32), chan.astype(np.float32)


@functools.lru_cache(maxsize=None)
def _rope_tables(n):
    half = HEAD_DIM // 2
    inv = ROPE_BASE ** (-np.arange(0, half, 2, dtype=np.float64) / half)
    t = np.arange(n)
    row = (t // GRID_W).astype(np.float64)
    col = (t % GRID_W).astype(np.float64)
    lane = np.arange(LANES)
    d = lane % HEAD_DIM
    pos = np.where((d // half)[None, :] == 0, row[:, None], col[:, None])
    ang = pos * inv[d % (half // 2)][None, :]
    sign = np.where((d % half) < half // 2, -1.0, 1.0)[None, :]
    return np.cos(ang).astype(np.float32), (np.sin(ang) * sign).astype(np.float32)


def _hyena_feats(n):
    d = jnp.arange(n, dtype=F32)
    t = jnp.linspace(0.0, 1.0, n, dtype=F32)[:, None]
    f = jnp.linspace(1e-4, HYENA_BANDS - 1, HYENA_BANDS, dtype=F32)
    ang = (2.0 * math.pi / n) * d[:, None] * f[None, :]
    feats = jnp.concatenate([t, jnp.cos(ang), -jnp.sin(ang)], axis=-1)
    return jnp.pad(feats, ((0, 0), (0, LANES - HYENA_EMB_DIM))), t


def _mod_kernel(c_ref, w_ref, b_ref, o_ref):
    c = c_ref[...]
    s = c / (1.0 + jnp.exp(-c))
    o_ref[0] = _dot(s.astype(BF16), w_ref[0].astype(BF16)) + b_ref[0]


def _modulation(cvecs, w_mod, b_mod):
    depth, _, width = w_mod.shape
    tn = 1536
    return pl.pallas_call(
        _mod_kernel,
        grid=(depth, width // tn),
        in_specs=[
            pl.BlockSpec((MOD_ROWS, D_MODEL), lambda l, j: (0, 0)),
            pl.BlockSpec((1, D_MODEL, tn), lambda l, j: (l, 0, j)),
            pl.BlockSpec((1, 1, tn), lambda l, j: (l, 0, j)),
        ],
        out_specs=pl.BlockSpec((1, MOD_ROWS, tn), lambda l, j: (l, 0, j)),
        out_shape=jax.ShapeDtypeStruct((depth, MOD_ROWS, width), F32),
        compiler_params=_cparams("arbitrary", "arbitrary"),
        name="modulation",
    )(cvecs, w_mod, b_mod.reshape(depth, 1, width))


def _resident(block_shape, index_map):
    return pl.BlockSpec(block_shape, index_map, pipeline_mode=pl.Buffered(1))


def _ffn_kernel(x_ref, mod_ref, g_ref, wg_ref, wu_ref, wd_ref, o_ref):
    x = x_ref[...]
    h = (_rms(x, g_ref[0:1]) * (1.0 + mod_ref[0, 1:2]) + mod_ref[0, 0:1]).astype(BF16)
    gate = _dot(h, wg_ref[...])
    up = _dot(h, wu_ref[...])
    act = (gate / (1.0 + jnp.exp(-gate))) * up
    y = _dot(act.astype(BF16), wd_ref[...])
    o_ref[...] = x + (0.5 * mod_ref[0, 2:3]) * _rms(y, g_ref[1:2])


def _ffn(x, mod, g, wg, wu, wd, layer, which, rows_per_req, tm=512):
    t = x.shape[0]
    tiles_per_req = rows_per_req // tm
    return pl.pallas_call(
        _ffn_kernel,
        grid=(t // tm,),
        in_specs=[
            pl.BlockSpec((tm, D_MODEL), lambda i: (i, 0)),
            pl.BlockSpec((1, 3, D_MODEL), lambda i: (i // tiles_per_req, 0, 0)),
            _resident((2, D_MODEL), lambda i: (0, 0)),
            _resident((None, None, D_MODEL, D_FF), lambda i: (layer, which, 0, 0)),
            _resident((None, None, D_MODEL, D_FF), lambda i: (layer, which, 0, 0)),
            _resident((None, None, D_FF, D_MODEL), lambda i: (layer, which, 0, 0)),
        ],
        out_specs=pl.BlockSpec((tm, D_MODEL), lambda i: (i, 0)),
        out_shape=jax.ShapeDtypeStruct((t, D_MODEL), F32),
        compiler_params=_cparams("parallel"),
        name="ffn",
    )(x, mod, g, wg, wu, wd)


def _inproj_kernel(x_ref, mod_ref, g_ref, w_ref, fh_ref, q_ref, k_ref, v_ref):
    h = _rms(x_ref[...], g_ref[...]) * (1.0 + mod_ref[0, 1:2]) + mod_ref[0, 0:1]
    p = _dot(h.astype(BF16), w_ref[...])
    k0 = FH_WIDTH + ATTN_WIDTH
    fh_ref[...] = p[:, :FH_WIDTH]
    q_ref[...] = p[:, FH_WIDTH:k0]
    k_ref[...] = p[:, k0:k0 + KV_WIDTH]
    v_ref[...] = p[:, k0 + KV_WIDTH:]


def _in_proj(x, mod, g, w_in, layer, rows_per_req, tm=512):
    t = x.shape[0]
    tiles_per_req = rows_per_req // tm
    widths = (FH_WIDTH, ATTN_WIDTH, KV_WIDTH, KV_WIDTH)
    return pl.pallas_call(
        _inproj_kernel,
        grid=(t // tm,),
        in_specs=[
            pl.BlockSpec((tm, D_MODEL), lambda i: (i, 0)),
            pl.BlockSpec((1, 3, D_MODEL), lambda i: (i // tiles_per_req, 0, 0)),
            _resident((1, D_MODEL), lambda i: (0, 0)),
            _resident((None, D_MODEL, IN_WIDTH), lambda i: (layer, 0, 0)),
        ],
        out_specs=[pl.BlockSpec((tm, w), lambda i: (i, 0)) for w in widths],
        out_shape=[jax.ShapeDtypeStruct((t, w), F32) for w in widths],
        compiler_params=_cparams("parallel"),
        name="in_proj",
    )(x, mod, g, w_in)


def _outproj_kernel(x_ref, fh_ref, att_ref, mod_ref, g_ref, w_ref, o_ref):
    half = FOURIER_WIDTH + HYENA_WIDTH
    y = _dot(fh_ref[...], w_ref[:half]) + _dot(att_ref[...], w_ref[half:])
    o_ref[...] = x_ref[...] + mod_ref[0, 2:3] * _rms(y, g_ref[...])


def _out_proj(x, y_fh, y_att, mod, g, w_out, layer, rows_per_req, tm=512):
    t = x.shape[0]
    tiles_per_req = rows_per_req // tm
    half = FOURIER_WIDTH + HYENA_WIDTH
    return pl.pallas_call(
        _outproj_kernel,
        grid=(t // tm,),
        in_specs=[
            pl.BlockSpec((tm, D_MODEL), lambda i: (i, 0)),
            pl.BlockSpec((tm, half), lambda i: (i, 0)),
            pl.BlockSpec((tm, ATTN_WIDTH), lambda i: (i, 0)),
            pl.BlockSpec((1, 3, D_MODEL), lambda i: (i // tiles_per_req, 0, 0)),
            _resident((1, D_MODEL), lambda i: (0, 0)),
            _resident((None, D_MODEL, D_MODEL), lambda i: (layer, 0, 0)),
        ],
        out_specs=pl.BlockSpec((tm, D_MODEL), lambda i: (i, 0)),
        out_shape=jax.ShapeDtypeStruct((t, D_MODEL), F32),
        compiler_params=_cparams("parallel"),
        name="out_proj",
    )(x, y_fh, y_att, mod, g, w_out)


def _filter_kernel(feats_ref, t_ref, w1_ref, b1_ref, w2_ref, b2_ref, w3_ref, fr_ref, decay_ref,
                   fw_ref, o_ref, *, n):
    fr = fr_ref[...]
    h = jnp.sin(fr * (_dot(feats_ref[...], w1_ref[...]) + b1_ref[...]))
    h = jnp.sin(fr * (_dot(h, w2_ref[...]) + b2_ref[...]))
    h = _dot(h, w3_ref[...])
    window = jnp.exp(-t_ref[...] * jnp.abs(decay_ref[...]))
    width = HYENA_ORDER * HYENA_WIDTH
    row = lax.broadcasted_iota(jnp.int32, (n, width), 0)
    fwd = h[:, :width] * window
    bwd = jnp.where(row == 0, 0.0, h[:, width:] * window)
    even = fwd + bwd
    odd = fwd - bwd
    k_re = _dot(fw_ref[:n], even.astype(BF16))
    k_sn = _dot(fw_ref[n:], odd.astype(BF16))
    sign = (1 - 2 * (row % 2)).astype(F32)
    k_ny = jnp.sum(even * sign, axis=0, keepdims=True)
    s0 = 1.0 / (4.0 * n * n)
    scale = jnp.where(row == 0, s0, 2.0 * s0)
    a = scale * k_re
    o_ref[0] = a
    o_ref[1] = jnp.where(row == 0, 0.0, scale * k_sn)
    o_ref[2] = jnp.where(row == 0, s0 * k_ny, a)


def _filter_spectrum(n, fw, w1, b1, w2, b2, w3, freq, decay):
    feats, t = _hyena_feats(n)
    pad_w = LANES - HYENA_FILTER_WIDTH
    w1p = jnp.pad(w1, ((0, LANES - HYENA_EMB_DIM), (0, pad_w)))
    w2p = jnp.pad(w2, ((0, pad_w), (0, pad_w)))
    w3p = jnp.pad(w3, ((0, pad_w), (0, 0)))
    padv = lambda v: jnp.pad(v, (0, pad_w)).reshape(1, LANES)
    width = HYENA_ORDER * HYENA_WIDTH
    args = (feats, t, w1p, padv(b1), w2p, padv(b2), w3p, padv(freq), decay.reshape(1, width), fw)
    return pl.pallas_call(
        functools.partial(_filter_kernel, n=n),
        out_shape=jax.ShapeDtypeStruct((3, n, width), F32),
        compiler_params=pltpu.CompilerParams(vmem_limit_bytes=VMEM_LIMIT),
        name="hyena_filter",
    )(*args)


def _fh_kernel(p_ref, pos_ref, chan_ref, fw_ref, iw_ref, coef_ref, convw_ref, bias_ref, o_ref,
               st_scr, *, n):
    u = p_ref[0, :, :FOURIER_WIDTH].astype(BF16)
    t = _dot(u, chan_ref[...])
    st_scr[:n] = t[:, :FOURIER_WIDTH].astype(BF16)
    st_scr[n:] = t[:, FOURIER_WIDTH:].astype(BF16)
    o_ref[0, :, :FOURIER_WIDTH] = _dot(pos_ref[...], st_scr[...]).astype(o_ref.dtype)

    z = p_ref[0, :, FOURIER_WIDTH:]
    row = lax.broadcasted_iota(jnp.int32, z.shape, 0)
    z_prev = jnp.where(row == 0, 0.0, pltpu.roll(z, 1, axis=0))
    z_next = jnp.where(row == n - 1, 0.0, pltpu.roll(z, n - 1, axis=0))
    z = z_prev * convw_ref[0:1] + z * convw_ref[1:2] + z_next * convw_ref[2:3]
    c = HYENA_WIDTH

    def long_conv(v, order):
        cols = slice(order * c, (order + 1) * c)
        uf = _dot(fw_ref[...], v.astype(BF16))
        u_re, u_sn = uf[:n], uf[n:]
        a, b, a2 = coef_ref[0, :, cols], coef_ref[1, :, cols], coef_ref[2, :, cols]
        st_scr[:n] = (u_re * a - u_sn * b).astype(BF16)
        st_scr[n:] = (u_re * b + u_sn * a2).astype(BF16)
        return _dot(iw_ref[...], st_scr[...]) + v * bias_ref[order:order + 1]

    y = z[:, c:2 * c] * long_conv(z[:, :c], 0)
    y = z[:, 2 * c:] * long_conv(y, 1)
    o_ref[0, :, FOURIER_WIDTH:] = y.astype(o_ref.dtype)


def _fh_mix(p_fh, n, pos, chan, fw, iw, coef, conv_w, hbias):
    b = p_fh.shape[0]
    width = FOURIER_WIDTH + HYENA_WIDTH
    const = lambda shape: _resident(shape, lambda i: (0,) * len(shape))
    return pl.pallas_call(
        functools.partial(_fh_kernel, n=n),
        grid=(b,),
        in_specs=[
            pl.BlockSpec((1, n, FH_WIDTH), lambda i: (i, 0, 0)),
            const((n, 2 * n)), const((FOURIER_WIDTH, 2 * FOURIER_WIDTH)),
            const((2 * n, n)), const((n, 2 * n)),
            const((3, n, HYENA_ORDER * HYENA_WIDTH)),
            const((3, HYENA_PROJ)), const((HYENA_ORDER, HYENA_WIDTH)),
        ],
        out_specs=pl.BlockSpec((1, n, width), lambda i: (i, 0, 0)),
        out_shape=jax.ShapeDtypeStruct((b, n, width), BF16),
        scratch_shapes=[pltpu.VMEM((2 * n, HYENA_WIDTH), BF16)],
        compiler_params=_cparams("parallel"),
        name="fourier_hyena",
    )(p_fh, pos, chan, fw, iw, coef, conv_w, hbias)


def _head_slots(x, kv_head, lo):
    xr = pltpu.roll(x, HEAD_DIM, axis=1)
    if kv_head == 0:
        return jnp.where(lo, x, 0.0), jnp.where(lo, 0.0, xr)
    return jnp.where(lo, xr, 0.0), jnp.where(lo, 0.0, x)


def _ctx_attn_kernel(sink_ref, q_ref, k_ref, v_ref, o_ref, *, n):
    lo = lax.broadcasted_iota(jnp.int32, (n, LANES), 1) < HEAD_DIM
    k = k_ref[0]
    v = v_ref[0]
    for kv_head in range(N_KV_HEADS):
        k_slots = [s.astype(BF16) for s in _head_slots(k, kv_head, lo)]
        v_slots = [s.astype(BF16) for s in _head_slots(v, kv_head, lo)]
        for pair in range(2):
            tile = kv_head * 2 + pair
            q = (q_ref[0, :, tile * LANES:(tile + 1) * LANES] * HEAD_DIM ** -0.5).astype(BF16)
            acc = None
            for slot in range(2):
                sink = sink_ref[2 * tile + slot]
                s = _dot_nt(q, k_slots[slot])
                m = jnp.maximum(jnp.max(s, axis=-1, keepdims=True), sink)
                e = jnp.exp(s - m)
                den = jnp.sum(e, axis=-1, keepdims=True) + jnp.exp(sink - m)
                o = _dot(e.astype(BF16), v_slots[slot]) * (1.0 / den)
                acc = o if acc is None else acc + o
            o_ref[0, :, tile * LANES:(tile + 1) * LANES] = acc.astype(o_ref.dtype)


def _ctx_attention(q, k, v, n, sink):
    b = q.shape[0]
    return pl.pallas_call(
        functools.partial(_ctx_attn_kernel, n=n),
        grid=(b,),
        in_specs=[
            pl.BlockSpec(memory_space=pltpu.SMEM),
            pl.BlockSpec((1, n, ATTN_WIDTH), lambda i: (i, 0, 0)),
            pl.BlockSpec((1, n, KV_WIDTH), lambda i: (i, 0, 0)),
            pl.BlockSpec((1, n, KV_WIDTH), lambda i: (i, 0, 0)),
        ],
        out_specs=pl.BlockSpec((1, n, ATTN_WIDTH), lambda i: (i, 0, 0)),
        out_shape=jax.ShapeDtypeStruct((b, n, ATTN_WIDTH), BF16),
        compiler_params=_cparams("parallel"),
        name="ctx_attention",
    )(sink, q, k, v)


def _rope(x, cos, sin_signed):
    lane = lax.broadcasted_iota(jnp.int32, x.shape, 1)
    first = (lane % (HEAD_DIM // 2)) < HEAD_DIM // 4
    partner = jnp.where(first, pltpu.roll(x, LANES - HEAD_DIM // 4, axis=1),
                        pltpu.roll(x, HEAD_DIM // 4, axis=1))
    return x * cos + partner * sin_signed


def _lat_attn_kernel(sink_ref, q_ref, k_ref, v_ref, ck_ref, cv_ref, cos_ref, sin_ref, o_ref,
                     q_scr, k_scr, v_scr, ck_scr, cv_scr, *, n, c_len):
    nb = n // BLOCK
    cb = c_len // BLOCK
    grp = 2 * BLOCK
    cos, sin = cos_ref[...], sin_ref[...]
    lo = lax.broadcasted_iota(jnp.int32, (n, LANES), 1) < HEAD_DIM
    lo_c = lax.broadcasted_iota(jnp.int32, (c_len, LANES), 1) < HEAD_DIM

    k = _rope(k_ref[0], cos, sin)
    v = v_ref[0]
    for kv_head in range(N_KV_HEADS):
        for t in range(2):
            tile = 2 * kv_head + t
            q = (_rope(q_ref[0, :, tile * LANES:(tile + 1) * LANES], cos, sin) * HEAD_DIM ** -0.5).astype(BF16)
            for i in range(nb):
                q_scr[kv_head, i * grp + t * BLOCK:i * grp + (t + 1) * BLOCK] = q[i * BLOCK:(i + 1) * BLOCK]
        for slot, (ks, vs, cks, cvs) in enumerate(zip(
                _head_slots(k, kv_head, lo), _head_slots(v, kv_head, lo),
                _head_slots(ck_ref[0, 0], kv_head, lo_c), _head_slots(cv_ref[0, 0], kv_head, lo_c))):
            ks, vs, cks, cvs = (a.astype(BF16) for a in (ks, vs, cks, cvs))
            for j in range(nb):
                rows = slice(j * grp + slot * BLOCK, j * grp + (slot + 1) * BLOCK)
                k_scr[kv_head, rows] = ks[j * BLOCK:(j + 1) * BLOCK]
                v_scr[kv_head, rows] = vs[j * BLOCK:(j + 1) * BLOCK]
            for j in range(cb):
                rows = slice(j * grp + slot * BLOCK, j * grp + (slot + 1) * BLOCK)
                ck_scr[kv_head, rows] = cks[j * BLOCK:(j + 1) * BLOCK]
                cv_scr[kv_head, rows] = cvs[j * BLOCK:(j + 1) * BLOCK]

    qi = lax.broadcasted_iota(jnp.int32, (grp, BLOCK), 0) % BLOCK
    kj = lax.broadcasted_iota(jnp.int32, (grp, BLOCK), 1)
    keep_prev = kj >= qi
    keep_next = kj <= qi
    top = lax.broadcasted_iota(jnp.int32, (grp, 1), 0) < BLOCK
    lo_g = lax.broadcasted_iota(jnp.int32, (grp, LANES), 1) < HEAD_DIM

    def lane_tiles(s):
        return [s[:, c * BLOCK:(c + 1) * BLOCK] for c in range(s.shape[1] // BLOCK)]

    for kv_head in range(N_KV_HEADS):
        h0 = 4 * kv_head
        sinks = [jnp.where(top, sink_ref[h0 + slot], sink_ref[h0 + 2 + slot]) for slot in range(2)]
        for i in range(nb):
            j0, j1 = max(i - 1, 0), min(i + 2, nb)
            q2 = q_scr[kv_head, i * grp:(i + 1) * grp]
            tiles = lane_tiles(_dot_nt(q2, k_scr[kv_head, j0 * grp:j1 * grp]))
            for b, j in enumerate(range(j0, j1)):
                for slot in range(2):
                    if j == i - 1:
                        tiles[2 * b + slot] = jnp.where(keep_prev, tiles[2 * b + slot], NEG_BIG)
                    elif j == i + 1:
                        tiles[2 * b + slot] = jnp.where(keep_next, tiles[2 * b + slot], NEG_BIG)
            tiles += lane_tiles(_dot_nt(q2, ck_scr[kv_head]))
            inv_den = []
            for slot in range(2):
                mine = tiles[slot::2]
                m = jnp.maximum(jnp.max(functools.reduce(jnp.maximum, mine), axis=-1, keepdims=True),
                                sinks[slot])
                mine = [jnp.exp(tl - m) for tl in mine]
                tiles[slot::2] = mine
                den = jnp.sum(functools.reduce(jnp.add, mine), axis=-1, keepdims=True) + jnp.exp(sinks[slot] - m)
                inv_den.append(1.0 / den)
            n_loc = 2 * (j1 - j0)
            e_loc = jnp.concatenate(tiles[:n_loc], axis=1).astype(BF16)
            e_ctx = jnp.concatenate(tiles[n_loc:], axis=1).astype(BF16)
            o = _dot(e_loc, v_scr[kv_head, j0 * grp:j1 * grp]) + _dot(e_ctx, cv_scr[kv_head])
            o = (o * jnp.where(lo_g, inv_den[0], inv_den[1])).astype(o_ref.dtype)
            for t in range(2):
                tile = 2 * kv_head + t
                o_ref[0, i * BLOCK:(i + 1) * BLOCK, tile * LANES:(tile + 1) * LANES] = o[t * BLOCK:(t + 1) * BLOCK]


def _lat_attention(q, k, v, n, sink, cache_k, cache_v, layer, cos, sin):
    b = q.shape[0]
    c_len = cache_k.shape[2]
    return pl.pallas_call(
        functools.partial(_lat_attn_kernel, n=n, c_len=c_len),
        grid=(b,),
        in_specs=[
            pl.BlockSpec(memory_space=pltpu.SMEM),
            pl.BlockSpec((1, n, ATTN_WIDTH), lambda i: (i, 0, 0)),
            pl.BlockSpec((1, n, KV_WIDTH), lambda i: (i, 0, 0)),
            pl.BlockSpec((1, n, KV_WIDTH), lambda i: (i, 0, 0)),
            pl.BlockSpec((1, 1, c_len, KV_WIDTH), lambda i: (i, layer, 0, 0)),
            pl.BlockSpec((1, 1, c_len, KV_WIDTH), lambda i: (i, layer, 0, 0)),
            _resident((n, LANES), lambda i: (0, 0)),
            _resident((n, LANES), lambda i: (0, 0)),
        ],
        out_specs=pl.BlockSpec((1, n, ATTN_WIDTH), lambda i: (i, 0, 0)),
        out_shape=jax.ShapeDtypeStruct((b, n, ATTN_WIDTH), BF16),
        scratch_shapes=[
            pltpu.VMEM((N_KV_HEADS, 2 * n, LANES), BF16),
            pltpu.VMEM((N_KV_HEADS, 2 * n, LANES), BF16),
            pltpu.VMEM((N_KV_HEADS, 2 * n, LANES), BF16),
            pltpu.VMEM((N_KV_HEADS, 2 * c_len, LANES), BF16),
            pltpu.VMEM((N_KV_HEADS, 2 * c_len, LANES), BF16),
        ],
        compiler_params=_cparams("parallel"),
        name="latent_attention",
    )(sink, q, k, v, cache_k, cache_v, cos, sin)


def kernel(x_prompt, x_sample, cache_k, cache_v, c, c_ctx, w_mod, b_mod, norm_g, ffn_w_gate, ffn_w_up,
           ffn_w_down, w_in, w_out, hyena_conv_w, hyena_f_w1, hyena_f_b1, hyena_f_w2, hyena_f_b2,
           hyena_f_w3, hyena_f_freq, hyena_decay, hyena_bias, attn_sink):
    batch, seq, d = x_prompt.shape
    dec_batch, dec_seq, _ = x_sample.shape
    past_len = cache_k.shape[2]

    cvecs = jnp.concatenate([c_ctx[None], c], axis=0)
    cvecs = jnp.pad(cvecs, ((0, MOD_ROWS - cvecs.shape[0]), (0, 0)))
    mod = _modulation(cvecs, w_mod, b_mod).reshape(DEPTH, MOD_ROWS, N_SUB, 3, d)

    wg = ffn_w_gate.astype(BF16)
    wu = ffn_w_up.astype(BF16)
    wd = ffn_w_down.astype(BF16)
    w_in_b = w_in.astype(BF16)
    w_out_b = w_out.astype(BF16)
    ck = cache_k.reshape(dec_batch, DEPTH, past_len, KV_WIDTH)
    cv = cache_v.reshape(dec_batch, DEPTH, past_len, KV_WIDTH)
    rope_cos, rope_sin = (jnp.asarray(t) for t in _rope_tables(dec_seq))

    passes = []
    for n, nreq in ((seq, 1), (dec_seq, dec_batch)):
        fw, iw = (jnp.asarray(t).astype(BF16) for t in _hyena_dft(n))
        pos, chan = (jnp.asarray(t).astype(BF16) for t in _fourier_tables(n))
        passes.append((n, nreq, fw, iw, pos, chan))

    xs = [x_prompt.reshape(batch * seq, d), x_sample.reshape(dec_batch * dec_seq, d)]
    new_k, new_v = [], []
    for l in range(DEPTH):
        g = norm_g[l]
        for which, (n, nreq, fw, iw, pos, chan) in enumerate(passes):
            latent = which == 1
            x = xs[which]
            nb = x.shape[0] // n
            m = mod[l, 1:1 + nreq] if latent else mod[l, 0:1]
            rows_per_req = n if latent else x.shape[0]
            x = _ffn(x, m[:, 0], g[0:2], wg, wu, wd, l, 0, rows_per_req)
            p_fh, q, k, v = _in_proj(x, m[:, 1], g[2:3], w_in_b, l, rows_per_req)
            coef = _filter_spectrum(n, fw, hyena_f_w1[l], hyena_f_b1[l], hyena_f_w2[l], hyena_f_b2[l],
                                    hyena_f_w3[l], hyena_f_freq[l], hyena_decay[l])
            y_fh = _fh_mix(p_fh.reshape(nb, n, FH_WIDTH), n, pos, chan, fw, iw, coef,
                           hyena_conv_w[l], hyena_bias[l])
            q, k, v = (a.reshape(nb, n, -1) for a in (q, k, v))
            if latent:
                y_att = _lat_attention(q, k, v, n, attn_sink[l], ck, cv, l, rope_cos, rope_sin)
            else:
                y_att = _ctx_attention(q, k, v, n, attn_sink[l])
                new_k.append(k.reshape(nb, n, N_KV_HEADS, HEAD_DIM))
                new_v.append(v.reshape(nb, n, N_KV_HEADS, HEAD_DIM))
            x = _out_proj(x, y_fh.reshape(nb * n, -1), y_att.reshape(nb * n, -1), m[:, 1], g[3:4],
                          w_out_b, l, rows_per_req)
            x = _ffn(x, m[:, 2], g[4:6], wg, wu, wd, l, 1, rows_per_req)
            xs[which] = x

    return (xs[0].reshape(batch, seq, d), xs[1].reshape(dec_batch, dec_seq, d),
            jnp.stack(new_k, axis=1), jnp.stack(new_v, axis=1))
```

```python
import functools
import math

import numpy as np
import jax
import jax.numpy as jnp
from jax import lax
from jax.experimental import pallas as pl
from jax.experimental.pallas import tpu as pltpu

F32 = jnp.float32
BF16 = jnp.bfloat16

D_MODEL = 1024
DEPTH = 2
GRID_W = 64
HEAD_DIM = 64
N_Q_HEADS = 8
N_KV_HEADS = 2
ATTN_WIDTH = N_Q_HEADS * HEAD_DIM
KV_WIDTH = N_KV_HEADS * HEAD_DIM
FOURIER_WIDTH = 256
FOURIER_GROUPS = 4
FOURIER_GROUP_DIM = 64
HYENA_WIDTH = 256
HYENA_ORDER = 2
HYENA_PROJ = 3 * HYENA_WIDTH
HYENA_EMB_DIM = 33
HYENA_BANDS = 16
HYENA_FILTER_WIDTH = 64
FH_WIDTH = FOURIER_WIDTH + HYENA_PROJ
ATT_WIDTH = ATTN_WIDTH + 2 * KV_WIDTH
IN_WIDTH = FH_WIDTH + ATT_WIDTH
BLOCK = 128
WINDOW = 128
ROPE_BASE = 10000.0
D_FF = 2816
N_SUB = 3
RMS_EPS = 1e-6

LANES = 128
MOD_ROWS = 16
VMEM_LIMIT = 56 * 1024 * 1024
NEG_BIG = -1e30


def _cparams(*sem):
    return pltpu.CompilerParams(dimension_semantics=sem, vmem_limit_bytes=VMEM_LIMIT)


def _rms(x, g):
    return x * lax.rsqrt(jnp.mean(x * x, axis=-1, keepdims=True) + RMS_EPS) * g


def _dot(a, b):
    return jnp.dot(a, b, preferred_element_type=F32)


def _dot_nt(a, b):
    return lax.dot_general(a, b, (((1,), (1,)), ((), ())), preferred_element_type=F32)


@functools.lru_cache(maxsize=None)
def _hyena_dft(n):
    f = np.arange(n, dtype=np.int64)[:, None]
    s = np.arange(n, dtype=np.int64)[None, :]
    ang = np.pi * ((f * s) % (2 * n)).astype(np.float64) / n
    c = np.cos(ang)
    sn = np.sin(ang)
    sn[0, :] = 1.0 - 2.0 * (np.arange(n) % 2)
    fw = np.concatenate([c, sn], axis=0).astype(np.float32)
    iw = np.concatenate([c, sn.T], axis=1).astype(np.float32)
    return fw, iw


@functools.lru_cache(maxsize=None)
def _fourier_tables(n):
    f = np.arange(n, dtype=np.int64)[:, None]
    s = np.arange(n, dtype=np.int64)[None, :]
    ang = 2.0 * np.pi * ((f * s) % n).astype(np.float64) / n
    pos = np.concatenate([np.cos(ang), -np.sin(ang)], axis=1) / math.sqrt(n)
    gd = FOURIER_GROUP_DIM
    a = np.arange(gd, dtype=np.int64)
    ang_c = 2.0 * np.pi * ((a[:, None] * a[None, :]) % gd).astype(np.float64) / gd
    bc = np.kron(np.eye(FOURIER_GROUPS), np.cos(ang_c)) / math.sqrt(gd)
    bs = np.kron(np.eye(FOURIER_GROUPS), np.sin(ang_c)) / math.sqrt(gd)
    chan = np.concatenate([bc, bs], axis=1)
    return pos.astype(np.float32), chan.astype(np.float32)


@functools.lru_cache(maxsize=None)
def _rope_tables(n):
    half = HEAD_DIM // 2
    inv = ROPE_BASE ** (-np.arange(0, half, 2, dtype=np.float64) / half)
    t = np.arange(n)
    row = (t // GRID_W).astype(np.float64)
    col = (t % GRID_W).astype(np.float64)
    lane = np.arange(LANES)
    d = lane % HEAD_DIM
    pos = np.where((d // half)[None, :] == 0, row[:, None], col[:, None])
    ang = pos * inv[d % (half // 2)][None, :]
    sign = np.where((d % half) < half // 2, -1.0, 1.0)[None, :]
    return np.cos(ang).astype(np.float32), (np.sin(ang) * sign).astype(np.float32)


def _hyena_feats(n):
    d = jnp.arange(n, dtype=F32)
    t = jnp.linspace(0.0, 1.0, n, dtype=F32)[:, None]
    f = jnp.linspace(1e-4, HYENA_BANDS - 1, HYENA_BANDS, dtype=F32)
    ang = (2.0 * math.pi / n) * d[:, None] * f[None, :]
    feats = jnp.concatenate([t, jnp.cos(ang), -jnp.sin(ang)], axis=-1)
    return jnp.pad(feats, ((0, 0), (0, LANES - HYENA_EMB_DIM))), t


def _mod_kernel(c_ref, w_ref, b_ref, o_ref):
    c = c_ref[...]
    s = c / (1.0 + jnp.exp(-c))
    o_ref[0] = _dot(s.astype(BF16), w_ref[0].astype(BF16)) + b_ref[0]


def _modulation(cvecs, w_mod, b_mod):
    depth, _, width = w_mod.shape
    tn = 1536
    return pl.pallas_call(
        _mod_kernel,
        grid=(depth, width // tn),
        in_specs=[
            pl.BlockSpec((MOD_ROWS, D_MODEL), lambda l, j: (0, 0)),
            pl.BlockSpec((1, D_MODEL, tn), lambda l, j: (l, 0, j)),
            pl.BlockSpec((1, 1, tn), lambda l, j: (l, 0, j)),
        ],
        out_specs=pl.BlockSpec((1, MOD_ROWS, tn), lambda l, j: (l, 0, j)),
        out_shape=jax.ShapeDtypeStruct((depth, MOD_ROWS, width), F32),
        compiler_params=_cparams("arbitrary", "arbitrary"),
        name="modulation",
    )(cvecs, w_mod, b_mod.reshape(depth, 1, width))


def _resident(block_shape, index_map):
    return pl.BlockSpec(block_shape, index_map, pipeline_mode=pl.Buffered(1))


def _ffn_kernel(x_ref, mod_ref, g_ref, wg_ref, wu_ref, wd_ref, o_ref, *, parts):
    rows = x_ref.shape[0] // parts
    for part in range(parts):
        sl = slice(part * rows, (part + 1) * rows)
        x = x_ref[sl]
        h = (_rms(x, g_ref[0:1]) * (1.0 + mod_ref[0, 1:2]) + mod_ref[0, 0:1]).astype(BF16)
        gate = _dot(h, wg_ref[...])
        up = _dot(h, wu_ref[...])
        act = (gate / (1.0 + jnp.exp(-gate))) * up
        y = _dot(act.astype(BF16), wd_ref[...])
        o_ref[sl] = x + (0.5 * mod_ref[0, 2:3]) * _rms(y, g_ref[1:2])


def _ffn(x, mod, g, wg, wu, wd, layer, which, rows_per_req, tm=1024, parts=4):
    t = x.shape[0]
    tiles_per_req = rows_per_req // tm
    return pl.pallas_call(
        functools.partial(_ffn_kernel, parts=parts),
        grid=(t // tm,),
        in_specs=[
            pl.BlockSpec((tm, D_MODEL), lambda i: (i, 0)),
            pl.BlockSpec((1, 3, D_MODEL), lambda i: (i // tiles_per_req, 0, 0)),
            _resident((2, D_MODEL), lambda i: (0, 0)),
            _resident((None, None, D_MODEL, D_FF), lambda i: (layer, which, 0, 0)),
            _resident((None, None, D_MODEL, D_FF), lambda i: (layer, which, 0, 0)),
            _resident((None, None, D_FF, D_MODEL), lambda i: (layer, which, 0, 0)),
        ],
        out_specs=pl.BlockSpec((tm, D_MODEL), lambda i: (i, 0)),
        out_shape=jax.ShapeDtypeStruct((t, D_MODEL), F32),
        compiler_params=_cparams("parallel"),
        name="ffn",
    )(x, mod, g, wg, wu, wd)


def _inproj_kernel(x_ref, mod_ref, g_ref, w_ref, fh_ref, q_ref, k_ref, v_ref):
    h = _rms(x_ref[...], g_ref[...]) * (1.0 + mod_ref[0, 1:2]) + mod_ref[0, 0:1]
    p = _dot(h.astype(BF16), w_ref[...])
    k0 = FH_WIDTH + ATTN_WIDTH
    fh_ref[...] = p[:, :FH_WIDTH].astype(fh_ref.dtype)
    q_ref[...] = p[:, FH_WIDTH:k0].astype(q_ref.dtype)
    k_ref[...] = p[:, k0:k0 + KV_WIDTH]
    v_ref[...] = p[:, k0 + KV_WIDTH:]


def _in_proj(x, mod, g, w_in, layer, rows_per_req, tm=512):
    t = x.shape[0]
    tiles_per_req = rows_per_req // tm
    widths = (FH_WIDTH, ATTN_WIDTH, KV_WIDTH, KV_WIDTH)
    dtypes = (BF16, BF16, F32, F32)
    return pl.pallas_call(
        _inproj_kernel,
        grid=(t // tm,),
        in_specs=[
            pl.BlockSpec((tm, D_MODEL), lambda i: (i, 0)),
            pl.BlockSpec((1, 3, D_MODEL), lambda i: (i // tiles_per_req, 0, 0)),
            _resident((1, D_MODEL), lambda i: (0, 0)),
            _resident((None, D_MODEL, IN_WIDTH), lambda i: (layer, 0, 0)),
        ],
        out_specs=[pl.BlockSpec((tm, w), lambda i: (i, 0)) for w in widths],
        out_shape=[jax.ShapeDtypeStruct((t, w), dt) for w, dt in zip(widths, dtypes)],
        compiler_params=_cparams("parallel"),
        name="in_proj",
    )(x, mod, g, w_in)


def _outproj_kernel(x_ref, fh_ref, att_ref, mod_ref, g_ref, w_ref, o_ref):
    half = FOURIER_WIDTH + HYENA_WIDTH
    y = _dot(fh_ref[...], w_ref[:half]) + _dot(att_ref[...], w_ref[half:])
    o_ref[...] = x_ref[...] + mod_ref[0, 2:3] * _rms(y, g_ref[...])


def _out_proj(x, y_fh, y_att, mod, g, w_out, layer, rows_per_req, tm=512):
    t = x.shape[0]
    tiles_per_req = rows_per_req // tm
    half = FOURIER_WIDTH + HYENA_WIDTH
    return pl.pallas_call(
        _outproj_kernel,
        grid=(t // tm,),
        in_specs=[
            pl.BlockSpec((tm, D_MODEL), lambda i: (i, 0)),
            pl.BlockSpec((tm, half), lambda i: (i, 0)),
            pl.BlockSpec((tm, ATTN_WIDTH), lambda i: (i, 0)),
            pl.BlockSpec((1, 3, D_MODEL), lambda i: (i // tiles_per_req, 0, 0)),
            _resident((1, D_MODEL), lambda i: (0, 0)),
            _resident((None, D_MODEL, D_MODEL), lambda i: (layer, 0, 0)),
        ],
        out_specs=pl.BlockSpec((tm, D_MODEL), lambda i: (i, 0)),
        out_shape=jax.ShapeDtypeStruct((t, D_MODEL), F32),
        compiler_params=_cparams("parallel"),
        name="out_proj",
    )(x, y_fh, y_att, mod, g, w_out)


def _filter_kernel(feats_ref, t_ref, w1_ref, b1_ref, w2_ref, b2_ref, w3_ref, fr_ref, decay_ref,
                   fw_ref, o_ref, *, n):
    fr = fr_ref[...]
    h = jnp.sin(fr * (_dot(feats_ref[...], w1_ref[...]) + b1_ref[...]))
    h = jnp.sin(fr * (_dot(h, w2_ref[...]) + b2_ref[...]))
    h = _dot(h, w3_ref[...])
    window = jnp.exp(-t_ref[...] * jnp.abs(decay_ref[...]))
    width = HYENA_ORDER * HYENA_WIDTH
    row = lax.broadcasted_iota(jnp.int32, (n, width), 0)
    fwd = h[:, :width] * window
    bwd = jnp.where(row == 0, 0.0, h[:, width:] * window)
    even = fwd + bwd
    odd = fwd - bwd
    k_re = _dot(fw_ref[:n], even.astype(BF16))
    k_sn = _dot(fw_ref[n:], odd.astype(BF16))
    sign = (1 - 2 * (row % 2)).astype(F32)
    k_ny = jnp.sum(even * sign, axis=0, keepdims=True)
    s0 = 1.0 / (4.0 * n * n)
    scale = jnp.where(row == 0, s0, 2.0 * s0)
    a = scale * k_re
    o_ref[0] = a
    o_ref[1] = jnp.where(row == 0, 0.0, scale * k_sn)
    o_ref[2] = jnp.where(row == 0, s0 * k_ny, a)


def _filter_spectrum(n, fw, w1, b1, w2, b2, w3, freq, decay):
    feats, t = _hyena_feats(n)
    pad_w = LANES - HYENA_FILTER_WIDTH
    w1p = jnp.pad(w1, ((0, LANES - HYENA_EMB_DIM), (0, pad_w)))
    w2p = jnp.pad(w2, ((0, pad_w), (0, pad_w)))
    w3p = jnp.pad(w3, ((0, pad_w), (0, 0)))
    padv = lambda v: jnp.pad(v, (0, pad_w)).reshape(1, LANES)
    width = HYENA_ORDER * HYENA_WIDTH
    args = (feats, t, w1p, padv(b1), w2p, padv(b2), w3p, padv(freq), decay.reshape(1, width), fw)
    return pl.pallas_call(
        functools.partial(_filter_kernel, n=n),
        out_shape=jax.ShapeDtypeStruct((3, n, width), F32),
        compiler_params=pltpu.CompilerParams(vmem_limit_bytes=VMEM_LIMIT),
        name="hyena_filter",
    )(*args)


def _fh_kernel(p_ref, pos_ref, chan_ref, fw_ref, iw_ref, coef_ref, convw_ref, bias_ref, o_ref,
               st_scr, *, n, group):
    c = HYENA_WIDTH
    cols = [slice(b * c, (b + 1) * c) for b in range(group)]

    for b in range(group):
        t = _dot(p_ref[b, :, :FOURIER_WIDTH], chan_ref[...])
        st_scr[:n, cols[b]] = t[:, :FOURIER_WIDTH].astype(BF16)
        st_scr[n:, cols[b]] = t[:, FOURIER_WIDTH:].astype(BF16)
    y_f = _dot(pos_ref[...], st_scr[...])
    for b in range(group):
        o_ref[b, :, :FOURIER_WIDTH] = y_f[:, cols[b]].astype(o_ref.dtype)

    row = lax.broadcasted_iota(jnp.int32, (n, HYENA_PROJ), 0)
    zs = []
    for b in range(group):
        z = p_ref[b, :, FOURIER_WIDTH:].astype(F32)
        z_prev = jnp.where(row == 0, 0.0, pltpu.roll(z, 1, axis=0))
        z_next = jnp.where(row == n - 1, 0.0, pltpu.roll(z, n - 1, axis=0))
        zs.append(z_prev * convw_ref[0:1] + z * convw_ref[1:2] + z_next * convw_ref[2:3])

    def long_conv(vs, order):
        oc = slice(order * c, (order + 1) * c)
        uf = _dot(fw_ref[...], jnp.concatenate([v.astype(BF16) for v in vs], axis=1))
        a, bb, a2 = coef_ref[0, :, oc], coef_ref[1, :, oc], coef_ref[2, :, oc]
        for b in range(group):
            u_re, u_sn = uf[:n, cols[b]], uf[n:, cols[b]]
            st_scr[:n, cols[b]] = (u_re * a - u_sn * bb).astype(BF16)
            st_scr[n:, cols[b]] = (u_re * bb + u_sn * a2).astype(BF16)
        y = _dot(iw_ref[...], st_scr[...])
        return [y[:, cols[b]] + vs[b] * bias_ref[order:order + 1] for b in range(group)]

    ys = long_conv([z[:, :c] for z in zs], 0)
    ys = long_conv([z[:, c:2 * c] * y for z, y in zip(zs, ys)], 1)
    for b in range(group):
        o_ref[b, :, FOURIER_WIDTH:] = (zs[b][:, 2 * c:] * ys[b]).astype(o_ref.dtype)


def _fh_mix(p_fh, n, group, pos, chan, fw, iw, coef, conv_w, hbias):
    b = p_fh.shape[0]
    width = FOURIER_WIDTH + HYENA_WIDTH
    const = lambda shape: _resident(shape, lambda i: (0,) * len(shape))
    return pl.pallas_call(
        functools.partial(_fh_kernel, n=n, group=group),
        grid=(b // group,),
        in_specs=[
            pl.BlockSpec((group, n, FH_WIDTH), lambda i: (i, 0, 0)),
            const((n, 2 * n)), const((FOURIER_WIDTH, 2 * FOURIER_WIDTH)),
            const((2 * n, n)), const((n, 2 * n)),
            const((3, n, HYENA_ORDER * HYENA_WIDTH)),
            const((3, HYENA_PROJ)), const((HYENA_ORDER, HYENA_WIDTH)),
        ],
        out_specs=pl.BlockSpec((group, n, width), lambda i: (i, 0, 0)),
        out_shape=jax.ShapeDtypeStruct((b, n, width), BF16),
        scratch_shapes=[pltpu.VMEM((2 * n, group * HYENA_WIDTH), BF16)],
        compiler_params=_cparams("parallel"),
        name="fourier_hyena",
    )(p_fh, pos, chan, fw, iw, coef, conv_w, hbias)


def _head_slots(x, kv_head, lo):
    xr = pltpu.roll(x, HEAD_DIM, axis=1)
    if kv_head == 0:
        return jnp.where(lo, x, 0.0), jnp.where(lo, 0.0, xr)
    return jnp.where(lo, xr, 0.0), jnp.where(lo, 0.0, x)


def _ctx_attn_kernel(sink_ref, q_ref, k_ref, v_ref, o_ref, *, n, group):
    lo = lax.broadcasted_iota(jnp.int32, (n, LANES), 1) < HEAD_DIM
    top = lax.broadcasted_iota(jnp.int32, (2 * n, 1), 0) < n
    lo_g = lax.broadcasted_iota(jnp.int32, (2 * n, LANES), 1) < HEAD_DIM
    for b in range(group):
        k = k_ref[b]
        v = v_ref[b]
        for kv_head in range(N_KV_HEADS):
            h0 = 4 * kv_head
            k_cat = jnp.concatenate(_head_slots(k, kv_head, lo), axis=0).astype(BF16)
            v_cat = jnp.concatenate(_head_slots(v, kv_head, lo), axis=0).astype(BF16)
            q2 = jnp.concatenate([q_ref[b, :, (2 * kv_head + t) * LANES:(2 * kv_head + t + 1) * LANES]
                                  for t in range(2)], axis=0) * HEAD_DIM ** -0.5
            s = _dot_nt(q2, k_cat)
            e, inv_den = [], []
            for slot in range(2):
                sink = jnp.where(top, sink_ref[h0 + slot], sink_ref[h0 + 2 + slot])
                cols = s[:, slot * n:(slot + 1) * n]
                m = jnp.maximum(jnp.max(cols, axis=-1, keepdims=True), sink)
                cols = jnp.exp(cols - m)
                e.append(cols.astype(BF16))
                inv_den.append(1.0 / (jnp.sum(cols, axis=-1, keepdims=True) + jnp.exp(sink - m)))
            o = _dot(jnp.concatenate(e, axis=1), v_cat) * jnp.where(lo_g, inv_den[0], inv_den[1])
            for t in range(2):
                tile = 2 * kv_head + t
                o_ref[b, :, tile * LANES:(tile + 1) * LANES] = o[t * n:(t + 1) * n].astype(o_ref.dtype)


def _ctx_attention(q, k, v, n, sink, group=4):
    b = q.shape[0]
    return pl.pallas_call(
        functools.partial(_ctx_attn_kernel, n=n, group=group),
        grid=(b // group,),
        in_specs=[
            pl.BlockSpec(memory_space=pltpu.SMEM),
            pl.BlockSpec((group, n, ATTN_WIDTH), lambda i: (i, 0, 0)),
            pl.BlockSpec((group, n, KV_WIDTH), lambda i: (i, 0, 0)),
            pl.BlockSpec((group, n, KV_WIDTH), lambda i: (i, 0, 0)),
        ],
        out_specs=pl.BlockSpec((group, n, ATTN_WIDTH), lambda i: (i, 0, 0)),
        out_shape=jax.ShapeDtypeStruct((b, n, ATTN_WIDTH), BF16),
        compiler_params=_cparams("parallel"),
        name="ctx_attention",
    )(sink, q, k, v)


def _rope(x, cos, sin_signed):
    lane = lax.broadcasted_iota(jnp.int32, x.shape, 1)
    first = (lane % (HEAD_DIM // 2)) < HEAD_DIM // 4
    partner = jnp.where(first, pltpu.roll(x, LANES - HEAD_DIM // 4, axis=1),
                        pltpu.roll(x, HEAD_DIM // 4, axis=1))
    return x * cos + partner * sin_signed


def _lat_attn_kernel(sink_ref, q_ref, k_ref, v_ref, ck_ref, cv_ref, cos_ref, sin_ref, o_ref,
                     q_scr, k_scr, v_scr, ck_scr, cv_scr, *, n, c_len):
    nb = n // BLOCK
    cb = c_len // BLOCK
    grp = 2 * BLOCK
    cos, sin = cos_ref[...], sin_ref[...]
    lo = lax.broadcasted_iota(jnp.int32, (n, LANES), 1) < HEAD_DIM
    lo_c = lax.broadcasted_iota(jnp.int32, (c_len, LANES), 1) < HEAD_DIM

    k = _rope(k_ref[0], cos, sin)
    v = v_ref[0]
    for kv_head in range(N_KV_HEADS):
        for t in range(2):
            tile = 2 * kv_head + t
            q = q_ref[0, :, tile * LANES:(tile + 1) * LANES].astype(F32)
            q = (_rope(q, cos, sin) * HEAD_DIM ** -0.5).astype(BF16)
            for i in range(nb):
                q_scr[kv_head, i * grp + t * BLOCK:i * grp + (t + 1) * BLOCK] = q[i * BLOCK:(i + 1) * BLOCK]
        for slot, (ks, vs, cks, cvs) in enumerate(zip(
                _head_slots(k, kv_head, lo), _head_slots(v, kv_head, lo),
                _head_slots(ck_ref[0, 0], kv_head, lo_c), _head_slots(cv_ref[0, 0], kv_head, lo_c))):
            ks, vs, cks, cvs = (a.astype(BF16) for a in (ks, vs, cks, cvs))
            for j in range(nb):
                rows = slice(j * grp + slot * BLOCK, j * grp + (slot + 1) * BLOCK)
                k_scr[kv_head, rows] = ks[j * BLOCK:(j + 1) * BLOCK]
                v_scr[kv_head, rows] = vs[j * BLOCK:(j + 1) * BLOCK]
            for j in range(cb):
                rows = slice(j * grp + slot * BLOCK, j * grp + (slot + 1) * BLOCK)
                ck_scr[kv_head, rows] = cks[j * BLOCK:(j + 1) * BLOCK]
                cv_scr[kv_head, rows] = cvs[j * BLOCK:(j + 1) * BLOCK]

    qi = lax.broadcasted_iota(jnp.int32, (grp, BLOCK), 0) % BLOCK
    kj = lax.broadcasted_iota(jnp.int32, (grp, BLOCK), 1)
    keep_prev = kj >= qi
    keep_next = kj <= qi
    top = lax.broadcasted_iota(jnp.int32, (grp, 1), 0) < BLOCK
    lo_g = lax.broadcasted_iota(jnp.int32, (grp, LANES), 1) < HEAD_DIM

    def lane_tiles(s):
        return [s[:, c * BLOCK:(c + 1) * BLOCK] for c in range(s.shape[1] // BLOCK)]

    for kv_head in range(N_KV_HEADS):
        h0 = 4 * kv_head
        sinks = [jnp.where(top, sink_ref[h0 + slot], sink_ref[h0 + 2 + slot]) for slot in range(2)]
        for i in range(nb):
            j0, j1 = max(i - 1, 0), min(i + 2, nb)
            q2 = q_scr[kv_head, i * grp:(i + 1) * grp]
            tiles = lane_tiles(_dot_nt(q2, k_scr[kv_head, j0 * grp:j1 * grp]))
            for b, j in enumerate(range(j0, j1)):
                for slot in range(2):
                    if j == i - 1:
                        tiles[2 * b + slot] = jnp.where(keep_prev, tiles[2 * b + slot], NEG_BIG)
                    elif j == i + 1:
                        tiles[2 * b + slot] = jnp.where(keep_next, tiles[2 * b + slot], NEG_BIG)
            tiles += lane_tiles(_dot_nt(q2, ck_scr[kv_head]))
            inv_den = []
            for slot in range(2):
                mine = tiles[slot::2]
                m = jnp.maximum(jnp.max(functools.reduce(jnp.maximum, mine), axis=-1, keepdims=True),
                                sinks[slot])
                mine = [jnp.exp(tl - m) for tl in mine]
                tiles[slot::2] = mine
                den = jnp.sum(functools.reduce(jnp.add, mine), axis=-1, keepdims=True) + jnp.exp(sinks[slot] - m)
                inv_den.append(1.0 / den)
            n_loc = 2 * (j1 - j0)
            e_loc = jnp.concatenate(tiles[:n_loc], axis=1).astype(BF16)
            e_ctx = jnp.concatenate(tiles[n_loc:], axis=1).astype(BF16)
            o = _dot(e_loc, v_scr[kv_head, j0 * grp:j1 * grp]) + _dot(e_ctx, cv_scr[kv_head])
            o = (o * jnp.where(lo_g, inv_den[0], inv_den[1])).astype(o_ref.dtype)
            for t in range(2):
                tile = 2 * kv_head + t
                o_ref[0, i * BLOCK:(i + 1) * BLOCK, tile * LANES:(tile + 1) * LANES] = o[t * BLOCK:(t + 1) * BLOCK]


def _lat_attention(q, k, v, n, sink, cache_k, cache_v, layer, cos, sin):
    b = q.shape[0]
    c_len = cache_k.shape[2]
    return pl.pallas_call(
        functools.partial(_lat_attn_kernel, n=n, c_len=c_len),
        grid=(b,),
        in_specs=[
            pl.BlockSpec(memory_space=pltpu.SMEM),
            pl.BlockSpec((1, n, ATTN_WIDTH), lambda i: (i, 0, 0)),
            pl.BlockSpec((1, n, KV_WIDTH), lambda i: (i, 0, 0)),
            pl.BlockSpec((1, n, KV_WIDTH), lambda i: (i, 0, 0)),
            pl.BlockSpec((1, 1, c_len, KV_WIDTH), lambda i: (i, layer, 0, 0)),
            pl.BlockSpec((1, 1, c_len, KV_WIDTH), lambda i: (i, layer, 0, 0)),
            _resident((n, LANES), lambda i: (0, 0)),
            _resident((n, LANES), lambda i: (0, 0)),
        ],
        out_specs=pl.BlockSpec((1, n, ATTN_WIDTH), lambda i: (i, 0, 0)),
        out_shape=jax.ShapeDtypeStruct((b, n, ATTN_WIDTH), BF16),
        scratch_shapes=[
            pltpu.VMEM((N_KV_HEADS, 2 * n, LANES), BF16),
            pltpu.VMEM((N_KV_HEADS, 2 * n, LANES), BF16),
            pltpu.VMEM((N_KV_HEADS, 2 * n, LANES), BF16),
            pltpu.VMEM((N_KV_HEADS, 2 * c_len, LANES), BF16),
            pltpu.VMEM((N_KV_HEADS, 2 * c_len, LANES), BF16),
        ],
        compiler_params=_cparams("parallel"),
        name="latent_attention",
    )(sink, q, k, v, cache_k, cache_v, cos, sin)


def kernel(x_prompt, x_sample, cache_k, cache_v, c, c_ctx, w_mod, b_mod, norm_g, ffn_w_gate, ffn_w_up,
           ffn_w_down, w_in, w_out, hyena_conv_w, hyena_f_w1, hyena_f_b1, hyena_f_w2, hyena_f_b2,
           hyena_f_w3, hyena_f_freq, hyena_decay, hyena_bias, attn_sink):
    batch, seq, d = x_prompt.shape
    dec_batch, dec_seq, _ = x_sample.shape
    past_len = cache_k.shape[2]

    cvecs = jnp.concatenate([c_ctx[None], c], axis=0)
    cvecs = jnp.pad(cvecs, ((0, MOD_ROWS - cvecs.shape[0]), (0, 0)))
    mod = _modulation(cvecs, w_mod, b_mod).reshape(DEPTH, MOD_ROWS, N_SUB, 3, d)

    wg = ffn_w_gate.astype(BF16)
    wu = ffn_w_up.astype(BF16)
    wd = ffn_w_down.astype(BF16)
    w_in_b = w_in.astype(BF16)
    w_out_b = w_out.astype(BF16)
    ck = cache_k.reshape(dec_batch, DEPTH, past_len, KV_WIDTH)
    cv = cache_v.reshape(dec_batch, DEPTH, past_len, KV_WIDTH)
    rope_cos, rope_sin = (jnp.asarray(t) for t in _rope_tables(dec_seq))

    passes = []
    for n, nreq in ((seq, 1), (dec_seq, dec_batch)):
        fw, iw = (jnp.asarray(t).astype(BF16) for t in _hyena_dft(n))
        pos, chan = (jnp.asarray(t).astype(BF16) for t in _fourier_tables(n))
        passes.append((n, nreq, fw, iw, pos, chan))

    xs = [x_prompt.reshape(batch * seq, d), x_sample.reshape(dec_batch * dec_seq, d)]
    new_k, new_v = [], []
    for l in range(DEPTH):
        g = norm_g[l]
        for which, (n, nreq, fw, iw, pos, chan) in enumerate(passes):
            latent = which == 1
            x = xs[which]
            nb = x.shape[0] // n
            m = mod[l, 1:1 + nreq] if latent else mod[l, 0:1]
            rows_per_req = n if latent else x.shape[0]
            x = _ffn(x, m[:, 0], g[0:2], wg, wu, wd, l, 0, rows_per_req)
            p_fh, q, k, v = _in_proj(x, m[:, 1], g[2:3], w_in_b, l, rows_per_req)
            coef = _filter_spectrum(n, fw, hyena_f_w1[l], hyena_f_b1[l], hyena_f_w2[l], hyena_f_b2[l],
                                    hyena_f_w3[l], hyena_f_freq[l], hyena_decay[l])
            y_fh = _fh_mix(p_fh.reshape(nb, n, FH_WIDTH), n, 2 if latent else 8, pos, chan, fw, iw, coef,
                           hyena_conv_w[l], hyena_bias[l])
            q, k, v = (a.reshape(nb, n, -1) for a in (q, k, v))
            if latent:
                y_att = _lat_attention(q, k, v, n, attn_sink[l], ck, cv, l, rope_cos, rope_sin)
            else:
                y_att = _ctx_attention(q, k, v, n, attn_sink[l])
                new_k.append(k.reshape(nb, n, N_KV_HEADS, HEAD_DIM))
                new_v.append(v.reshape(nb, n, N_KV_HEADS, HEAD_DIM))
            x = _out_proj(x, y_fh.reshape(nb * n, -1), y_att.reshape(nb * n, -1), m[:, 1], g[3:4],
                          w_out_b, l, rows_per_req)
            x = _ffn(x, m[:, 2], g[4:6], wg, wu, wd, l, 1, rows_per_req)
            xs[which] = x

    return (xs[0].reshape(batch, seq, d), xs[1].reshape(dec_batch, dec_seq, d),
            jnp.stack(new_k, axis=1), jnp.stack(new_v, axis=1))
```

```python
import functools
import math

import numpy as np
import jax
import jax.numpy as jnp
from jax import lax
from jax.experimental import pallas as pl
from jax.experimental.pallas import tpu as pltpu

F32 = jnp.float32
BF16 = jnp.bfloat16

D_MODEL = 1024
DEPTH = 2
GRID_W = 64
HEAD_DIM = 64
N_Q_HEADS = 8
N_KV_HEADS = 2
ATTN_WIDTH = N_Q_HEADS * HEAD_DIM
KV_WIDTH = N_KV_HEADS * HEAD_DIM
FOURIER_WIDTH = 256
FOURIER_GROUPS = 4
FOURIER_GROUP_DIM = 64
HYENA_WIDTH = 256
HYENA_ORDER = 2
HYENA_PROJ = 3 * HYENA_WIDTH
HYENA_EMB_DIM = 33
HYENA_BANDS = 16
HYENA_FILTER_WIDTH = 64
FH_WIDTH = FOURIER_WIDTH + HYENA_PROJ
ATT_WIDTH = ATTN_WIDTH + 2 * KV_WIDTH
IN_WIDTH = FH_WIDTH + ATT_WIDTH
BLOCK = 128
WINDOW = 128
ROPE_BASE = 10000.0
D_FF = 2816
N_SUB = 3
RMS_EPS = 1e-6

LANES = 128
MOD_ROWS = 16
VMEM_LIMIT = 56 * 1024 * 1024
NEG_BIG = -1e30


def _cparams(*sem):
    return pltpu.CompilerParams(dimension_semantics=sem, vmem_limit_bytes=VMEM_LIMIT)


def _rms(x, g):
    return x * lax.rsqrt(jnp.mean(x * x, axis=-1, keepdims=True) + RMS_EPS) * g


def _dot(a, b):
    return jnp.dot(a, b, preferred_element_type=F32)


def _dot_nt(a, b):
    return lax.dot_general(a, b, (((1,), (1,)), ((), ())), preferred_element_type=F32)


@functools.lru_cache(maxsize=None)
def _hyena_dft(n):
    f = np.arange(n, dtype=np.int64)[:, None]
    s = np.arange(n, dtype=np.int64)[None, :]
    ang = np.pi * ((f * s) % (2 * n)).astype(np.float64) / n
    c = np.cos(ang)
    sn = np.sin(ang)
    sn[0, :] = 1.0 - 2.0 * (np.arange(n) % 2)
    fw = np.concatenate([c, sn], axis=0).astype(np.float32)
    iw = np.concatenate([c, sn.T], axis=1).astype(np.float32)
    return fw, iw


@functools.lru_cache(maxsize=None)
def _fourier_tables(n):
    f = np.arange(n, dtype=np.int64)[:, None]
    s = np.arange(n, dtype=np.int64)[None, :]
    ang = 2.0 * np.pi * ((f * s) % n).astype(np.float64) / n
    pos = np.concatenate([np.cos(ang), -np.sin(ang)], axis=1) / math.sqrt(n)
    gd = FOURIER_GROUP_DIM
    a = np.arange(gd, dtype=np.int64)
    ang_c = 2.0 * np.pi * ((a[:, None] * a[None, :]) % gd).astype(np.float64) / gd
    bc = np.kron(np.eye(FOURIER_GROUPS), np.cos(ang_c)) / math.sqrt(gd)
    bs = np.kron(np.eye(FOURIER_GROUPS), np.sin(ang_c)) / math.sqrt(gd)
    chan = np.concatenate([bc, bs], axis=1)
    return pos.astype(np.float32), chan.astype(np.float32)


@functools.lru_cache(maxsize=None)
def _rope_tables(n):
    half = HEAD_DIM // 2
    inv = ROPE_BASE ** (-np.arange(0, half, 2, dtype=np.float64) / half)
    t = np.arange(n)
    row = (t // GRID_W).astype(np.float64)
    col = (t % GRID_W).astype(np.float64)
    lane = np.arange(LANES)
    d = lane % HEAD_DIM
    pos = np.where((d // half)[None, :] == 0, row[:, None], col[:, None])
    ang = pos * inv[d % (half // 2)][None, :]
    sign = np.where((d % half) < half // 2, -1.0, 1.0)[None, :]
    return np.cos(ang).astype(np.float32), (np.sin(ang) * sign).astype(np.float32)


def _hyena_feats(n):
    d = jnp.arange(n, dtype=F32)
    t = jnp.linspace(0.0, 1.0, n, dtype=F32)[:, None]
    f = jnp.linspace(1e-4, HYENA_BANDS - 1, HYENA_BANDS, dtype=F32)
    ang = (2.0 * math.pi / n) * d[:, None] * f[None, :]
    feats = jnp.concatenate([t, jnp.cos(ang), -jnp.sin(ang)], axis=-1)
    return jnp.pad(feats, ((0, 0), (0, LANES - HYENA_EMB_DIM))), t


def _mod_kernel(c_ref, w_ref, b_ref, o_ref):
    c = c_ref[...]
    s = c / (1.0 + jnp.exp(-c))
    o_ref[0] = _dot(s.astype(BF16), w_ref[0].astype(BF16)) + b_ref[0]


def _modulation(cvecs, w_mod, b_mod):
    depth, _, width = w_mod.shape
    tn = 1536
    return pl.pallas_call(
        _mod_kernel,
        grid=(depth, width // tn),
        in_specs=[
            pl.BlockSpec((MOD_ROWS, D_MODEL), lambda l, j: (0, 0)),
            pl.BlockSpec((1, D_MODEL, tn), lambda l, j: (l, 0, j)),
            pl.BlockSpec((1, 1, tn), lambda l, j: (l, 0, j)),
        ],
        out_specs=pl.BlockSpec((1, MOD_ROWS, tn), lambda l, j: (l, 0, j)),
        out_shape=jax.ShapeDtypeStruct((depth, MOD_ROWS, width), F32),
        compiler_params=_cparams("arbitrary", "arbitrary"),
        name="modulation",
    )(cvecs, w_mod, b_mod.reshape(depth, 1, width))


def _resident(block_shape, index_map):
    return pl.BlockSpec(block_shape, index_map, pipeline_mode=pl.Buffered(1))


ROW_GROUP = 256
TOKEN_TILE = 1024
EDGE_TILE = 512


def _modulate(x, g_row, mod_ref, sub):
    return (_rms(x, g_row) * (1.0 + mod_ref[0, sub, 1:2]) + mod_ref[0, sub, 0:1]).astype(BF16)


def _swiglu_rows(x, mod_ref, sub, g_ref, wg_ref, wu_ref, wd_ref):
    ga = 2 * sub
    h = _modulate(x, g_ref[ga:ga + 1], mod_ref, sub)
    gate = _dot(h, wg_ref[...])
    up = _dot(h, wu_ref[...])
    act = (gate / (1.0 + jnp.exp(-gate))) * up
    y = _dot(act.astype(BF16), wd_ref[...])
    return x + (0.5 * mod_ref[0, sub, 2:3]) * _rms(y, g_ref[ga + 1:ga + 2])


def _row_groups(rows):
    return [slice(r, r + ROW_GROUP) for r in range(0, rows, ROW_GROUP)]


def _stage_a_kernel(*refs, n_in, ctx_tiles):
    x_refs = refs[:n_in]
    mod_ref, g_ref, wg_ref, wu_ref, wd_ref, win_ref, x1_ref, fh_ref, q_ref, k_ref, v_ref = refs[n_in:]
    is_ctx = pl.program_id(0) < ctx_tiles
    k0 = FH_WIDTH + ATTN_WIDTH
    for sl in _row_groups(x1_ref.shape[0]):
        x = x_refs[0][sl] if n_in == 1 else jnp.where(is_ctx, x_refs[0][sl], x_refs[1][sl])
        x1 = _swiglu_rows(x, mod_ref, 0, g_ref, wg_ref, wu_ref, wd_ref)
        x1_ref[sl] = x1
        p = _dot(_modulate(x1, g_ref[2:3], mod_ref, 1), win_ref[...])
        fh_ref[sl] = p[:, :FH_WIDTH].astype(fh_ref.dtype)
        q_ref[sl] = p[:, FH_WIDTH:k0].astype(q_ref.dtype)
        k_ref[sl] = p[:, k0:k0 + KV_WIDTH]
        v_ref[sl] = p[:, k0 + KV_WIDTH:]


def _stage_b_kernel(x_ref, fhc_ref, fhl_ref, atc_ref, atl_ref, mod_ref, g_ref, wout_ref, wg_ref, wu_ref,
                    wd_ref, *o_refs, ctx_tiles):
    is_ctx = pl.program_id(0) < ctx_tiles
    half = FOURIER_WIDTH + HYENA_WIDTH

    def body(mixed, o_ref):
        for sl in _row_groups(x_ref.shape[0]):
            y_fh, y_at = mixed(sl)
            y = _dot(y_fh, wout_ref[:half]) + _dot(y_at, wout_ref[half:])
            x2 = x_ref[sl] + mod_ref[0, 1, 2:3] * _rms(y, g_ref[3:4])
            o_ref[sl] = _swiglu_rows(x2, mod_ref, 2, g_ref, wg_ref, wu_ref, wd_ref)

    if len(o_refs) == 1:
        body(lambda sl: (jnp.where(is_ctx, fhc_ref[sl], fhl_ref[sl]),
                         jnp.where(is_ctx, atc_ref[sl], atl_ref[sl])), o_refs[0])
    else:
        pl.when(is_ctx)(lambda: body(lambda sl: (fhc_ref[sl], atc_ref[sl]), o_refs[0]))
        pl.when(jnp.logical_not(is_ctx))(lambda: body(lambda sl: (fhl_ref[sl], atl_ref[sl]), o_refs[1]))


def _token_specs(tm, ctx_rows, lat_rows, lat_seq):
    ctx_tiles = ctx_rows // tm
    tiles_per_seq = lat_seq // tm
    tile = lambda width: pl.BlockSpec((tm, width), lambda i: (i, 0))
    ctx_tile = lambda width: pl.BlockSpec((tm, width), lambda i: (jnp.minimum(i, ctx_tiles - 1), 0))
    lat_tile = lambda width: pl.BlockSpec((tm, width), lambda i: (jnp.maximum(i - ctx_tiles, 0), 0))
    request = lambda i: jnp.where(i < ctx_tiles, 0, 1 + (i - ctx_tiles) // tiles_per_seq)
    return ctx_tiles, (ctx_rows + lat_rows) // tm, tile, ctx_tile, lat_tile, request


def _stage_a(xs, mod, norm_g, wg, wu, wd, w_in, layer, ctx_rows, lat_rows, lat_seq, tm):
    ctx_tiles, tiles, tile, ctx_tile, lat_tile, request = _token_specs(tm, ctx_rows, lat_rows, lat_seq)
    t = ctx_rows + lat_rows
    widths = (D_MODEL, FH_WIDTH, ATTN_WIDTH, KV_WIDTH, KV_WIDTH)
    dtypes = (F32, BF16, BF16, F32, F32)
    x_specs = [tile(D_MODEL)] if len(xs) == 1 else [ctx_tile(D_MODEL), lat_tile(D_MODEL)]
    return pl.pallas_call(
        functools.partial(_stage_a_kernel, n_in=len(xs), ctx_tiles=ctx_tiles),
        grid=(tiles,),
        in_specs=x_specs + [
            pl.BlockSpec((None, 1, N_SUB, 3, D_MODEL), lambda i: (layer, request(i), 0, 0, 0)),
            _resident((None, 2 * N_SUB, D_MODEL), lambda i: (layer, 0, 0)),
            _resident((None, None, D_MODEL, D_FF), lambda i: (layer, 0, 0, 0)),
            _resident((None, None, D_MODEL, D_FF), lambda i: (layer, 0, 0, 0)),
            _resident((None, None, D_FF, D_MODEL), lambda i: (layer, 0, 0, 0)),
            _resident((None, D_MODEL, IN_WIDTH), lambda i: (layer, 0, 0)),
        ],
        out_specs=[tile(w) for w in widths],
        out_shape=[jax.ShapeDtypeStruct((t, w), dt) for w, dt in zip(widths, dtypes)],
        compiler_params=_cparams("parallel"),
        name="stage_a",
    )(*xs, mod, norm_g, wg, wu, wd, w_in)


def _stage_b(x1, y_fh, y_att, mod, norm_g, w_out, wg, wu, wd, layer, ctx_rows, lat_rows, lat_seq, tm, split):
    ctx_tiles, tiles, tile, ctx_tile, lat_tile, request = _token_specs(tm, ctx_rows, lat_rows, lat_seq)
    half = FOURIER_WIDTH + HYENA_WIDTH
    if split:
        out_specs = [ctx_tile(D_MODEL), lat_tile(D_MODEL)]
        out_shape = [jax.ShapeDtypeStruct((r, D_MODEL), F32) for r in (ctx_rows, lat_rows)]
    else:
        out_specs = [tile(D_MODEL)]
        out_shape = [jax.ShapeDtypeStruct((ctx_rows + lat_rows, D_MODEL), F32)]
    return pl.pallas_call(
        functools.partial(_stage_b_kernel, ctx_tiles=ctx_tiles),
        grid=(tiles,),
        in_specs=[
            tile(D_MODEL), ctx_tile(half), lat_tile(half), ctx_tile(ATTN_WIDTH), lat_tile(ATTN_WIDTH),
            pl.BlockSpec((None, 1, N_SUB, 3, D_MODEL), lambda i: (layer, request(i), 0, 0, 0)),
            _resident((None, 2 * N_SUB, D_MODEL), lambda i: (layer, 0, 0)),
            _resident((None, D_MODEL, D_MODEL), lambda i: (layer, 0, 0)),
            _resident((None, None, D_MODEL, D_FF), lambda i: (layer, 1, 0, 0)),
            _resident((None, None, D_MODEL, D_FF), lambda i: (layer, 1, 0, 0)),
            _resident((None, None, D_FF, D_MODEL), lambda i: (layer, 1, 0, 0)),
        ],
        out_specs=out_specs,
        out_shape=out_shape,
        compiler_params=_cparams("arbitrary" if split else "parallel"),
        name="stage_b",
    )(x1, y_fh[0], y_fh[1], y_att[0], y_att[1], mod, norm_g, w_out, wg, wu, wd)


def _filter_kernel(feats_ref, t_ref, w1_ref, b1_ref, w2_ref, b2_ref, w3_ref, fr_ref, decay_ref,
                   fw_ref, o_ref, *, n):
    fr = fr_ref[...]
    h = jnp.sin(fr * (_dot(feats_ref[...], w1_ref[...]) + b1_ref[...]))
    h = jnp.sin(fr * (_dot(h, w2_ref[...]) + b2_ref[...]))
    h = _dot(h, w3_ref[...])
    window = jnp.exp(-t_ref[...] * jnp.abs(decay_ref[...]))
    width = HYENA_ORDER * HYENA_WIDTH
    row = lax.broadcasted_iota(jnp.int32, (n, width), 0)
    fwd = h[:, :width] * window
    bwd = jnp.where(row == 0, 0.0, h[:, width:] * window)
    even = fwd + bwd
    odd = fwd - bwd
    k_re = _dot(fw_ref[:n], even.astype(BF16))
    k_sn = _dot(fw_ref[n:], odd.astype(BF16))
    sign = (1 - 2 * (row % 2)).astype(F32)
    k_ny = jnp.sum(even * sign, axis=0, keepdims=True)
    s0 = 1.0 / (4.0 * n * n)
    scale = jnp.where(row == 0, s0, 2.0 * s0)
    a = scale * k_re
    o_ref[0] = a
    o_ref[1] = jnp.where(row == 0, 0.0, scale * k_sn)
    o_ref[2] = jnp.where(row == 0, s0 * k_ny, a)


def _filter_spectrum(n, fw, w1, b1, w2, b2, w3, freq, decay):
    feats, t = _hyena_feats(n)
    pad_w = LANES - HYENA_FILTER_WIDTH
    w1p = jnp.pad(w1, ((0, LANES - HYENA_EMB_DIM), (0, pad_w)))
    w2p = jnp.pad(w2, ((0, pad_w), (0, pad_w)))
    w3p = jnp.pad(w3, ((0, pad_w), (0, 0)))
    padv = lambda v: jnp.pad(v, (0, pad_w)).reshape(1, LANES)
    width = HYENA_ORDER * HYENA_WIDTH
    args = (feats, t, w1p, padv(b1), w2p, padv(b2), w3p, padv(freq), decay.reshape(1, width), fw)
    return pl.pallas_call(
        functools.partial(_filter_kernel, n=n),
        out_shape=jax.ShapeDtypeStruct((3, n, width), F32),
        compiler_params=pltpu.CompilerParams(vmem_limit_bytes=VMEM_LIMIT),
        name="hyena_filter",
    )(*args)


def _fh_kernel(p_ref, pos_ref, chan_ref, fw_ref, iw_ref, coef_ref, convw_ref, bias_ref, o_ref,
               st_scr, *, n, group):
    c = HYENA_WIDTH
    cols = [slice(b * c, (b + 1) * c) for b in range(group)]

    for b in range(group):
        t = _dot(p_ref[b, :, :FOURIER_WIDTH], chan_ref[...])
        st_scr[:n, cols[b]] = t[:, :FOURIER_WIDTH].astype(BF16)
        st_scr[n:, cols[b]] = t[:, FOURIER_WIDTH:].astype(BF16)
    y_f = _dot(pos_ref[...], st_scr[...])
    for b in range(group):
        o_ref[b, :, :FOURIER_WIDTH] = y_f[:, cols[b]].astype(o_ref.dtype)

    row = lax.broadcasted_iota(jnp.int32, (n, HYENA_PROJ), 0)
    zs = []
    for b in range(group):
        z = p_ref[b, :, FOURIER_WIDTH:].astype(F32)
        z_prev = jnp.where(row == 0, 0.0, pltpu.roll(z, 1, axis=0))
        z_next = jnp.where(row == n - 1, 0.0, pltpu.roll(z, n - 1, axis=0))
        zs.append(z_prev * convw_ref[0:1] + z * convw_ref[1:2] + z_next * convw_ref[2:3])

    def long_conv(vs, order):
        oc = slice(order * c, (order + 1) * c)
        uf = _dot(fw_ref[...], jnp.concatenate([v.astype(BF16) for v in vs], axis=1))
        a, bb, a2 = coef_ref[0, :, oc], coef_ref[1, :, oc], coef_ref[2, :, oc]
        for b in range(group):
            u_re, u_sn = uf[:n, cols[b]], uf[n:, cols[b]]
            st_scr[:n, cols[b]] = (u_re * a - u_sn * bb).astype(BF16)
            st_scr[n:, cols[b]] = (u_re * bb + u_sn * a2).astype(BF16)
        y = _dot(iw_ref[...], st_scr[...])
        return [y[:, cols[b]] + vs[b] * bias_ref[order:order + 1] for b in range(group)]

    ys = long_conv([z[:, :c] for z in zs], 0)
    ys = long_conv([z[:, c:2 * c] * y for z, y in zip(zs, ys)], 1)
    for b in range(group):
        o_ref[b, :, FOURIER_WIDTH:] = (zs[b][:, 2 * c:] * ys[b]).astype(o_ref.dtype)


def _fh_mix(p_fh, b, first, n, group, pos, chan, fw, iw, coef, conv_w, hbias):
    width = FOURIER_WIDTH + HYENA_WIDTH
    const = lambda shape: _resident(shape, lambda i: (0,) * len(shape))
    return pl.pallas_call(
        functools.partial(_fh_kernel, n=n, group=group),
        grid=(b // group,),
        in_specs=[
            pl.BlockSpec((group, n, FH_WIDTH), lambda i: (i + first // group, 0, 0)),
            const((n, 2 * n)), const((FOURIER_WIDTH, 2 * FOURIER_WIDTH)),
            const((2 * n, n)), const((n, 2 * n)),
            const((3, n, HYENA_ORDER * HYENA_WIDTH)),
            const((3, HYENA_PROJ)), const((HYENA_ORDER, HYENA_WIDTH)),
        ],
        out_specs=pl.BlockSpec((group, n, width), lambda i: (i, 0, 0)),
        out_shape=jax.ShapeDtypeStruct((b, n, width), BF16),
        scratch_shapes=[pltpu.VMEM((2 * n, group * HYENA_WIDTH), BF16)],
        compiler_params=_cparams("parallel"),
        name="fourier_hyena",
    )(p_fh, pos, chan, fw, iw, coef, conv_w, hbias)


def _head_slots(x, kv_head, lo):
    xr = pltpu.roll(x, HEAD_DIM, axis=1)
    if kv_head == 0:
        return jnp.where(lo, x, 0.0), jnp.where(lo, 0.0, xr)
    return jnp.where(lo, xr, 0.0), jnp.where(lo, 0.0, x)


def _ctx_attn_kernel(sink_ref, q_ref, k_ref, v_ref, o_ref, *, n, group):
    lo = lax.broadcasted_iota(jnp.int32, (n, LANES), 1) < HEAD_DIM
    top = lax.broadcasted_iota(jnp.int32, (2 * n, 1), 0) < n
    lo_g = lax.broadcasted_iota(jnp.int32, (2 * n, LANES), 1) < HEAD_DIM
    for b in range(group):
        k = k_ref[b]
        v = v_ref[b]
        for kv_head in range(N_KV_HEADS):
            h0 = 4 * kv_head
            k_cat = jnp.concatenate(_head_slots(k, kv_head, lo), axis=0).astype(BF16)
            v_cat = jnp.concatenate(_head_slots(v, kv_head, lo), axis=0).astype(BF16)
            q2 = jnp.concatenate([q_ref[b, :, (2 * kv_head + t) * LANES:(2 * kv_head + t + 1) * LANES]
                                  for t in range(2)], axis=0) * HEAD_DIM ** -0.5
            s = _dot_nt(q2, k_cat)
            e, inv_den = [], []
            for slot in range(2):
                sink = jnp.where(top, sink_ref[h0 + slot], sink_ref[h0 + 2 + slot])
                cols = s[:, slot * n:(slot + 1) * n]
                m = jnp.maximum(jnp.max(cols, axis=-1, keepdims=True), sink)
                cols = jnp.exp(cols - m)
                e.append(cols.astype(BF16))
                inv_den.append(1.0 / (jnp.sum(cols, axis=-1, keepdims=True) + jnp.exp(sink - m)))
            o = _dot(jnp.concatenate(e, axis=1), v_cat) * jnp.where(lo_g, inv_den[0], inv_den[1])
            for t in range(2):
                tile = 2 * kv_head + t
                o_ref[b, :, tile * LANES:(tile + 1) * LANES] = o[t * n:(t + 1) * n].astype(o_ref.dtype)


def _ctx_attention(q, k, v, b, n, sink, group=4):
    return pl.pallas_call(
        functools.partial(_ctx_attn_kernel, n=n, group=group),
        grid=(b // group,),
        in_specs=[
            pl.BlockSpec(memory_space=pltpu.SMEM),
            pl.BlockSpec((group, n, ATTN_WIDTH), lambda i: (i, 0, 0)),
            pl.BlockSpec((group, n, KV_WIDTH), lambda i: (i, 0, 0)),
            pl.BlockSpec((group, n, KV_WIDTH), lambda i: (i, 0, 0)),
        ],
        out_specs=pl.BlockSpec((group, n, ATTN_WIDTH), lambda i: (i, 0, 0)),
        out_shape=jax.ShapeDtypeStruct((b, n, ATTN_WIDTH), BF16),
        compiler_params=_cparams("parallel"),
        name="ctx_attention",
    )(sink, q, k, v)


def _rope(x, cos, sin_signed):
    lane = lax.broadcasted_iota(jnp.int32, x.shape, 1)
    first = (lane % (HEAD_DIM // 2)) < HEAD_DIM // 4
    partner = jnp.where(first, pltpu.roll(x, LANES - HEAD_DIM // 4, axis=1),
                        pltpu.roll(x, HEAD_DIM // 4, axis=1))
    return x * cos + partner * sin_signed


def _lat_attn_kernel(sink_ref, q_ref, k_ref, v_ref, ck_ref, cv_ref, cos_ref, sin_ref, o_ref,
                     q_scr, k_scr, v_scr, ck_scr, cv_scr, *, n, c_len):
    nb = n // BLOCK
    cb = c_len // BLOCK
    grp = 2 * BLOCK
    cos, sin = cos_ref[...], sin_ref[...]
    lo = lax.broadcasted_iota(jnp.int32, (n, LANES), 1) < HEAD_DIM
    lo_c = lax.broadcasted_iota(jnp.int32, (c_len, LANES), 1) < HEAD_DIM

    k = _rope(k_ref[0], cos, sin)
    v = v_ref[0]
    for kv_head in range(N_KV_HEADS):
        for t in range(2):
            tile = 2 * kv_head + t
            q = q_ref[0, :, tile * LANES:(tile + 1) * LANES].astype(F32)
            q = (_rope(q, cos, sin) * HEAD_DIM ** -0.5).astype(BF16)
            for i in range(nb):
                q_scr[kv_head, i * grp + t * BLOCK:i * grp + (t + 1) * BLOCK] = q[i * BLOCK:(i + 1) * BLOCK]
        for slot, (ks, vs, cks, cvs) in enumerate(zip(
                _head_slots(k, kv_head, lo), _head_slots(v, kv_head, lo),
                _head_slots(ck_ref[0, 0], kv_head, lo_c), _head_slots(cv_ref[0, 0], kv_head, lo_c))):
            ks, vs, cks, cvs = (a.astype(BF16) for a in (ks, vs, cks, cvs))
            for j in range(nb):
                rows = slice(j * grp + slot * BLOCK, j * grp + (slot + 1) * BLOCK)
                k_scr[kv_head, rows] = ks[j * BLOCK:(j + 1) * BLOCK]
                v_scr[kv_head, rows] = vs[j * BLOCK:(j + 1) * BLOCK]
            for j in range(cb):
                rows = slice(j * grp + slot * BLOCK, j * grp + (slot + 1) * BLOCK)
                ck_scr[kv_head, rows] = cks[j * BLOCK:(j + 1) * BLOCK]
                cv_scr[kv_head, rows] = cvs[j * BLOCK:(j + 1) * BLOCK]

    qi = lax.broadcasted_iota(jnp.int32, (grp, BLOCK), 0) % BLOCK
    kj = lax.broadcasted_iota(jnp.int32, (grp, BLOCK), 1)
    keep_prev = kj >= qi
    keep_next = kj <= qi
    top = lax.broadcasted_iota(jnp.int32, (grp, 1), 0) < BLOCK
    lo_g = lax.broadcasted_iota(jnp.int32, (grp, LANES), 1) < HEAD_DIM

    def lane_tiles(s):
        return [s[:, c * BLOCK:(c + 1) * BLOCK] for c in range(s.shape[1] // BLOCK)]

    for kv_head in range(N_KV_HEADS):
        h0 = 4 * kv_head
        sinks = [jnp.where(top, sink_ref[h0 + slot], sink_ref[h0 + 2 + slot]) for slot in range(2)]
        for i in range(nb):
            j0, j1 = max(i - 1, 0), min(i + 2, nb)
            q2 = q_scr[kv_head, i * grp:(i + 1) * grp]
            tiles = lane_tiles(_dot_nt(q2, k_scr[kv_head, j0 * grp:j1 * grp]))
            for b, j in enumerate(range(j0, j1)):
                for slot in range(2):
                    if j == i - 1:
                        tiles[2 * b + slot] = jnp.where(keep_prev, tiles[2 * b + slot], NEG_BIG)
                    elif j == i + 1:
                        tiles[2 * b + slot] = jnp.where(keep_next, tiles[2 * b + slot], NEG_BIG)
            tiles += lane_tiles(_dot_nt(q2, ck_scr[kv_head]))
            inv_den = []
            for slot in range(2):
                mine = tiles[slot::2]
                m = jnp.maximum(jnp.max(functools.reduce(jnp.maximum, mine), axis=-1, keepdims=True),
                                sinks[slot])
                mine = [jnp.exp(tl - m) for tl in mine]
                tiles[slot::2] = mine
                den = jnp.sum(functools.reduce(jnp.add, mine), axis=-1, keepdims=True) + jnp.exp(sinks[slot] - m)
                inv_den.append(1.0 / den)
            n_loc = 2 * (j1 - j0)
            e_loc = jnp.concatenate(tiles[:n_loc], axis=1).astype(BF16)
            e_ctx = jnp.concatenate(tiles[n_loc:], axis=1).astype(BF16)
            o = _dot(e_loc, v_scr[kv_head, j0 * grp:j1 * grp]) + _dot(e_ctx, cv_scr[kv_head])
            o = (o * jnp.where(lo_g, inv_den[0], inv_den[1])).astype(o_ref.dtype)
            for t in range(2):
                tile = 2 * kv_head + t
                o_ref[0, i * BLOCK:(i + 1) * BLOCK, tile * LANES:(tile + 1) * LANES] = o[t * BLOCK:(t + 1) * BLOCK]


def _lat_attention(q, k, v, b, first, n, sink, cache_k, cache_v, layer, cos, sin):
    c_len = cache_k.shape[2]
    return pl.pallas_call(
        functools.partial(_lat_attn_kernel, n=n, c_len=c_len),
        grid=(b,),
        in_specs=[
            pl.BlockSpec(memory_space=pltpu.SMEM),
            pl.BlockSpec((1, n, ATTN_WIDTH), lambda i: (i + first, 0, 0)),
            pl.BlockSpec((1, n, KV_WIDTH), lambda i: (i + first, 0, 0)),
            pl.BlockSpec((1, n, KV_WIDTH), lambda i: (i + first, 0, 0)),
            pl.BlockSpec((1, 1, c_len, KV_WIDTH), lambda i: (i, layer, 0, 0)),
            pl.BlockSpec((1, 1, c_len, KV_WIDTH), lambda i: (i, layer, 0, 0)),
            _resident((n, LANES), lambda i: (0, 0)),
            _resident((n, LANES), lambda i: (0, 0)),
        ],
        out_specs=pl.BlockSpec((1, n, ATTN_WIDTH), lambda i: (i, 0, 0)),
        out_shape=jax.ShapeDtypeStruct((b, n, ATTN_WIDTH), BF16),
        scratch_shapes=[
            pltpu.VMEM((N_KV_HEADS, 2 * n, LANES), BF16),
            pltpu.VMEM((N_KV_HEADS, 2 * n, LANES), BF16),
            pltpu.VMEM((N_KV_HEADS, 2 * n, LANES), BF16),
            pltpu.VMEM((N_KV_HEADS, 2 * c_len, LANES), BF16),
            pltpu.VMEM((N_KV_HEADS, 2 * c_len, LANES), BF16),
        ],
        compiler_params=_cparams("parallel"),
        name="latent_attention",
    )(sink, q, k, v, cache_k, cache_v, cos, sin)


def kernel(x_prompt, x_sample, cache_k, cache_v, c, c_ctx, w_mod, b_mod, norm_g, ffn_w_gate, ffn_w_up,
           ffn_w_down, w_in, w_out, hyena_conv_w, hyena_f_w1, hyena_f_b1, hyena_f_w2, hyena_f_b2,
           hyena_f_w3, hyena_f_freq, hyena_decay, hyena_bias, attn_sink):
    batch, seq, d = x_prompt.shape
    dec_batch, dec_seq, _ = x_sample.shape
    past_len = cache_k.shape[2]

    cvecs = jnp.concatenate([c_ctx[None], c], axis=0)
    cvecs = jnp.pad(cvecs, ((0, MOD_ROWS - cvecs.shape[0]), (0, 0)))
    mod = _modulation(cvecs, w_mod, b_mod).reshape(DEPTH, MOD_ROWS, N_SUB, 3, d)

    wg = ffn_w_gate.astype(BF16)
    wu = ffn_w_up.astype(BF16)
    wd = ffn_w_down.astype(BF16)
    w_in_b = w_in.astype(BF16)
    w_out_b = w_out.astype(BF16)
    ck = cache_k.reshape(dec_batch, DEPTH, past_len, KV_WIDTH)
    cv = cache_v.reshape(dec_batch, DEPTH, past_len, KV_WIDTH)
    rope_cos, rope_sin = (jnp.asarray(t) for t in _rope_tables(dec_seq))

    tables = {}
    for n in (seq, dec_seq):
        fw, iw = (jnp.asarray(t).astype(BF16) for t in _hyena_dft(n))
        pos, chan = (jnp.asarray(t).astype(BF16) for t in _fourier_tables(n))
        tables[n] = (fw, iw, pos, chan)

    ctx_rows, lat_rows = batch * seq, dec_batch * dec_seq
    rows = ctx_rows + lat_rows
    sizes = (ctx_rows, lat_rows, dec_seq)
    xs = (x_prompt.reshape(ctx_rows, d), x_sample.reshape(lat_rows, d))
    new_k, new_v = [], []
    for l in range(DEPTH):
        first, last = l == 0, l == DEPTH - 1
        x1, p_fh, q, k, v = _stage_a(xs, mod, norm_g, wg, wu, wd, w_in_b, l, *sizes,
                                     tm=EDGE_TILE if first else TOKEN_TILE)
        y_fh, y_att = [], []
        for n, nseq, first_seq, latent in ((seq, batch, 0, False), (dec_seq, dec_batch, ctx_rows // dec_seq, True)):
            fw, iw, pos, chan = tables[n]
            coef = _filter_spectrum(n, fw, hyena_f_w1[l], hyena_f_b1[l], hyena_f_w2[l], hyena_f_b2[l],
                                    hyena_f_w3[l], hyena_f_freq[l], hyena_decay[l])
            y = _fh_mix(p_fh.reshape(rows // n, n, FH_WIDTH), nseq, first_seq, n, 2 if latent else 8,
                        pos, chan, fw, iw, coef, hyena_conv_w[l], hyena_bias[l])
            y_fh.append(y.reshape(nseq * n, -1))
            qs, ks, vs = (a.reshape(rows // n, n, -1) for a in (q, k, v))
            if latent:
                y = _lat_attention(qs, ks, vs, nseq, first_seq, n, attn_sink[l], ck, cv, l, rope_cos, rope_sin)
            else:
                y = _ctx_attention(qs, ks, vs, nseq, n, attn_sink[l])
            y_att.append(y.reshape(nseq * n, -1))
        new_k.append(k[:ctx_rows].reshape(batch, seq, N_KV_HEADS, HEAD_DIM))
        new_v.append(v[:ctx_rows].reshape(batch, seq, N_KV_HEADS, HEAD_DIM))
        xs = _stage_b(x1, y_fh, y_att, mod, norm_g, w_out_b, wg, wu, wd, l, *sizes,
                      tm=EDGE_TILE if last else TOKEN_TILE, split=last)

    return (xs[0].reshape(batch, seq, d), xs[1].reshape(dec_batch, dec_seq, d),
            jnp.stack(new_k, axis=1), jnp.stack(new_v, axis=1))
```

```python
import functools
import math

import numpy as np
import jax
import jax.numpy as jnp
from jax import lax
from jax.experimental import pallas as pl
from jax.experimental.pallas import tpu as pltpu

F32 = jnp.float32
BF16 = jnp.bfloat16

D_MODEL = 1024
DEPTH = 2
GRID_W = 64
HEAD_DIM = 64
N_Q_HEADS = 8
N_KV_HEADS = 2
ATTN_WIDTH = N_Q_HEADS * HEAD_DIM
KV_WIDTH = N_KV_HEADS * HEAD_DIM
FOURIER_WIDTH = 256
FOURIER_GROUPS = 4
FOURIER_GROUP_DIM = 64
HYENA_WIDTH = 256
HYENA_ORDER = 2
HYENA_PROJ = 3 * HYENA_WIDTH
HYENA_EMB_DIM = 33
HYENA_BANDS = 16
HYENA_FILTER_WIDTH = 64
FH_WIDTH = FOURIER_WIDTH + HYENA_PROJ
ATT_WIDTH = ATTN_WIDTH + 2 * KV_WIDTH
IN_WIDTH = FH_WIDTH + ATT_WIDTH
BLOCK = 128
WINDOW = 128
ROPE_BASE = 10000.0
D_FF = 2816
N_SUB = 3
RMS_EPS = 1e-6

LANES = 128
MOD_ROWS = 16
VMEM_LIMIT = 56 * 1024 * 1024
NEG_BIG = -1e30


def _cparams(*sem):
    return pltpu.CompilerParams(dimension_semantics=sem, vmem_limit_bytes=VMEM_LIMIT)


def _rms(x, g):
    return x * lax.rsqrt(jnp.mean(x * x, axis=-1, keepdims=True) + RMS_EPS) * g


def _dot(a, b):
    return jnp.dot(a, b, preferred_element_type=F32)


def _dot_nt(a, b):
    return lax.dot_general(a, b, (((1,), (1,)), ((), ())), preferred_element_type=F32)


@functools.lru_cache(maxsize=None)
def _hyena_dft(n):
    f = np.arange(n, dtype=np.int64)[:, None]
    s = np.arange(n, dtype=np.int64)[None, :]
    ang = np.pi * ((f * s) % (2 * n)).astype(np.float64) / n
    c = np.cos(ang)
    sn = np.sin(ang)
    sn[0, :] = 1.0 - 2.0 * (np.arange(n) % 2)
    fw = np.concatenate([c, sn], axis=0).astype(np.float32)
    iw = np.concatenate([c, sn.T], axis=1).astype(np.float32)
    return fw, iw


@functools.lru_cache(maxsize=None)
def _fourier_tables(n):
    f = np.arange(n, dtype=np.int64)[:, None]
    s = np.arange(n, dtype=np.int64)[None, :]
    ang = 2.0 * np.pi * ((f * s) % n).astype(np.float64) / n
    pos = np.concatenate([np.cos(ang), -np.sin(ang)], axis=1) / math.sqrt(n)
    gd = FOURIER_GROUP_DIM
    a = np.arange(gd, dtype=np.int64)
    ang_c = 2.0 * np.pi * ((a[:, None] * a[None, :]) % gd).astype(np.float64) / gd
    bc = np.kron(np.eye(FOURIER_GROUPS), np.cos(ang_c)) / math.sqrt(gd)
    bs = np.kron(np.eye(FOURIER_GROUPS), np.sin(ang_c)) / math.sqrt(gd)
    chan = np.concatenate([bc, bs], axis=1)
    return pos.astype(np.float32), chan.astype(np.float32)


@functools.lru_cache(maxsize=None)
def _rope_tables(n):
    half = HEAD_DIM // 2
    inv = ROPE_BASE ** (-np.arange(0, half, 2, dtype=np.float64) / half)
    t = np.arange(n)
    row = (t // GRID_W).astype(np.float64)
    col = (t % GRID_W).astype(np.float64)
    lane = np.arange(LANES)
    d = lane % HEAD_DIM
    pos = np.where((d // half)[None, :] == 0, row[:, None], col[:, None])
    ang = pos * inv[d % (half // 2)][None, :]
    sign = np.where((d % half) < half // 2, -1.0, 1.0)[None, :]
    return np.cos(ang).astype(np.float32), (np.sin(ang) * sign).astype(np.float32)


def _hyena_feats(n):
    d = jnp.arange(n, dtype=F32)
    t = jnp.linspace(0.0, 1.0, n, dtype=F32)[:, None]
    f = jnp.linspace(1e-4, HYENA_BANDS - 1, HYENA_BANDS, dtype=F32)
    ang = (2.0 * math.pi / n) * d[:, None] * f[None, :]
    feats = jnp.concatenate([t, jnp.cos(ang), -jnp.sin(ang)], axis=-1)
    return jnp.pad(feats, ((0, 0), (0, LANES - HYENA_EMB_DIM))), t


def _mod_kernel(c_ref, w_ref, b_ref, o_ref):
    c = c_ref[...]
    s = c / (1.0 + jnp.exp(-c))
    o_ref[0] = _dot(s.astype(BF16), w_ref[0].astype(BF16)) + b_ref[0]


def _modulation(cvecs, w_mod, b_mod):
    depth, _, width = w_mod.shape
    tn = 1536
    return pl.pallas_call(
        _mod_kernel,
        grid=(depth, width // tn),
        in_specs=[
            pl.BlockSpec((MOD_ROWS, D_MODEL), lambda l, j: (0, 0)),
            pl.BlockSpec((1, D_MODEL, tn), lambda l, j: (l, 0, j)),
            pl.BlockSpec((1, 1, tn), lambda l, j: (l, 0, j)),
        ],
        out_specs=pl.BlockSpec((1, MOD_ROWS, tn), lambda l, j: (l, 0, j)),
        out_shape=jax.ShapeDtypeStruct((depth, MOD_ROWS, width), F32),
        compiler_params=_cparams("arbitrary", "arbitrary"),
        name="modulation",
    )(cvecs, w_mod, b_mod.reshape(depth, 1, width))


def _resident(block_shape, index_map):
    return pl.BlockSpec(block_shape, index_map, pipeline_mode=pl.Buffered(1))


ROW_GROUP = 256
TOKEN_TILE = 1024
EDGE_TILE = 512


def _modulate(x, g_row, mod_ref, sub):
    return (_rms(x, g_row) * (1.0 + mod_ref[0, sub, 1:2]) + mod_ref[0, sub, 0:1]).astype(BF16)


def _swiglu_rows(x, h, mod_ref, sub, g_ref, wg_ref, wu_ref, wd_ref):
    gate = _dot(h, wg_ref[...])
    up = _dot(h, wu_ref[...])
    act = (gate / (1.0 + jnp.exp(-gate))) * up
    y = _dot(act.astype(BF16), wd_ref[...])
    return x + (0.5 * mod_ref[0, sub, 2:3]) * _rms(y, g_ref[2 * sub + 1:2 * sub + 2])


def _software_pipeline(rows, phases):
    groups = [slice(r, r + ROW_GROUP) for r in range(0, rows, ROW_GROUP)]
    state = {}
    for turn in range(len(groups) + len(phases) - 1):
        for p, phase in enumerate(phases):
            g = turn - p
            if 0 <= g < len(groups):
                state[g] = phase(groups[g], state.get(g))


def _stage_a_kernel(*refs, n_in, ctx_tiles):
    x_refs = refs[:n_in]
    mod_ref, g_ref, wg_ref, wu_ref, wd_ref, win_ref, x1_ref, fh_ref, q_ref, k_ref, v_ref = refs[n_in:]
    is_ctx = pl.program_id(0) < ctx_tiles
    k0 = FH_WIDTH + ATTN_WIDTH

    def modulated_input(sl, _):
        x = x_refs[0][sl] if n_in == 1 else jnp.where(is_ctx, x_refs[0][sl], x_refs[1][sl])
        return x, _modulate(x, g_ref[0:1], mod_ref, 0)

    def half_step(sl, xh):
        x1 = _swiglu_rows(*xh, mod_ref, 0, g_ref, wg_ref, wu_ref, wd_ref)
        x1_ref[sl] = x1
        return _modulate(x1, g_ref[2:3], mod_ref, 1)

    def in_projection(sl, h):
        p = _dot(h, win_ref[...])
        fh_ref[sl] = p[:, :FH_WIDTH].astype(fh_ref.dtype)
        q_ref[sl] = p[:, FH_WIDTH:k0].astype(q_ref.dtype)
        k_ref[sl] = p[:, k0:k0 + KV_WIDTH]
        v_ref[sl] = p[:, k0 + KV_WIDTH:]

    _software_pipeline(x1_ref.shape[0], [modulated_input, half_step, in_projection])


def _stage_b_kernel(x_ref, fhc_ref, fhl_ref, atc_ref, atl_ref, mod_ref, g_ref, wout_ref, wg_ref, wu_ref,
                    wd_ref, *o_refs, ctx_tiles):
    is_ctx = pl.program_id(0) < ctx_tiles
    half = FOURIER_WIDTH + HYENA_WIDTH

    def body(mixed, o_ref):
        def out_projection(sl, _):
            y_fh, y_at = mixed(sl)
            y = _dot(y_fh, wout_ref[:half]) + _dot(y_at, wout_ref[half:])
            x2 = x_ref[sl] + mod_ref[0, 1, 2:3] * _rms(y, g_ref[3:4])
            return x2, _modulate(x2, g_ref[4:5], mod_ref, 2)

        def half_step(sl, xh):
            o_ref[sl] = _swiglu_rows(*xh, mod_ref, 2, g_ref, wg_ref, wu_ref, wd_ref)

        _software_pipeline(x_ref.shape[0], [out_projection, half_step])

    if len(o_refs) == 1:
        body(lambda sl: (jnp.where(is_ctx, fhc_ref[sl], fhl_ref[sl]),
                         jnp.where(is_ctx, atc_ref[sl], atl_ref[sl])), o_refs[0])
    else:
        pl.when(is_ctx)(lambda: body(lambda sl: (fhc_ref[sl], atc_ref[sl]), o_refs[0]))
        pl.when(jnp.logical_not(is_ctx))(lambda: body(lambda sl: (fhl_ref[sl], atl_ref[sl]), o_refs[1]))


def _token_specs(tm, ctx_rows, lat_rows, lat_seq):
    ctx_tiles = ctx_rows // tm
    tiles_per_seq = lat_seq // tm
    tile = lambda width: pl.BlockSpec((tm, width), lambda i: (i, 0))
    ctx_tile = lambda width: pl.BlockSpec((tm, width), lambda i: (jnp.minimum(i, ctx_tiles - 1), 0))
    lat_tile = lambda width: pl.BlockSpec((tm, width), lambda i: (jnp.maximum(i - ctx_tiles, 0), 0))
    request = lambda i: jnp.where(i < ctx_tiles, 0, 1 + (i - ctx_tiles) // tiles_per_seq)
    return ctx_tiles, (ctx_rows + lat_rows) // tm, tile, ctx_tile, lat_tile, request


def _stage_a(xs, mod, norm_g, wg, wu, wd, w_in, layer, ctx_rows, lat_rows, lat_seq, tm):
    ctx_tiles, tiles, tile, ctx_tile, lat_tile, request = _token_specs(tm, ctx_rows, lat_rows, lat_seq)
    t = ctx_rows + lat_rows
    widths = (D_MODEL, FH_WIDTH, ATTN_WIDTH, KV_WIDTH, KV_WIDTH)
    dtypes = (F32, BF16, BF16, F32, F32)
    x_specs = [tile(D_MODEL)] if len(xs) == 1 else [ctx_tile(D_MODEL), lat_tile(D_MODEL)]
    return pl.pallas_call(
        functools.partial(_stage_a_kernel, n_in=len(xs), ctx_tiles=ctx_tiles),
        grid=(tiles,),
        in_specs=x_specs + [
            pl.BlockSpec((None, 1, N_SUB, 3, D_MODEL), lambda i: (layer, request(i), 0, 0, 0)),
            _resident((None, 2 * N_SUB, D_MODEL), lambda i: (layer, 0, 0)),
            _resident((None, None, D_MODEL, D_FF), lambda i: (layer, 0, 0, 0)),
            _resident((None, None, D_MODEL, D_FF), lambda i: (layer, 0, 0, 0)),
            _resident((None, None, D_FF, D_MODEL), lambda i: (layer, 0, 0, 0)),
            _resident((None, D_MODEL, IN_WIDTH), lambda i: (layer, 0, 0)),
        ],
        out_specs=[tile(w) for w in widths],
        out_shape=[jax.ShapeDtypeStruct((t, w), dt) for w, dt in zip(widths, dtypes)],
        compiler_params=_cparams("parallel"),
        name="stage_a",
    )(*xs, mod, norm_g, wg, wu, wd, w_in)


def _stage_b(x1, y_fh, y_att, mod, norm_g, w_out, wg, wu, wd, layer, ctx_rows, lat_rows, lat_seq, tm, split):
    ctx_tiles, tiles, tile, ctx_tile, lat_tile, request = _token_specs(tm, ctx_rows, lat_rows, lat_seq)
    half = FOURIER_WIDTH + HYENA_WIDTH
    if split:
        out_specs = [ctx_tile(D_MODEL), lat_tile(D_MODEL)]
        out_shape = [jax.ShapeDtypeStruct((r, D_MODEL), F32) for r in (ctx_rows, lat_rows)]
    else:
        out_specs = [tile(D_MODEL)]
        out_shape = [jax.ShapeDtypeStruct((ctx_rows + lat_rows, D_MODEL), F32)]
    return pl.pallas_call(
        functools.partial(_stage_b_kernel, ctx_tiles=ctx_tiles),
        grid=(tiles,),
        in_specs=[
            tile(D_MODEL), ctx_tile(half), lat_tile(half), ctx_tile(ATTN_WIDTH), lat_tile(ATTN_WIDTH),
            pl.BlockSpec((None, 1, N_SUB, 3, D_MODEL), lambda i: (layer, request(i), 0, 0, 0)),
            _resident((None, 2 * N_SUB, D_MODEL), lambda i: (layer, 0, 0)),
            _resident((None, D_MODEL, D_MODEL), lambda i: (layer, 0, 0)),
            _resident((None, None, D_MODEL, D_FF), lambda i: (layer, 1, 0, 0)),
            _resident((None, None, D_MODEL, D_FF), lambda i: (layer, 1, 0, 0)),
            _resident((None, None, D_FF, D_MODEL), lambda i: (layer, 1, 0, 0)),
        ],
        out_specs=out_specs,
        out_shape=out_shape,
        compiler_params=_cparams("arbitrary" if split else "parallel"),
        name="stage_b",
    )(x1, y_fh[0], y_fh[1], y_att[0], y_att[1], mod, norm_g, w_out, wg, wu, wd)


def _filter_kernel(feats_ref, t_ref, w1_ref, b1_ref, w2_ref, b2_ref, w3_ref, fr_ref, decay_ref,
                   fw_ref, o_ref, *, n):
    fr = fr_ref[...]
    h = jnp.sin(fr * (_dot(feats_ref[...], w1_ref[...]) + b1_ref[...]))
    h = jnp.sin(fr * (_dot(h, w2_ref[...]) + b2_ref[...]))
    h = _dot(h, w3_ref[...])
    window = jnp.exp(-t_ref[...] * jnp.abs(decay_ref[...]))
    width = HYENA_ORDER * HYENA_WIDTH
    row = lax.broadcasted_iota(jnp.int32, (n, width), 0)
    fwd = h[:, :width] * window
    bwd = jnp.where(row == 0, 0.0, h[:, width:] * window)
    even = fwd + bwd
    odd = fwd - bwd
    k_re = _dot(fw_ref[:n], even.astype(BF16))
    k_sn = _dot(fw_ref[n:], odd.astype(BF16))
    sign = (1 - 2 * (row % 2)).astype(F32)
    k_ny = jnp.sum(even * sign, axis=0, keepdims=True)
    s0 = 1.0 / (4.0 * n * n)
    scale = jnp.where(row == 0, s0, 2.0 * s0)
    a = scale * k_re
    o_ref[0] = a
    o_ref[1] = jnp.where(row == 0, 0.0, scale * k_sn)
    o_ref[2] = jnp.where(row == 0, s0 * k_ny, a)


def _filter_spectrum(n, fw, w1, b1, w2, b2, w3, freq, decay):
    feats, t = _hyena_feats(n)
    pad_w = LANES - HYENA_FILTER_WIDTH
    w1p = jnp.pad(w1, ((0, LANES - HYENA_EMB_DIM), (0, pad_w)))
    w2p = jnp.pad(w2, ((0, pad_w), (0, pad_w)))
    w3p = jnp.pad(w3, ((0, pad_w), (0, 0)))
    padv = lambda v: jnp.pad(v, (0, pad_w)).reshape(1, LANES)
    width = HYENA_ORDER * HYENA_WIDTH
    args = (feats, t, w1p, padv(b1), w2p, padv(b2), w3p, padv(freq), decay.reshape(1, width), fw)
    return pl.pallas_call(
        functools.partial(_filter_kernel, n=n),
        out_shape=jax.ShapeDtypeStruct((3, n, width), F32),
        compiler_params=pltpu.CompilerParams(vmem_limit_bytes=VMEM_LIMIT),
        name="hyena_filter",
    )(*args)


def _fh_kernel(p_ref, pos_ref, chan_ref, fw_ref, iw_ref, coef_ref, convw_ref, bias_ref, o_ref,
               st_scr, *, n, group):
    c = HYENA_WIDTH
    cols = [slice(b * c, (b + 1) * c) for b in range(group)]

    for b in range(group):
        t = _dot(p_ref[b, :, :FOURIER_WIDTH], chan_ref[...])
        st_scr[:n, cols[b]] = t[:, :FOURIER_WIDTH].astype(BF16)
        st_scr[n:, cols[b]] = t[:, FOURIER_WIDTH:].astype(BF16)
    y_f = _dot(pos_ref[...], st_scr[...])
    for b in range(group):
        o_ref[b, :, :FOURIER_WIDTH] = y_f[:, cols[b]].astype(o_ref.dtype)

    row = lax.broadcasted_iota(jnp.int32, (n, HYENA_PROJ), 0)
    zs = []
    for b in range(group):
        z = p_ref[b, :, FOURIER_WIDTH:].astype(F32)
        z_prev = jnp.where(row == 0, 0.0, pltpu.roll(z, 1, axis=0))
        z_next = jnp.where(row == n - 1, 0.0, pltpu.roll(z, n - 1, axis=0))
        zs.append(z_prev * convw_ref[0:1] + z * convw_ref[1:2] + z_next * convw_ref[2:3])

    def long_conv(vs, order):
        oc = slice(order * c, (order + 1) * c)
        uf = _dot(fw_ref[...], jnp.concatenate([v.astype(BF16) for v in vs], axis=1))
        a, bb, a2 = coef_ref[0, :, oc], coef_ref[1, :, oc], coef_ref[2, :, oc]
        for b in range(group):
            u_re, u_sn = uf[:n, cols[b]], uf[n:, cols[b]]
            st_scr[:n, cols[b]] = (u_re * a - u_sn * bb).astype(BF16)
            st_scr[n:, cols[b]] = (u_re * bb + u_sn * a2).astype(BF16)
        y = _dot(iw_ref[...], st_scr[...])
        return [y[:, cols[b]] + vs[b] * bias_ref[order:order + 1] for b in range(group)]

    ys = long_conv([z[:, :c] for z in zs], 0)
    ys = long_conv([z[:, c:2 * c] * y for z, y in zip(zs, ys)], 1)
    for b in range(group):
        o_ref[b, :, FOURIER_WIDTH:] = (zs[b][:, 2 * c:] * ys[b]).astype(o_ref.dtype)


def _fh_mix(p_fh, b, first, n, group, pos, chan, fw, iw, coef, conv_w, hbias):
    width = FOURIER_WIDTH + HYENA_WIDTH
    const = lambda shape: _resident(shape, lambda i: (0,) * len(shape))
    return pl.pallas_call(
        functools.partial(_fh_kernel, n=n, group=group),
        grid=(b // group,),
        in_specs=[
            pl.BlockSpec((group, n, FH_WIDTH), lambda i: (i + first // group, 0, 0)),
            const((n, 2 * n)), const((FOURIER_WIDTH, 2 * FOURIER_WIDTH)),
            const((2 * n, n)), const((n, 2 * n)),
            const((3, n, HYENA_ORDER * HYENA_WIDTH)),
            const((3, HYENA_PROJ)), const((HYENA_ORDER, HYENA_WIDTH)),
        ],
        out_specs=pl.BlockSpec((group, n, width), lambda i: (i, 0, 0)),
        out_shape=jax.ShapeDtypeStruct((b, n, width), BF16),
        scratch_shapes=[pltpu.VMEM((2 * n, group * HYENA_WIDTH), BF16)],
        compiler_params=_cparams("parallel"),
        name="fourier_hyena",
    )(p_fh, pos, chan, fw, iw, coef, conv_w, hbias)


def _head_slots(x, kv_head, lo):
    xr = pltpu.roll(x, HEAD_DIM, axis=1)
    if kv_head == 0:
        return jnp.where(lo, x, 0.0), jnp.where(lo, 0.0, xr)
    return jnp.where(lo, xr, 0.0), jnp.where(lo, 0.0, x)


def _ctx_attn_kernel(sink_ref, q_ref, k_ref, v_ref, o_ref, *, n, group):
    lo = lax.broadcasted_iota(jnp.int32, (n, LANES), 1) < HEAD_DIM
    top = lax.broadcasted_iota(jnp.int32, (2 * n, 1), 0) < n
    lo_g = lax.broadcasted_iota(jnp.int32, (2 * n, LANES), 1) < HEAD_DIM
    for b in range(group):
        k = k_ref[b]
        v = v_ref[b]
        for kv_head in range(N_KV_HEADS):
            h0 = 4 * kv_head
            k_cat = jnp.concatenate(_head_slots(k, kv_head, lo), axis=0).astype(BF16)
            v_cat = jnp.concatenate(_head_slots(v, kv_head, lo), axis=0).astype(BF16)
            q2 = jnp.concatenate([q_ref[b, :, (2 * kv_head + t) * LANES:(2 * kv_head + t + 1) * LANES]
                                  for t in range(2)], axis=0) * HEAD_DIM ** -0.5
            s = _dot_nt(q2, k_cat)
            e, inv_den = [], []
            for slot in range(2):
                sink = jnp.where(top, sink_ref[h0 + slot], sink_ref[h0 + 2 + slot])
                cols = s[:, slot * n:(slot + 1) * n]
                m = jnp.maximum(jnp.max(cols, axis=-1, keepdims=True), sink)
                cols = jnp.exp(cols - m)
                e.append(cols.astype(BF16))
                inv_den.append(1.0 / (jnp.sum(cols, axis=-1, keepdims=True) + jnp.exp(sink - m)))
            o = _dot(jnp.concatenate(e, axis=1), v_cat) * jnp.where(lo_g, inv_den[0], inv_den[1])
            for t in range(2):
                tile = 2 * kv_head + t
                o_ref[b, :, tile * LANES:(tile + 1) * LANES] = o[t * n:(t + 1) * n].astype(o_ref.dtype)


def _ctx_attention(q, k, v, b, n, sink, group=4):
    return pl.pallas_call(
        functools.partial(_ctx_attn_kernel, n=n, group=group),
        grid=(b // group,),
        in_specs=[
            pl.BlockSpec(memory_space=pltpu.SMEM),
            pl.BlockSpec((group, n, ATTN_WIDTH), lambda i: (i, 0, 0)),
            pl.BlockSpec((group, n, KV_WIDTH), lambda i: (i, 0, 0)),
            pl.BlockSpec((group, n, KV_WIDTH), lambda i: (i, 0, 0)),
        ],
        out_specs=pl.BlockSpec((group, n, ATTN_WIDTH), lambda i: (i, 0, 0)),
        out_shape=jax.ShapeDtypeStruct((b, n, ATTN_WIDTH), BF16),
        compiler_params=_cparams("parallel"),
        name="ctx_attention",
    )(sink, q, k, v)


def _rope(x, cos, sin_signed):
    lane = lax.broadcasted_iota(jnp.int32, x.shape, 1)
    first = (lane % (HEAD_DIM // 2)) < HEAD_DIM // 4
    partner = jnp.where(first, pltpu.roll(x, LANES - HEAD_DIM // 4, axis=1),
                        pltpu.roll(x, HEAD_DIM // 4, axis=1))
    return x * cos + partner * sin_signed


def _lat_attn_kernel(sink_ref, q_ref, k_ref, v_ref, ck_ref, cv_ref, cos_ref, sin_ref, o_ref,
                     q_scr, k_scr, v_scr, ck_scr, cv_scr, *, n, c_len):
    nb = n // BLOCK
    cb = c_len // BLOCK
    grp = 2 * BLOCK
    cos, sin = cos_ref[...], sin_ref[...]
    lo = lax.broadcasted_iota(jnp.int32, (n, LANES), 1) < HEAD_DIM
    lo_c = lax.broadcasted_iota(jnp.int32, (c_len, LANES), 1) < HEAD_DIM

    k = _rope(k_ref[0], cos, sin)
    v = v_ref[0]
    for kv_head in range(N_KV_HEADS):
        for t in range(2):
            tile = 2 * kv_head + t
            q = q_ref[0, :, tile * LANES:(tile + 1) * LANES].astype(F32)
            q = (_rope(q, cos, sin) * HEAD_DIM ** -0.5).astype(BF16)
            for i in range(nb):
                q_scr[kv_head, i * grp + t * BLOCK:i * grp + (t + 1) * BLOCK] = q[i * BLOCK:(i + 1) * BLOCK]
        for slot, (ks, vs, cks, cvs) in enumerate(zip(
                _head_slots(k, kv_head, lo), _head_slots(v, kv_head, lo),
                _head_slots(ck_ref[0, 0], kv_head, lo_c), _head_slots(cv_ref[0, 0], kv_head, lo_c))):
            ks, vs, cks, cvs = (a.astype(BF16) for a in (ks, vs, cks, cvs))
            for j in range(nb):
                rows = slice(j * grp + slot * BLOCK, j * grp + (slot + 1) * BLOCK)
                k_scr[kv_head, rows] = ks[j * BLOCK:(j + 1) * BLOCK]
                v_scr[kv_head, rows] = vs[j * BLOCK:(j + 1) * BLOCK]
            for j in range(cb):
                rows = slice(j * grp + slot * BLOCK, j * grp + (slot + 1) * BLOCK)
                ck_scr[kv_head, rows] = cks[j * BLOCK:(j + 1) * BLOCK]
                cv_scr[kv_head, rows] = cvs[j * BLOCK:(j + 1) * BLOCK]

    qi = lax.broadcasted_iota(jnp.int32, (grp, BLOCK), 0) % BLOCK
    kj = lax.broadcasted_iota(jnp.int32, (grp, BLOCK), 1)
    keep_prev = kj >= qi
    keep_next = kj <= qi
    top = lax.broadcasted_iota(jnp.int32, (grp, 1), 0) < BLOCK
    lo_g = lax.broadcasted_iota(jnp.int32, (grp, LANES), 1) < HEAD_DIM

    def lane_tiles(s):
        return [s[:, c * BLOCK:(c + 1) * BLOCK] for c in range(s.shape[1] // BLOCK)]

    for kv_head in range(N_KV_HEADS):
        h0 = 4 * kv_head
        sinks = [jnp.where(top, sink_ref[h0 + slot], sink_ref[h0 + 2 + slot]) for slot in range(2)]
        for i in range(nb):
            j0, j1 = max(i - 1, 0), min(i + 2, nb)
            q2 = q_scr[kv_head, i * grp:(i + 1) * grp]
            tiles = lane_tiles(_dot_nt(q2, k_scr[kv_head, j0 * grp:j1 * grp]))
            for b, j in enumerate(range(j0, j1)):
                for slot in range(2):
                    if j == i - 1:
                        tiles[2 * b + slot] = jnp.where(keep_prev, tiles[2 * b + slot], NEG_BIG)
                    elif j == i + 1:
                        tiles[2 * b + slot] = jnp.where(keep_next, tiles[2 * b + slot], NEG_BIG)
            tiles += lane_tiles(_dot_nt(q2, ck_scr[kv_head]))
            inv_den = []
            for slot in range(2):
                mine = tiles[slot::2]
                m = jnp.maximum(jnp.max(functools.reduce(jnp.maximum, mine), axis=-1, keepdims=True),
                                sinks[slot])
                mine = [jnp.exp(tl - m) for tl in mine]
                tiles[slot::2] = mine
                den = jnp.sum(functools.reduce(jnp.add, mine), axis=-1, keepdims=True) + jnp.exp(sinks[slot] - m)
                inv_den.append(1.0 / den)
            n_loc = 2 * (j1 - j0)
            e_loc = jnp.concatenate(tiles[:n_loc], axis=1).astype(BF16)
            e_ctx = jnp.concatenate(tiles[n_loc:], axis=1).astype(BF16)
            o = _dot(e_loc, v_scr[kv_head, j0 * grp:j1 * grp]) + _dot(e_ctx, cv_scr[kv_head])
            o = (o * jnp.where(lo_g, inv_den[0], inv_den[1])).astype(o_ref.dtype)
            for t in range(2):
                tile = 2 * kv_head + t
                o_ref[0, i * BLOCK:(i + 1) * BLOCK, tile * LANES:(tile + 1) * LANES] = o[t * BLOCK:(t + 1) * BLOCK]


def _lat_attention(q, k, v, b, first, n, sink, cache_k, cache_v, layer, cos, sin):
    c_len = cache_k.shape[2]
    return pl.pallas_call(
        functools.partial(_lat_attn_kernel, n=n, c_len=c_len),
        grid=(b,),
        in_specs=[
            pl.BlockSpec(memory_space=pltpu.SMEM),
            pl.BlockSpec((1, n, ATTN_WIDTH), lambda i: (i + first, 0, 0)),
            pl.BlockSpec((1, n, KV_WIDTH), lambda i: (i + first, 0, 0)),
            pl.BlockSpec((1, n, KV_WIDTH), lambda i: (i + first, 0, 0)),
            pl.BlockSpec((1, 1, c_len, KV_WIDTH), lambda i: (i, layer, 0, 0)),
            pl.BlockSpec((1, 1, c_len, KV_WIDTH), lambda i: (i, layer, 0, 0)),
            _resident((n, LANES), lambda i: (0, 0)),
            _resident((n, LANES), lambda i: (0, 0)),
        ],
        out_specs=pl.BlockSpec((1, n, ATTN_WIDTH), lambda i: (i, 0, 0)),
        out_shape=jax.ShapeDtypeStruct((b, n, ATTN_WIDTH), BF16),
        scratch_shapes=[
            pltpu.VMEM((N_KV_HEADS, 2 * n, LANES), BF16),
            pltpu.VMEM((N_KV_HEADS, 2 * n, LANES), BF16),
            pltpu.VMEM((N_KV_HEADS, 2 * n, LANES), BF16),
            pltpu.VMEM((N_KV_HEADS, 2 * c_len, LANES), BF16),
            pltpu.VMEM((N_KV_HEADS, 2 * c_len, LANES), BF16),
        ],
        compiler_params=_cparams("parallel"),
        name="latent_attention",
    )(sink, q, k, v, cache_k, cache_v, cos, sin)


def kernel(x_prompt, x_sample, cache_k, cache_v, c, c_ctx, w_mod, b_mod, norm_g, ffn_w_gate, ffn_w_up,
           ffn_w_down, w_in, w_out, hyena_conv_w, hyena_f_w1, hyena_f_b1, hyena_f_w2, hyena_f_b2,
           hyena_f_w3, hyena_f_freq, hyena_decay, hyena_bias, attn_sink):
    batch, seq, d = x_prompt.shape
    dec_batch, dec_seq, _ = x_sample.shape
    past_len = cache_k.shape[2]

    cvecs = jnp.concatenate([c_ctx[None], c], axis=0)
    cvecs = jnp.pad(cvecs, ((0, MOD_ROWS - cvecs.shape[0]), (0, 0)))
    mod = _modulation(cvecs, w_mod, b_mod).reshape(DEPTH, MOD_ROWS, N_SUB, 3, d)

    wg = ffn_w_gate.astype(BF16)
    wu = ffn_w_up.astype(BF16)
    wd = ffn_w_down.astype(BF16)
    w_in_b = w_in.astype(BF16)
    w_out_b = w_out.astype(BF16)
    ck = cache_k.reshape(dec_batch, DEPTH, past_len, KV_WIDTH)
    cv = cache_v.reshape(dec_batch, DEPTH, past_len, KV_WIDTH)
    rope_cos, rope_sin = (jnp.asarray(t) for t in _rope_tables(dec_seq))

    tables = {}
    for n in (seq, dec_seq):
        fw, iw = (jnp.asarray(t).astype(BF16) for t in _hyena_dft(n))
        pos, chan = (jnp.asarray(t).astype(BF16) for t in _fourier_tables(n))
        tables[n] = (fw, iw, pos, chan)

    ctx_rows, lat_rows = batch * seq, dec_batch * dec_seq
    rows = ctx_rows + lat_rows
    sizes = (ctx_rows, lat_rows, dec_seq)
    xs = (x_prompt.reshape(ctx_rows, d), x_sample.reshape(lat_rows, d))
    new_k, new_v = [], []
    for l in range(DEPTH):
        first, last = l == 0, l == DEPTH - 1
        x1, p_fh, q, k, v = _stage_a(xs, mod, norm_g, wg, wu, wd, w_in_b, l, *sizes,
                                     tm=EDGE_TILE if first else TOKEN_TILE)
        y_fh, y_att = [], []
        for n, nseq, first_seq, latent in ((seq, batch, 0, False), (dec_seq, dec_batch, ctx_rows // dec_seq, True)):
            fw, iw, pos, chan = tables[n]
            coef = _filter_spectrum(n, fw, hyena_f_w1[l], hyena_f_b1[l], hyena_f_w2[l], hyena_f_b2[l],
                                    hyena_f_w3[l], hyena_f_freq[l], hyena_decay[l])
            y = _fh_mix(p_fh.reshape(rows // n, n, FH_WIDTH), nseq, first_seq, n, 2 if latent else 8,
                        pos, chan, fw, iw, coef, hyena_conv_w[l], hyena_bias[l])
            y_fh.append(y.reshape(nseq * n, -1))
            qs, ks, vs = (a.reshape(rows // n, n, -1) for a in (q, k, v))
            if latent:
                y = _lat_attention(qs, ks, vs, nseq, first_seq, n, attn_sink[l], ck, cv, l, rope_cos, rope_sin)
            else:
                y = _ctx_attention(qs, ks, vs, nseq, n, attn_sink[l])
            y_att.append(y.reshape(nseq * n, -1))
        new_k.append(k[:ctx_rows].reshape(batch, seq, N_KV_HEADS, HEAD_DIM))
        new_v.append(v[:ctx_rows].reshape(batch, seq, N_KV_HEADS, HEAD_DIM))
        xs = _stage_b(x1, y_fh, y_att, mod, norm_g, w_out_b, wg, wu, wd, l, *sizes,
                      tm=EDGE_TILE if last else TOKEN_TILE, split=last)

    return (xs[0].reshape(batch, seq, d), xs[1].reshape(dec_batch, dec_seq, d),
            jnp.stack(new_k, axis=1), jnp.stack(new_v, axis=1))
```

```python
import functools
import math

import numpy as np
import jax
import jax.numpy as jnp
from jax import lax
from jax.experimental import pallas as pl
from jax.experimental.pallas import tpu as pltpu

F32 = jnp.float32
BF16 = jnp.bfloat16

D_MODEL = 1024
DEPTH = 2
GRID_W = 64
HEAD_DIM = 64
N_Q_HEADS = 8
N_KV_HEADS = 2
ATTN_WIDTH = N_Q_HEADS * HEAD_DIM
KV_WIDTH = N_KV_HEADS * HEAD_DIM
FOURIER_WIDTH = 256
FOURIER_GROUPS = 4
FOURIER_GROUP_DIM = 64
HYENA_WIDTH = 256
HYENA_ORDER = 2
HYENA_PROJ = 3 * HYENA_WIDTH
HYENA_EMB_DIM = 33
HYENA_BANDS = 16
HYENA_FILTER_WIDTH = 64
FH_WIDTH = FOURIER_WIDTH + HYENA_PROJ
ATT_WIDTH = ATTN_WIDTH + 2 * KV_WIDTH
IN_WIDTH = FH_WIDTH + ATT_WIDTH
BLOCK = 128
WINDOW = 128
ROPE_BASE = 10000.0
D_FF = 2816
N_SUB = 3
RMS_EPS = 1e-6

LANES = 128
MOD_ROWS = 16
VMEM_LIMIT = 56 * 1024 * 1024
NEG_BIG = -1e30
LOG2E = 1.4426950408889634


def _cparams(*sem):
    return pltpu.CompilerParams(dimension_semantics=sem, vmem_limit_bytes=VMEM_LIMIT)


def _rms(x, g):
    return x * lax.rsqrt(jnp.mean(x * x, axis=-1, keepdims=True) + RMS_EPS) * g


def _dot(a, b):
    return jnp.dot(a, b, preferred_element_type=F32)


def _dot_nt(a, b):
    return lax.dot_general(a, b, (((1,), (1,)), ((), ())), preferred_element_type=F32)


@functools.lru_cache(maxsize=None)
def _hyena_dft(n):
    f = np.arange(n, dtype=np.int64)[:, None]
    s = np.arange(n, dtype=np.int64)[None, :]
    ang = np.pi * ((f * s) % (2 * n)).astype(np.float64) / n
    c = np.cos(ang)
    sn = np.sin(ang)
    sn[0, :] = 1.0 - 2.0 * (np.arange(n) % 2)
    fw = np.concatenate([c, sn], axis=0).astype(np.float32)
    iw = np.concatenate([c, sn.T], axis=1).astype(np.float32)
    return fw, iw


@functools.lru_cache(maxsize=None)
def _fourier_tables(n):
    f = np.arange(n, dtype=np.int64)[:, None]
    s = np.arange(n, dtype=np.int64)[None, :]
    ang = 2.0 * np.pi * ((f * s) % n).astype(np.float64) / n
    pos = np.concatenate([np.cos(ang), -np.sin(ang)], axis=1) / math.sqrt(n)
    gd = FOURIER_GROUP_DIM
    a = np.arange(gd, dtype=np.int64)
    ang_c = 2.0 * np.pi * ((a[:, None] * a[None, :]) % gd).astype(np.float64) / gd
    bc = np.kron(np.eye(FOURIER_GROUPS), np.cos(ang_c)) / math.sqrt(gd)
    bs = np.kron(np.eye(FOURIER_GROUPS), np.sin(ang_c)) / math.sqrt(gd)
    chan = np.concatenate([bc, bs], axis=1)
    return pos.astype(np.float32), chan.astype(np.float32)


@functools.lru_cache(maxsize=None)
def _rope_tables(n):
    half = HEAD_DIM // 2
    inv = ROPE_BASE ** (-np.arange(0, half, 2, dtype=np.float64) / half)
    t = np.arange(n)
    row = (t // GRID_W).astype(np.float64)
    col = (t % GRID_W).astype(np.float64)
    lane = np.arange(LANES)
    d = lane % HEAD_DIM
    pos = np.where((d // half)[None, :] == 0, row[:, None], col[:, None])
    ang = pos * inv[d % (half // 2)][None, :]
    sign = np.where((d % half) < half // 2, -1.0, 1.0)[None, :]
    return np.cos(ang).astype(np.float32), (np.sin(ang) * sign).astype(np.float32)


def _hyena_feats(n):
    d = jnp.arange(n, dtype=F32)
    t = jnp.linspace(0.0, 1.0, n, dtype=F32)[:, None]
    f = jnp.linspace(1e-4, HYENA_BANDS - 1, HYENA_BANDS, dtype=F32)
    ang = (2.0 * math.pi / n) * d[:, None] * f[None, :]
    feats = jnp.concatenate([t, jnp.cos(ang), -jnp.sin(ang)], axis=-1)
    return jnp.pad(feats, ((0, 0), (0, LANES - HYENA_EMB_DIM))), t


def _mod_kernel(c_ref, w_ref, b_ref, o_ref):
    c = c_ref[...]
    s = c / (1.0 + jnp.exp(-c))
    o_ref[0] = _dot(s.astype(BF16), w_ref[0].astype(BF16)) + b_ref[0]


def _modulation(cvecs, w_mod, b_mod):
    depth, _, width = w_mod.shape
    tn = 1536
    return pl.pallas_call(
        _mod_kernel,
        grid=(depth, width // tn),
        in_specs=[
            pl.BlockSpec((MOD_ROWS, D_MODEL), lambda l, j: (0, 0)),
            pl.BlockSpec((1, D_MODEL, tn), lambda l, j: (l, 0, j)),
            pl.BlockSpec((1, 1, tn), lambda l, j: (l, 0, j)),
        ],
        out_specs=pl.BlockSpec((1, MOD_ROWS, tn), lambda l, j: (l, 0, j)),
        out_shape=jax.ShapeDtypeStruct((depth, MOD_ROWS, width), F32),
        compiler_params=_cparams("arbitrary", "arbitrary"),
        name="modulation",
    )(cvecs, w_mod, b_mod.reshape(depth, 1, width))


def _resident(block_shape, index_map):
    return pl.BlockSpec(block_shape, index_map, pipeline_mode=pl.Buffered(1))


ROW_GROUP = 256
TOKEN_TILE = 1024
EDGE_TILE = 512


def _modulate(x, g_row, mod_ref, sub):
    return (_rms(x, g_row) * (1.0 + mod_ref[0, sub, 1:2]) + mod_ref[0, sub, 0:1]).astype(BF16)


def _swiglu_rows(x, h, mod_ref, sub, g_ref, wg_ref, wu_ref, wd_ref):
    gate = _dot(h, wg_ref[...])
    up = _dot(h, wu_ref[...])
    act = (gate / (1.0 + jnp.exp(-gate))) * up
    y = _dot(act.astype(BF16), wd_ref[...])
    return x + (0.5 * mod_ref[0, sub, 2:3]) * _rms(y, g_ref[2 * sub + 1:2 * sub + 2])


def _software_pipeline(rows, phases):
    groups = [slice(r, r + ROW_GROUP) for r in range(0, rows, ROW_GROUP)]
    state = {}
    for turn in range(len(groups) + len(phases) - 1):
        for p, phase in enumerate(phases):
            g = turn - p
            if 0 <= g < len(groups):
                state[g] = phase(groups[g], state.get(g))


def _stage_a_kernel(*refs, n_in, ctx_tiles):
    x_refs = refs[:n_in]
    mod_ref, g_ref, wg_ref, wu_ref, wd_ref, win_ref, x1_ref, fh_ref, q_ref, k_ref, v_ref = refs[n_in:]
    is_ctx = pl.program_id(0) < ctx_tiles
    k0 = FH_WIDTH + ATTN_WIDTH

    def modulated_input(sl, _):
        x = x_refs[0][sl] if n_in == 1 else jnp.where(is_ctx, x_refs[0][sl], x_refs[1][sl])
        return x, _modulate(x, g_ref[0:1], mod_ref, 0)

    def half_step(sl, xh):
        x1 = _swiglu_rows(*xh, mod_ref, 0, g_ref, wg_ref, wu_ref, wd_ref)
        x1_ref[sl] = x1
        return _modulate(x1, g_ref[2:3], mod_ref, 1)

    def in_projection(sl, h):
        p = _dot(h, win_ref[...])
        fh_ref[sl] = p[:, :FH_WIDTH].astype(fh_ref.dtype)
        q_ref[sl] = p[:, FH_WIDTH:k0].astype(q_ref.dtype)
        k_ref[sl] = p[:, k0:k0 + KV_WIDTH]
        v_ref[sl] = p[:, k0 + KV_WIDTH:]

    _software_pipeline(x1_ref.shape[0], [modulated_input, half_step, in_projection])


def _stage_b_kernel(x_ref, fhc_ref, fhl_ref, atc_ref, atl_ref, mod_ref, g_ref, wout_ref, wg_ref, wu_ref,
                    wd_ref, *o_refs, ctx_tiles):
    is_ctx = pl.program_id(0) < ctx_tiles
    half = FOURIER_WIDTH + HYENA_WIDTH

    def body(mixed, o_ref):
        def out_projection(sl, _):
            y_fh, y_at = mixed(sl)
            y = _dot(y_fh, wout_ref[:half]) + _dot(y_at, wout_ref[half:])
            x2 = x_ref[sl] + mod_ref[0, 1, 2:3] * _rms(y, g_ref[3:4])
            return x2, _modulate(x2, g_ref[4:5], mod_ref, 2)

        def half_step(sl, xh):
            o_ref[sl] = _swiglu_rows(*xh, mod_ref, 2, g_ref, wg_ref, wu_ref, wd_ref)

        _software_pipeline(x_ref.shape[0], [out_projection, half_step])

    if len(o_refs) == 1:
        body(lambda sl: (jnp.where(is_ctx, fhc_ref[sl], fhl_ref[sl]),
                         jnp.where(is_ctx, atc_ref[sl], atl_ref[sl])), o_refs[0])
    else:
        pl.when(is_ctx)(lambda: body(lambda sl: (fhc_ref[sl], atc_ref[sl]), o_refs[0]))
        pl.when(jnp.logical_not(is_ctx))(lambda: body(lambda sl: (fhl_ref[sl], atl_ref[sl]), o_refs[1]))


def _token_specs(tm, ctx_rows, lat_rows, lat_seq):
    ctx_tiles = ctx_rows // tm
    tiles_per_seq = lat_seq // tm
    tile = lambda width: pl.BlockSpec((tm, width), lambda i: (i, 0))
    ctx_tile = lambda width: pl.BlockSpec((tm, width), lambda i: (jnp.minimum(i, ctx_tiles - 1), 0))
    lat_tile = lambda width: pl.BlockSpec((tm, width), lambda i: (jnp.maximum(i - ctx_tiles, 0), 0))
    request = lambda i: jnp.where(i < ctx_tiles, 0, 1 + (i - ctx_tiles) // tiles_per_seq)
    return ctx_tiles, (ctx_rows + lat_rows) // tm, tile, ctx_tile, lat_tile, request


def _stage_a(xs, mod, norm_g, wg, wu, wd, w_in, layer, ctx_rows, lat_rows, lat_seq, tm):
    ctx_tiles, tiles, tile, ctx_tile, lat_tile, request = _token_specs(tm, ctx_rows, lat_rows, lat_seq)
    t = ctx_rows + lat_rows
    widths = (D_MODEL, FH_WIDTH, ATTN_WIDTH, KV_WIDTH, KV_WIDTH)
    dtypes = (F32, BF16, BF16, F32, F32)
    x_specs = [tile(D_MODEL)] if len(xs) == 1 else [ctx_tile(D_MODEL), lat_tile(D_MODEL)]
    return pl.pallas_call(
        functools.partial(_stage_a_kernel, n_in=len(xs), ctx_tiles=ctx_tiles),
        grid=(tiles,),
        in_specs=x_specs + [
            pl.BlockSpec((None, 1, N_SUB, 3, D_MODEL), lambda i: (layer, request(i), 0, 0, 0)),
            _resident((None, 2 * N_SUB, D_MODEL), lambda i: (layer, 0, 0)),
            _resident((None, None, D_MODEL, D_FF), lambda i: (layer, 0, 0, 0)),
            _resident((None, None, D_MODEL, D_FF), lambda i: (layer, 0, 0, 0)),
            _resident((None, None, D_FF, D_MODEL), lambda i: (layer, 0, 0, 0)),
            _resident((None, D_MODEL, IN_WIDTH), lambda i: (layer, 0, 0)),
        ],
        out_specs=[tile(w) for w in widths],
        out_shape=[jax.ShapeDtypeStruct((t, w), dt) for w, dt in zip(widths, dtypes)],
        compiler_params=_cparams("parallel"),
        name="stage_a",
    )(*xs, mod, norm_g, wg, wu, wd, w_in)


def _stage_b(x1, y_fh, y_att, mod, norm_g, w_out, wg, wu, wd, layer, ctx_rows, lat_rows, lat_seq, tm, split):
    ctx_tiles, tiles, tile, ctx_tile, lat_tile, request = _token_specs(tm, ctx_rows, lat_rows, lat_seq)
    half = FOURIER_WIDTH + HYENA_WIDTH
    if split:
        out_specs = [ctx_tile(D_MODEL), lat_tile(D_MODEL)]
        out_shape = [jax.ShapeDtypeStruct((r, D_MODEL), F32) for r in (ctx_rows, lat_rows)]
    else:
        out_specs = [tile(D_MODEL)]
        out_shape = [jax.ShapeDtypeStruct((ctx_rows + lat_rows, D_MODEL), F32)]
    return pl.pallas_call(
        functools.partial(_stage_b_kernel, ctx_tiles=ctx_tiles),
        grid=(tiles,),
        in_specs=[
            tile(D_MODEL), ctx_tile(half), lat_tile(half), ctx_tile(ATTN_WIDTH), lat_tile(ATTN_WIDTH),
            pl.BlockSpec((None, 1, N_SUB, 3, D_MODEL), lambda i: (layer, request(i), 0, 0, 0)),
            _resident((None, 2 * N_SUB, D_MODEL), lambda i: (layer, 0, 0)),
            _resident((None, D_MODEL, D_MODEL), lambda i: (layer, 0, 0)),
            _resident((None, None, D_MODEL, D_FF), lambda i: (layer, 1, 0, 0)),
            _resident((None, None, D_MODEL, D_FF), lambda i: (layer, 1, 0, 0)),
            _resident((None, None, D_FF, D_MODEL), lambda i: (layer, 1, 0, 0)),
        ],
        out_specs=out_specs,
        out_shape=out_shape,
        compiler_params=_cparams("arbitrary" if split else "parallel"),
        name="stage_b",
    )(x1, y_fh[0], y_fh[1], y_att[0], y_att[1], mod, norm_g, w_out, wg, wu, wd)


def _filter_kernel(feats_ref, t_ref, w1_ref, b1_ref, w2_ref, b2_ref, w3_ref, fr_ref, decay_ref,
                   fw_ref, o_ref, *, n):
    fr = fr_ref[...]
    h = jnp.sin(fr * (_dot(feats_ref[...], w1_ref[...]) + b1_ref[...]))
    h = jnp.sin(fr * (_dot(h, w2_ref[...]) + b2_ref[...]))
    h = _dot(h, w3_ref[...])
    window = jnp.exp(-t_ref[...] * jnp.abs(decay_ref[...]))
    width = HYENA_ORDER * HYENA_WIDTH
    row = lax.broadcasted_iota(jnp.int32, (n, width), 0)
    fwd = h[:, :width] * window
    bwd = jnp.where(row == 0, 0.0, h[:, width:] * window)
    even = fwd + bwd
    odd = fwd - bwd
    k_re = _dot(fw_ref[:n], even.astype(BF16))
    k_sn = _dot(fw_ref[n:], odd.astype(BF16))
    sign = (1 - 2 * (row % 2)).astype(F32)
    k_ny = jnp.sum(even * sign, axis=0, keepdims=True)
    s0 = 1.0 / (4.0 * n * n)
    scale = jnp.where(row == 0, s0, 2.0 * s0)
    a = scale * k_re
    o_ref[0] = a
    o_ref[1] = jnp.where(row == 0, 0.0, scale * k_sn)
    o_ref[2] = jnp.where(row == 0, s0 * k_ny, a)


def _filter_spectrum(n, fw, w1, b1, w2, b2, w3, freq, decay):
    feats, t = _hyena_feats(n)
    pad_w = LANES - HYENA_FILTER_WIDTH
    w1p = jnp.pad(w1, ((0, LANES - HYENA_EMB_DIM), (0, pad_w)))
    w2p = jnp.pad(w2, ((0, pad_w), (0, pad_w)))
    w3p = jnp.pad(w3, ((0, pad_w), (0, 0)))
    padv = lambda v: jnp.pad(v, (0, pad_w)).reshape(1, LANES)
    width = HYENA_ORDER * HYENA_WIDTH
    args = (feats, t, w1p, padv(b1), w2p, padv(b2), w3p, padv(freq), decay.reshape(1, width), fw)
    return pl.pallas_call(
        functools.partial(_filter_kernel, n=n),
        out_shape=jax.ShapeDtypeStruct((3, n, width), F32),
        compiler_params=pltpu.CompilerParams(vmem_limit_bytes=VMEM_LIMIT),
        name="hyena_filter",
    )(*args)


def _fh_kernel(p_ref, pos_ref, chan_ref, fw_ref, iw_ref, coef_ref, convw_ref, bias_ref, o_ref,
               st_scr, *, n, group):
    c = HYENA_WIDTH
    cols = [slice(b * c, (b + 1) * c) for b in range(group)]

    for b in range(group):
        t = _dot(p_ref[b, :, :FOURIER_WIDTH], chan_ref[...])
        st_scr[:n, cols[b]] = t[:, :FOURIER_WIDTH].astype(BF16)
        st_scr[n:, cols[b]] = t[:, FOURIER_WIDTH:].astype(BF16)
    y_f = _dot(pos_ref[...], st_scr[...])
    for b in range(group):
        o_ref[b, :, :FOURIER_WIDTH] = y_f[:, cols[b]].astype(o_ref.dtype)

    row = lax.broadcasted_iota(jnp.int32, (n, HYENA_PROJ), 0)
    zs = []
    for b in range(group):
        z = p_ref[b, :, FOURIER_WIDTH:].astype(F32)
        z_prev = jnp.where(row == 0, 0.0, pltpu.roll(z, 1, axis=0))
        z_next = jnp.where(row == n - 1, 0.0, pltpu.roll(z, n - 1, axis=0))
        zs.append(z_prev * convw_ref[0:1] + z * convw_ref[1:2] + z_next * convw_ref[2:3])

    def long_conv(vs, order):
        oc = slice(order * c, (order + 1) * c)
        uf = _dot(fw_ref[...], jnp.concatenate([v.astype(BF16) for v in vs], axis=1))
        a, bb, a2 = coef_ref[0, :, oc], coef_ref[1, :, oc], coef_ref[2, :, oc]
        for b in range(group):
            u_re, u_sn = uf[:n, cols[b]], uf[n:, cols[b]]
            st_scr[:n, cols[b]] = (u_re * a - u_sn * bb).astype(BF16)
            st_scr[n:, cols[b]] = (u_re * bb + u_sn * a2).astype(BF16)
        y = _dot(iw_ref[...], st_scr[...])
        return [y[:, cols[b]] + vs[b] * bias_ref[order:order + 1] for b in range(group)]

    ys = long_conv([z[:, :c] for z in zs], 0)
    ys = long_conv([z[:, c:2 * c] * y for z, y in zip(zs, ys)], 1)
    for b in range(group):
        o_ref[b, :, FOURIER_WIDTH:] = (zs[b][:, 2 * c:] * ys[b]).astype(o_ref.dtype)


def _fh_mix(p_fh, b, first, n, group, pos, chan, fw, iw, coef, conv_w, hbias):
    width = FOURIER_WIDTH + HYENA_WIDTH
    const = lambda shape: _resident(shape, lambda i: (0,) * len(shape))
    return pl.pallas_call(
        functools.partial(_fh_kernel, n=n, group=group),
        grid=(b // group,),
        in_specs=[
            pl.BlockSpec((group, n, FH_WIDTH), lambda i: (i + first // group, 0, 0)),
            const((n, 2 * n)), const((FOURIER_WIDTH, 2 * FOURIER_WIDTH)),
            const((2 * n, n)), const((n, 2 * n)),
            const((3, n, HYENA_ORDER * HYENA_WIDTH)),
            const((3, HYENA_PROJ)), const((HYENA_ORDER, HYENA_WIDTH)),
        ],
        out_specs=pl.BlockSpec((group, n, width), lambda i: (i, 0, 0)),
        out_shape=jax.ShapeDtypeStruct((b, n, width), BF16),
        scratch_shapes=[pltpu.VMEM((2 * n, group * HYENA_WIDTH), BF16)],
        compiler_params=_cparams("parallel"),
        name="fourier_hyena",
    )(p_fh, pos, chan, fw, iw, coef, conv_w, hbias)


def _head_slots(x, kv_head, lo):
    xr = pltpu.roll(x, HEAD_DIM, axis=1)
    if kv_head == 0:
        return jnp.where(lo, x, 0.0), jnp.where(lo, 0.0, xr)
    return jnp.where(lo, xr, 0.0), jnp.where(lo, 0.0, x)


def _slot_ones(rows, slot):
    lo = lax.broadcasted_iota(jnp.int32, (rows, LANES), 1) < HEAD_DIM
    return jnp.where(lo if slot == 0 else jnp.logical_not(lo), 1.0, 0.0).astype(BF16)


def _ctx_attn_kernel(sink_ref, q_ref, k_ref, v_ref, o_ref, *, n, group):
    lo = lax.broadcasted_iota(jnp.int32, (n, LANES), 1) < HEAD_DIM
    top = lax.broadcasted_iota(jnp.int32, (2 * n, 1), 0) < n
    lo_g = lax.broadcasted_iota(jnp.int32, (2 * n, LANES), 1) < HEAD_DIM
    ones = jnp.concatenate([_slot_ones(n, 0), _slot_ones(n, 1)], axis=0)
    for b in range(group):
        k = k_ref[b] * LOG2E
        v = v_ref[b]
        for kv_head in range(N_KV_HEADS):
            h0 = 4 * kv_head
            k_cat = jnp.concatenate(_head_slots(k, kv_head, lo), axis=0).astype(BF16)
            v_cat = jnp.concatenate(_head_slots(v, kv_head, lo), axis=0).astype(BF16)
            v_cat = jnp.concatenate([v_cat, ones], axis=1)
            q2 = jnp.concatenate([q_ref[b, :, (2 * kv_head + t) * LANES:(2 * kv_head + t + 1) * LANES]
                                  for t in range(2)], axis=0) * HEAD_DIM ** -0.5
            s = _dot_nt(q2, k_cat)
            e, sink_term = [], []
            for slot in range(2):
                sink = jnp.where(top, sink_ref[h0 + slot], sink_ref[h0 + 2 + slot]) * LOG2E
                cols = s[:, slot * n:(slot + 1) * n]
                m = jnp.maximum(jnp.max(cols, axis=-1, keepdims=True), sink)
                e.append(jnp.exp2(cols - m).astype(BF16))
                sink_term.append(jnp.exp2(sink - m))
            o = _dot(jnp.concatenate(e, axis=1), v_cat)
            o = o[:, :LANES] / (o[:, LANES:] + jnp.where(lo_g, sink_term[0], sink_term[1]))
            for t in range(2):
                tile = 2 * kv_head + t
                o_ref[b, :, tile * LANES:(tile + 1) * LANES] = o[t * n:(t + 1) * n].astype(o_ref.dtype)


def _ctx_attention(q, k, v, b, n, sink, group=4):
    return pl.pallas_call(
        functools.partial(_ctx_attn_kernel, n=n, group=group),
        grid=(b // group,),
        in_specs=[
            pl.BlockSpec(memory_space=pltpu.SMEM),
            pl.BlockSpec((group, n, ATTN_WIDTH), lambda i: (i, 0, 0)),
            pl.BlockSpec((group, n, KV_WIDTH), lambda i: (i, 0, 0)),
            pl.BlockSpec((group, n, KV_WIDTH), lambda i: (i, 0, 0)),
        ],
        out_specs=pl.BlockSpec((group, n, ATTN_WIDTH), lambda i: (i, 0, 0)),
        out_shape=jax.ShapeDtypeStruct((b, n, ATTN_WIDTH), BF16),
        compiler_params=_cparams("parallel"),
        name="ctx_attention",
    )(sink, q, k, v)


def _rope(x, cos, sin_signed):
    lane = lax.broadcasted_iota(jnp.int32, x.shape, 1)
    first = (lane % (HEAD_DIM // 2)) < HEAD_DIM // 4
    partner = jnp.where(first, pltpu.roll(x, LANES - HEAD_DIM // 4, axis=1),
                        pltpu.roll(x, HEAD_DIM // 4, axis=1))
    return x * cos + partner * sin_signed


def _lat_attn_kernel(sink_ref, q_ref, k_ref, v_ref, ck_ref, cv_ref, cos_ref, sin_ref, o_ref,
                     q_scr, k_scr, v_scr, ck_scr, cv_scr, *, n, c_len):
    nb = n // BLOCK
    cb = c_len // BLOCK
    grp = 2 * BLOCK
    cos, sin = cos_ref[...], sin_ref[...]
    lo = lax.broadcasted_iota(jnp.int32, (n, LANES), 1) < HEAD_DIM
    lo_c = lax.broadcasted_iota(jnp.int32, (c_len, LANES), 1) < HEAD_DIM

    k = _rope(k_ref[0], cos, sin) * LOG2E
    v = v_ref[0]
    ck = ck_ref[0, 0] * LOG2E
    for kv_head in range(N_KV_HEADS):
        for t in range(2):
            tile = 2 * kv_head + t
            q = q_ref[0, :, tile * LANES:(tile + 1) * LANES].astype(F32)
            q = (_rope(q, cos, sin) * HEAD_DIM ** -0.5).astype(BF16)
            for i in range(nb):
                q_scr[kv_head, i * grp + t * BLOCK:i * grp + (t + 1) * BLOCK] = q[i * BLOCK:(i + 1) * BLOCK]
        for slot, (ks, vs, cks, cvs) in enumerate(zip(
                _head_slots(k, kv_head, lo), _head_slots(v, kv_head, lo),
                _head_slots(ck, kv_head, lo_c), _head_slots(cv_ref[0, 0], kv_head, lo_c))):
            ks, vs, cks, cvs = (a.astype(BF16) for a in (ks, vs, cks, cvs))
            ones = _slot_ones(BLOCK, slot)
            for j in range(nb):
                rows = slice(j * grp + slot * BLOCK, j * grp + (slot + 1) * BLOCK)
                k_scr[kv_head, rows] = ks[j * BLOCK:(j + 1) * BLOCK]
                v_scr[kv_head, rows, :LANES] = vs[j * BLOCK:(j + 1) * BLOCK]
                v_scr[kv_head, rows, LANES:] = ones
            for j in range(cb):
                rows = slice(j * grp + slot * BLOCK, j * grp + (slot + 1) * BLOCK)
                ck_scr[kv_head, rows] = cks[j * BLOCK:(j + 1) * BLOCK]
                cv_scr[kv_head, rows, :LANES] = cvs[j * BLOCK:(j + 1) * BLOCK]
                cv_scr[kv_head, rows, LANES:] = ones

    qi = lax.broadcasted_iota(jnp.int32, (grp, BLOCK), 0) % BLOCK
    kj = lax.broadcasted_iota(jnp.int32, (grp, BLOCK), 1)
    keep_prev = kj >= qi
    keep_next = kj <= qi
    top = lax.broadcasted_iota(jnp.int32, (grp, 1), 0) < BLOCK
    lo_g = lax.broadcasted_iota(jnp.int32, (grp, LANES), 1) < HEAD_DIM

    def lane_tiles(s):
        return [s[:, c * BLOCK:(c + 1) * BLOCK] for c in range(s.shape[1] // BLOCK)]

    for kv_head in range(N_KV_HEADS):
        h0 = 4 * kv_head
        sinks = [jnp.where(top, sink_ref[h0 + slot], sink_ref[h0 + 2 + slot]) * LOG2E for slot in range(2)]
        for i in range(nb):
            j0, j1 = max(i - 1, 0), min(i + 2, nb)
            q2 = q_scr[kv_head, i * grp:(i + 1) * grp]
            tiles = lane_tiles(_dot_nt(q2, k_scr[kv_head, j0 * grp:j1 * grp]))
            for b, j in enumerate(range(j0, j1)):
                for slot in range(2):
                    if j == i - 1:
                        tiles[2 * b + slot] = jnp.where(keep_prev, tiles[2 * b + slot], NEG_BIG)
                    elif j == i + 1:
                        tiles[2 * b + slot] = jnp.where(keep_next, tiles[2 * b + slot], NEG_BIG)
            tiles += lane_tiles(_dot_nt(q2, ck_scr[kv_head]))
            sink_term = []
            for slot in range(2):
                mine = tiles[slot::2]
                m = jnp.maximum(jnp.max(functools.reduce(jnp.maximum, mine), axis=-1, keepdims=True),
                                sinks[slot])
                tiles[slot::2] = [jnp.exp2(tl - m).astype(BF16) for tl in mine]
                sink_term.append(jnp.exp2(sinks[slot] - m))
            n_loc = 2 * (j1 - j0)
            e_loc = jnp.concatenate(tiles[:n_loc], axis=1)
            e_ctx = jnp.concatenate(tiles[n_loc:], axis=1)
            o = _dot(e_loc, v_scr[kv_head, j0 * grp:j1 * grp]) + _dot(e_ctx, cv_scr[kv_head])
            o = o[:, :LANES] / (o[:, LANES:] + jnp.where(lo_g, sink_term[0], sink_term[1]))
            o = o.astype(o_ref.dtype)
            for t in range(2):
                tile = 2 * kv_head + t
                o_ref[0, i * BLOCK:(i + 1) * BLOCK, tile * LANES:(tile + 1) * LANES] = o[t * BLOCK:(t + 1) * BLOCK]


def _lat_attention(q, k, v, b, first, n, sink, cache_k, cache_v, layer, cos, sin):
    c_len = cache_k.shape[2]
    return pl.pallas_call(
        functools.partial(_lat_attn_kernel, n=n, c_len=c_len),
        grid=(b,),
        in_specs=[
            pl.BlockSpec(memory_space=pltpu.SMEM),
            pl.BlockSpec((1, n, ATTN_WIDTH), lambda i: (i + first, 0, 0)),
            pl.BlockSpec((1, n, KV_WIDTH), lambda i: (i + first, 0, 0)),
            pl.BlockSpec((1, n, KV_WIDTH), lambda i: (i + first, 0, 0)),
            pl.BlockSpec((1, 1, c_len, KV_WIDTH), lambda i: (i, layer, 0, 0)),
            pl.BlockSpec((1, 1, c_len, KV_WIDTH), lambda i: (i, layer, 0, 0)),
            _resident((n, LANES), lambda i: (0, 0)),
            _resident((n, LANES), lambda i: (0, 0)),
        ],
        out_specs=pl.BlockSpec((1, n, ATTN_WIDTH), lambda i: (i, 0, 0)),
        out_shape=jax.ShapeDtypeStruct((b, n, ATTN_WIDTH), BF16),
        scratch_shapes=[
            pltpu.VMEM((N_KV_HEADS, 2 * n, LANES), BF16),
            pltpu.VMEM((N_KV_HEADS, 2 * n, LANES), BF16),
            pltpu.VMEM((N_KV_HEADS, 2 * n, 2 * LANES), BF16),
            pltpu.VMEM((N_KV_HEADS, 2 * c_len, LANES), BF16),
            pltpu.VMEM((N_KV_HEADS, 2 * c_len, 2 * LANES), BF16),
        ],
        compiler_params=_cparams("parallel"),
        name="latent_attention",
    )(sink, q, k, v, cache_k, cache_v, cos, sin)


def kernel(x_prompt, x_sample, cache_k, cache_v, c, c_ctx, w_mod, b_mod, norm_g, ffn_w_gate, ffn_w_up,
           ffn_w_down, w_in, w_out, hyena_conv_w, hyena_f_w1, hyena_f_b1, hyena_f_w2, hyena_f_b2,
           hyena_f_w3, hyena_f_freq, hyena_decay, hyena_bias, attn_sink):
    batch, seq, d = x_prompt.shape
    dec_batch, dec_seq, _ = x_sample.shape
    past_len = cache_k.shape[2]

    cvecs = jnp.concatenate([c_ctx[None], c], axis=0)
    cvecs = jnp.pad(cvecs, ((0, MOD_ROWS - cvecs.shape[0]), (0, 0)))
    mod = _modulation(cvecs, w_mod, b_mod).reshape(DEPTH, MOD_ROWS, N_SUB, 3, d)

    wg = ffn_w_gate.astype(BF16)
    wu = ffn_w_up.astype(BF16)
    wd = ffn_w_down.astype(BF16)
    w_in_b = w_in.astype(BF16)
    w_out_b = w_out.astype(BF16)
    ck = cache_k.reshape(dec_batch, DEPTH, past_len, KV_WIDTH)
    cv = cache_v.reshape(dec_batch, DEPTH, past_len, KV_WIDTH)
    rope_cos, rope_sin = (jnp.asarray(t) for t in _rope_tables(dec_seq))

    tables = {}
    for n in (seq, dec_seq):
        fw, iw = (jnp.asarray(t).astype(BF16) for t in _hyena_dft(n))
        pos, chan = (jnp.asarray(t).astype(BF16) for t in _fourier_tables(n))
        tables[n] = (fw, iw, pos, chan)

    ctx_rows, lat_rows = batch * seq, dec_batch * dec_seq
    rows = ctx_rows + lat_rows
    sizes = (ctx_rows, lat_rows, dec_seq)
    xs = (x_prompt.reshape(ctx_rows, d), x_sample.reshape(lat_rows, d))
    new_k, new_v = [], []
    for l in range(DEPTH):
        first, last = l == 0, l == DEPTH - 1
        x1, p_fh, q, k, v = _stage_a(xs, mod, norm_g, wg, wu, wd, w_in_b, l, *sizes,
                                     tm=EDGE_TILE if first else TOKEN_TILE)
        y_fh, y_att = [], []
        for n, nseq, first_seq, latent in ((seq, batch, 0, False), (dec_seq, dec_batch, ctx_rows // dec_seq, True)):
            fw, iw, pos, chan = tables[n]
            coef = _filter_spectrum(n, fw, hyena_f_w1[l], hyena_f_b1[l], hyena_f_w2[l], hyena_f_b2[l],
                                    hyena_f_w3[l], hyena_f_freq[l], hyena_decay[l])
            y = _fh_mix(p_fh.reshape(rows // n, n, FH_WIDTH), nseq, first_seq, n, 2 if latent else 8,
                        pos, chan, fw, iw, coef, hyena_conv_w[l], hyena_bias[l])
            y_fh.append(y.reshape(nseq * n, -1))
            qs, ks, vs = (a.reshape(rows // n, n, -1) for a in (q, k, v))
            if latent:
                y = _lat_attention(qs, ks, vs, nseq, first_seq, n, attn_sink[l], ck, cv, l, rope_cos, rope_sin)
            else:
                y = _ctx_attention(qs, ks, vs, nseq, n, attn_sink[l])
            y_att.append(y.reshape(nseq * n, -1))
        new_k.append(k[:ctx_rows].reshape(batch, seq, N_KV_HEADS, HEAD_DIM))
        new_v.append(v[:ctx_rows].reshape(batch, seq, N_KV_HEADS, HEAD_DIM))
        xs = _stage_b(x1, y_fh, y_att, mod, norm_g, w_out_b, wg, wu, wd, l, *sizes,
                      tm=EDGE_TILE if last else TOKEN_TILE, split=last)

    return (xs[0].reshape(batch, seq, d), xs[1].reshape(dec_batch, dec_seq, d),
            jnp.stack(new_k, axis=1), jnp.stack(new_v, axis=1))
```

```python
import functools
import math

import numpy as np
import jax
import jax.numpy as jnp
from jax import lax
from jax.experimental import pallas as pl
from jax.experimental.pallas import tpu as pltpu

F32 = jnp.float32
BF16 = jnp.bfloat16

D_MODEL = 1024
DEPTH = 2
GRID_W = 64
HEAD_DIM = 64
N_Q_HEADS = 8
N_KV_HEADS = 2
ATTN_WIDTH = N_Q_HEADS * HEAD_DIM
KV_WIDTH = N_KV_HEADS * HEAD_DIM
FOURIER_WIDTH = 256
FOURIER_GROUPS = 4
FOURIER_GROUP_DIM = 64
HYENA_WIDTH = 256
HYENA_ORDER = 2
HYENA_PROJ = 3 * HYENA_WIDTH
HYENA_EMB_DIM = 33
HYENA_BANDS = 16
HYENA_FILTER_WIDTH = 64
FH_WIDTH = FOURIER_WIDTH + HYENA_PROJ
ATT_WIDTH = ATTN_WIDTH + 2 * KV_WIDTH
IN_WIDTH = FH_WIDTH + ATT_WIDTH
BLOCK = 128
WINDOW = 128
ROPE_BASE = 10000.0
D_FF = 2816
N_SUB = 3
RMS_EPS = 1e-6

LANES = 128
MOD_ROWS = 16
VMEM_LIMIT = 56 * 1024 * 1024
NEG_BIG = -1e30
LOG2E = 1.4426950408889634


def _cparams(*sem):
    return pltpu.CompilerParams(dimension_semantics=sem, vmem_limit_bytes=VMEM_LIMIT)


def _rms(x, g):
    return x * lax.rsqrt(jnp.mean(x * x, axis=-1, keepdims=True) + RMS_EPS) * g


def _dot(a, b):
    return jnp.dot(a, b, preferred_element_type=F32)


def _dot_nt(a, b):
    return lax.dot_general(a, b, (((1,), (1,)), ((), ())), preferred_element_type=F32)


@functools.lru_cache(maxsize=None)
def _hyena_dft(n):
    f = np.arange(n, dtype=np.int64)[:, None]
    s = np.arange(n, dtype=np.int64)[None, :]
    ang = np.pi * ((f * s) % (2 * n)).astype(np.float64) / n
    c = np.cos(ang)
    sn = np.sin(ang)
    sn[0, :] = 1.0 - 2.0 * (np.arange(n) % 2)
    fw = np.concatenate([c, sn], axis=0).astype(np.float32)
    iw = np.concatenate([c, sn.T], axis=1).astype(np.float32)
    return fw, iw


@functools.lru_cache(maxsize=None)
def _fourier_tables(n):
    f = np.arange(n, dtype=np.int64)[:, None]
    s = np.arange(n, dtype=np.int64)[None, :]
    ang = 2.0 * np.pi * ((f * s) % n).astype(np.float64) / n
    pos = np.concatenate([np.cos(ang), -np.sin(ang)], axis=1) / math.sqrt(n)
    gd = FOURIER_GROUP_DIM
    a = np.arange(gd, dtype=np.int64)
    ang_c = 2.0 * np.pi * ((a[:, None] * a[None, :]) % gd).astype(np.float64) / gd
    bc = np.kron(np.eye(FOURIER_GROUPS), np.cos(ang_c)) / math.sqrt(gd)
    bs = np.kron(np.eye(FOURIER_GROUPS), np.sin(ang_c)) / math.sqrt(gd)
    chan = np.concatenate([bc, bs], axis=1)
    return pos.astype(np.float32), chan.astype(np.float32)


@functools.lru_cache(maxsize=None)
def _rope_tables(n):
    half = HEAD_DIM // 2
    inv = ROPE_BASE ** (-np.arange(0, half, 2, dtype=np.float64) / half)
    t = np.arange(n)
    row = (t // GRID_W).astype(np.float64)
    col = (t % GRID_W).astype(np.float64)
    lane = np.arange(LANES)
    d = lane % HEAD_DIM
    pos = np.where((d // half)[None, :] == 0, row[:, None], col[:, None])
    ang = pos * inv[d % (half // 2)][None, :]
    sign = np.where((d % half) < half // 2, -1.0, 1.0)[None, :]
    return np.cos(ang).astype(np.float32), (np.sin(ang) * sign).astype(np.float32)


def _hyena_feats(n):
    d = jnp.arange(n, dtype=F32)
    t = jnp.linspace(0.0, 1.0, n, dtype=F32)[:, None]
    f = jnp.linspace(1e-4, HYENA_BANDS - 1, HYENA_BANDS, dtype=F32)
    ang = (2.0 * math.pi / n) * d[:, None] * f[None, :]
    feats = jnp.concatenate([t, jnp.cos(ang), -jnp.sin(ang)], axis=-1)
    return jnp.pad(feats, ((0, 0), (0, LANES - HYENA_EMB_DIM))), t


def _mod_kernel(c_ref, w_ref, b_ref, o_ref):
    c = c_ref[...]
    s = c / (1.0 + jnp.exp(-c))
    o_ref[0] = _dot(s.astype(BF16), w_ref[0].astype(BF16)) + b_ref[0]


def _modulation(cvecs, w_mod, b_mod):
    depth, _, width = w_mod.shape
    tn = 1536
    return pl.pallas_call(
        _mod_kernel,
        grid=(depth, width // tn),
        in_specs=[
            pl.BlockSpec((MOD_ROWS, D_MODEL), lambda l, j: (0, 0)),
            pl.BlockSpec((1, D_MODEL, tn), lambda l, j: (l, 0, j)),
            pl.BlockSpec((1, 1, tn), lambda l, j: (l, 0, j)),
        ],
        out_specs=pl.BlockSpec((1, MOD_ROWS, tn), lambda l, j: (l, 0, j)),
        out_shape=jax.ShapeDtypeStruct((depth, MOD_ROWS, width), F32),
        compiler_params=_cparams("arbitrary", "arbitrary"),
        name="modulation",
    )(cvecs, w_mod, b_mod.reshape(depth, 1, width))


def _resident(block_shape, index_map):
    return pl.BlockSpec(block_shape, index_map, pipeline_mode=pl.Buffered(1))


ROW_GROUP = 256
TOKEN_TILE = 1024
EDGE_TILE = 512


def _modulate(x, g_row, mod_ref, sub):
    return (_rms(x, g_row) * (1.0 + mod_ref[0, sub, 1:2]) + mod_ref[0, sub, 0:1]).astype(BF16)


def _swiglu_rows(x, h, mod_ref, sub, g_ref, wg_ref, wu_ref, wd_ref):
    gate = _dot(h, wg_ref[...])
    up = _dot(h, wu_ref[...])
    act = (gate / (1.0 + jnp.exp(-gate))) * up
    y = _dot(act.astype(BF16), wd_ref[...])
    return x + (0.5 * mod_ref[0, sub, 2:3]) * _rms(y, g_ref[2 * sub + 1:2 * sub + 2])


def _software_pipeline(rows, phases):
    groups = [slice(r, r + ROW_GROUP) for r in range(0, rows, ROW_GROUP)]
    state = {}
    for turn in range(len(groups) + len(phases) - 1):
        for p, phase in enumerate(phases):
            g = turn - p
            if 0 <= g < len(groups):
                state[g] = phase(groups[g], state.get(g))


def _stage_a_kernel(*refs, n_in, ctx_tiles):
    x_refs = refs[:n_in]
    mod_ref, g_ref, wg_ref, wu_ref, wd_ref, win_ref, x1_ref, fh_ref, q_ref, k_ref, v_ref = refs[n_in:]
    is_ctx = pl.program_id(0) < ctx_tiles
    k0 = FH_WIDTH + ATTN_WIDTH

    def modulated_input(sl, _):
        x = x_refs[0][sl] if n_in == 1 else jnp.where(is_ctx, x_refs[0][sl], x_refs[1][sl])
        return x, _modulate(x, g_ref[0:1], mod_ref, 0)

    def half_step(sl, xh):
        x1 = _swiglu_rows(*xh, mod_ref, 0, g_ref, wg_ref, wu_ref, wd_ref)
        x1_ref[sl] = x1
        return _modulate(x1, g_ref[2:3], mod_ref, 1)

    def in_projection(sl, h):
        p = _dot(h, win_ref[...])
        fh_ref[sl] = p[:, :FH_WIDTH].astype(fh_ref.dtype)
        q_ref[sl] = p[:, FH_WIDTH:k0].astype(q_ref.dtype)
        k_ref[sl] = p[:, k0:k0 + KV_WIDTH]
        v_ref[sl] = p[:, k0 + KV_WIDTH:]

    _software_pipeline(x1_ref.shape[0], [modulated_input, half_step, in_projection])


def _stage_b_kernel(x_ref, fhc_ref, fhl_ref, atc_ref, atl_ref, mod_ref, g_ref, wout_ref, wg_ref, wu_ref,
                    wd_ref, *o_refs, ctx_tiles):
    is_ctx = pl.program_id(0) < ctx_tiles
    half = FOURIER_WIDTH + HYENA_WIDTH

    def body(mixed, o_ref):
        def out_projection(sl, _):
            y_fh, y_at = mixed(sl)
            y = _dot(y_fh, wout_ref[:half]) + _dot(y_at, wout_ref[half:])
            x2 = x_ref[sl] + mod_ref[0, 1, 2:3] * _rms(y, g_ref[3:4])
            return x2, _modulate(x2, g_ref[4:5], mod_ref, 2)

        def half_step(sl, xh):
            o_ref[sl] = _swiglu_rows(*xh, mod_ref, 2, g_ref, wg_ref, wu_ref, wd_ref)

        _software_pipeline(x_ref.shape[0], [out_projection, half_step])

    if len(o_refs) == 1:
        body(lambda sl: (jnp.where(is_ctx, fhc_ref[sl], fhl_ref[sl]),
                         jnp.where(is_ctx, atc_ref[sl], atl_ref[sl])), o_refs[0])
    else:
        pl.when(is_ctx)(lambda: body(lambda sl: (fhc_ref[sl], atc_ref[sl]), o_refs[0]))
        pl.when(jnp.logical_not(is_ctx))(lambda: body(lambda sl: (fhl_ref[sl], atl_ref[sl]), o_refs[1]))


def _token_specs(tm, ctx_rows, lat_rows, lat_seq):
    ctx_tiles = ctx_rows // tm
    tiles_per_seq = lat_seq // tm
    tile = lambda width: pl.BlockSpec((tm, width), lambda i: (i, 0))
    ctx_tile = lambda width: pl.BlockSpec((tm, width), lambda i: (jnp.minimum(i, ctx_tiles - 1), 0))
    lat_tile = lambda width: pl.BlockSpec((tm, width), lambda i: (jnp.maximum(i - ctx_tiles, 0), 0))
    request = lambda i: jnp.where(i < ctx_tiles, 0, 1 + (i - ctx_tiles) // tiles_per_seq)
    return ctx_tiles, (ctx_rows + lat_rows) // tm, tile, ctx_tile, lat_tile, request


def _stage_a(xs, mod, norm_g, wg, wu, wd, w_in, layer, ctx_rows, lat_rows, lat_seq, tm):
    ctx_tiles, tiles, tile, ctx_tile, lat_tile, request = _token_specs(tm, ctx_rows, lat_rows, lat_seq)
    t = ctx_rows + lat_rows
    widths = (D_MODEL, FH_WIDTH, ATTN_WIDTH, KV_WIDTH, KV_WIDTH)
    dtypes = (F32, BF16, BF16, F32, F32)
    x_specs = [tile(D_MODEL)] if len(xs) == 1 else [ctx_tile(D_MODEL), lat_tile(D_MODEL)]
    return pl.pallas_call(
        functools.partial(_stage_a_kernel, n_in=len(xs), ctx_tiles=ctx_tiles),
        grid=(tiles,),
        in_specs=x_specs + [
            pl.BlockSpec((None, 1, N_SUB, 3, D_MODEL), lambda i: (layer, request(i), 0, 0, 0)),
            _resident((None, 2 * N_SUB, D_MODEL), lambda i: (layer, 0, 0)),
            _resident((None, None, D_MODEL, D_FF), lambda i: (layer, 0, 0, 0)),
            _resident((None, None, D_MODEL, D_FF), lambda i: (layer, 0, 0, 0)),
            _resident((None, None, D_FF, D_MODEL), lambda i: (layer, 0, 0, 0)),
            _resident((None, D_MODEL, IN_WIDTH), lambda i: (layer, 0, 0)),
        ],
        out_specs=[tile(w) for w in widths],
        out_shape=[jax.ShapeDtypeStruct((t, w), dt) for w, dt in zip(widths, dtypes)],
        compiler_params=_cparams("parallel"),
        name="stage_a",
    )(*xs, mod, norm_g, wg, wu, wd, w_in)


def _stage_b(x1, y_fh, y_att, mod, norm_g, w_out, wg, wu, wd, layer, ctx_rows, lat_rows, lat_seq, tm, split):
    ctx_tiles, tiles, tile, ctx_tile, lat_tile, request = _token_specs(tm, ctx_rows, lat_rows, lat_seq)
    half = FOURIER_WIDTH + HYENA_WIDTH
    if split:
        out_specs = [ctx_tile(D_MODEL), lat_tile(D_MODEL)]
        out_shape = [jax.ShapeDtypeStruct((r, D_MODEL), F32) for r in (ctx_rows, lat_rows)]
    else:
        out_specs = [tile(D_MODEL)]
        out_shape = [jax.ShapeDtypeStruct((ctx_rows + lat_rows, D_MODEL), F32)]
    return pl.pallas_call(
        functools.partial(_stage_b_kernel, ctx_tiles=ctx_tiles),
        grid=(tiles,),
        in_specs=[
            tile(D_MODEL), ctx_tile(half), lat_tile(half), ctx_tile(ATTN_WIDTH), lat_tile(ATTN_WIDTH),
            pl.BlockSpec((None, 1, N_SUB, 3, D_MODEL), lambda i: (layer, request(i), 0, 0, 0)),
            _resident((None, 2 * N_SUB, D_MODEL), lambda i: (layer, 0, 0)),
            _resident((None, D_MODEL, D_MODEL), lambda i: (layer, 0, 0)),
            _resident((None, None, D_MODEL, D_FF), lambda i: (layer, 1, 0, 0)),
            _resident((None, None, D_MODEL, D_FF), lambda i: (layer, 1, 0, 0)),
            _resident((None, None, D_FF, D_MODEL), lambda i: (layer, 1, 0, 0)),
        ],
        out_specs=out_specs,
        out_shape=out_shape,
        compiler_params=_cparams("arbitrary" if split else "parallel"),
        name="stage_b",
    )(x1, y_fh[0], y_fh[1], y_att[0], y_att[1], mod, norm_g, w_out, wg, wu, wd)


def _filter_kernel(feats_ref, t_ref, w1_ref, b1_ref, w2_ref, b2_ref, w3_ref, fr_ref, decay_ref,
                   fw_ref, o_ref, *, n):
    fr = fr_ref[...]
    h = jnp.sin(fr * (_dot(feats_ref[...], w1_ref[...]) + b1_ref[...]))
    h = jnp.sin(fr * (_dot(h, w2_ref[...]) + b2_ref[...]))
    h = _dot(h, w3_ref[...])
    window = jnp.exp(-t_ref[...] * jnp.abs(decay_ref[...]))
    width = HYENA_ORDER * HYENA_WIDTH
    row = lax.broadcasted_iota(jnp.int32, (n, width), 0)
    fwd = h[:, :width] * window
    bwd = jnp.where(row == 0, 0.0, h[:, width:] * window)
    even = fwd + bwd
    odd = fwd - bwd
    k_re = _dot(fw_ref[:n], even.astype(BF16))
    k_sn = _dot(fw_ref[n:], odd.astype(BF16))
    sign = (1 - 2 * (row % 2)).astype(F32)
    k_ny = jnp.sum(even * sign, axis=0, keepdims=True)
    s0 = 1.0 / (4.0 * n * n)
    scale = jnp.where(row == 0, s0, 2.0 * s0)
    a = scale * k_re
    o_ref[0] = a
    o_ref[1] = jnp.where(row == 0, 0.0, scale * k_sn)
    o_ref[2] = jnp.where(row == 0, s0 * k_ny, a)


def _filter_spectrum(n, fw, w1, b1, w2, b2, w3, freq, decay):
    feats, t = _hyena_feats(n)
    pad_w = LANES - HYENA_FILTER_WIDTH
    w1p = jnp.pad(w1, ((0, LANES - HYENA_EMB_DIM), (0, pad_w)))
    w2p = jnp.pad(w2, ((0, pad_w), (0, pad_w)))
    w3p = jnp.pad(w3, ((0, pad_w), (0, 0)))
    padv = lambda v: jnp.pad(v, (0, pad_w)).reshape(1, LANES)
    width = HYENA_ORDER * HYENA_WIDTH
    args = (feats, t, w1p, padv(b1), w2p, padv(b2), w3p, padv(freq), decay.reshape(1, width), fw)
    return pl.pallas_call(
        functools.partial(_filter_kernel, n=n),
        out_shape=jax.ShapeDtypeStruct((3, n, width), F32),
        compiler_params=pltpu.CompilerParams(vmem_limit_bytes=VMEM_LIMIT),
        name="hyena_filter",
    )(*args)


def _fh_kernel(p_ref, pos_ref, chan_ref, fw_ref, iw_ref, coef_ref, convw_ref, bias_ref, o_ref,
               st_scr, *, n, group):
    c = HYENA_WIDTH
    cols = [slice(b * c, (b + 1) * c) for b in range(group)]

    for b in range(group):
        t = _dot(p_ref[b, :, :FOURIER_WIDTH], chan_ref[...])
        st_scr[:n, cols[b]] = t[:, :FOURIER_WIDTH].astype(BF16)
        st_scr[n:, cols[b]] = t[:, FOURIER_WIDTH:].astype(BF16)
    y_f = _dot(pos_ref[...], st_scr[...])
    for b in range(group):
        o_ref[b, :, :FOURIER_WIDTH] = y_f[:, cols[b]].astype(o_ref.dtype)

    row = lax.broadcasted_iota(jnp.int32, (n, HYENA_PROJ), 0)
    zs = []
    for b in range(group):
        z = p_ref[b, :, FOURIER_WIDTH:].astype(F32)
        z_prev = jnp.where(row == 0, 0.0, pltpu.roll(z, 1, axis=0))
        z_next = jnp.where(row == n - 1, 0.0, pltpu.roll(z, n - 1, axis=0))
        zs.append(z_prev * convw_ref[0:1] + z * convw_ref[1:2] + z_next * convw_ref[2:3])

    def long_conv(vs, order):
        oc = slice(order * c, (order + 1) * c)
        uf = _dot(fw_ref[...], jnp.concatenate([v.astype(BF16) for v in vs], axis=1))
        a, bb, a2 = coef_ref[0, :, oc], coef_ref[1, :, oc], coef_ref[2, :, oc]
        for b in range(group):
            u_re, u_sn = uf[:n, cols[b]], uf[n:, cols[b]]
            st_scr[:n, cols[b]] = (u_re * a - u_sn * bb).astype(BF16)
            st_scr[n:, cols[b]] = (u_re * bb + u_sn * a2).astype(BF16)
        y = _dot(iw_ref[...], st_scr[...])
        return [y[:, cols[b]] + vs[b] * bias_ref[order:order + 1] for b in range(group)]

    ys = long_conv([z[:, :c] for z in zs], 0)
    ys = long_conv([z[:, c:2 * c] * y for z, y in zip(zs, ys)], 1)
    for b in range(group):
        o_ref[b, :, FOURIER_WIDTH:] = (zs[b][:, 2 * c:] * ys[b]).astype(o_ref.dtype)


def _fh_mix(p_fh, b, first, n, group, pos, chan, fw, iw, coef, conv_w, hbias):
    width = FOURIER_WIDTH + HYENA_WIDTH
    const = lambda shape: _resident(shape, lambda i: (0,) * len(shape))
    return pl.pallas_call(
        functools.partial(_fh_kernel, n=n, group=group),
        grid=(b // group,),
        in_specs=[
            pl.BlockSpec((group, n, FH_WIDTH), lambda i: (i + first // group, 0, 0)),
            const((n, 2 * n)), const((FOURIER_WIDTH, 2 * FOURIER_WIDTH)),
            const((2 * n, n)), const((n, 2 * n)),
            const((3, n, HYENA_ORDER * HYENA_WIDTH)),
            const((3, HYENA_PROJ)), const((HYENA_ORDER, HYENA_WIDTH)),
        ],
        out_specs=pl.BlockSpec((group, n, width), lambda i: (i, 0, 0)),
        out_shape=jax.ShapeDtypeStruct((b, n, width), BF16),
        scratch_shapes=[pltpu.VMEM((2 * n, group * HYENA_WIDTH), BF16)],
        compiler_params=_cparams("parallel"),
        name="fourier_hyena",
    )(p_fh, pos, chan, fw, iw, coef, conv_w, hbias)


def _head_slots(x, kv_head, lo):
    xr = pltpu.roll(x, HEAD_DIM, axis=1)
    if kv_head == 0:
        return jnp.where(lo, x, 0.0), jnp.where(lo, 0.0, xr)
    return jnp.where(lo, xr, 0.0), jnp.where(lo, 0.0, x)


def _slot_ones(rows, slot):
    lo = lax.broadcasted_iota(jnp.int32, (rows, LANES), 1) < HEAD_DIM
    return jnp.where(lo if slot == 0 else jnp.logical_not(lo), 1.0, 0.0).astype(BF16)


def _ctx_attn_kernel(sink_ref, q_ref, k_ref, v_ref, o_ref, kt_ref, vt_ref, *, n, group):
    lo = lax.broadcasted_iota(jnp.int32, (n, LANES), 1) < HEAD_DIM
    top = lax.broadcasted_iota(jnp.int32, (2 * n, 1), 0) < n
    lo_g = lax.broadcasted_iota(jnp.int32, (2 * n, LANES), 1) < HEAD_DIM
    ones = jnp.concatenate([_slot_ones(n, 0), _slot_ones(n, 1)], axis=0)
    for b in range(group):
        kt_ref[b] = k_ref[b].T
        vt_ref[b] = v_ref[b].T
        k = k_ref[b] * LOG2E
        v = v_ref[b]
        for kv_head in range(N_KV_HEADS):
            h0 = 4 * kv_head
            k_cat = jnp.concatenate(_head_slots(k, kv_head, lo), axis=0).astype(BF16)
            v_cat = jnp.concatenate(_head_slots(v, kv_head, lo), axis=0).astype(BF16)
            v_cat = jnp.concatenate([v_cat, ones], axis=1)
            q2 = jnp.concatenate([q_ref[b, :, (2 * kv_head + t) * LANES:(2 * kv_head + t + 1) * LANES]
                                  for t in range(2)], axis=0) * HEAD_DIM ** -0.5
            s = _dot_nt(q2, k_cat)
            e, sink_term = [], []
            for slot in range(2):
                sink = jnp.where(top, sink_ref[h0 + slot], sink_ref[h0 + 2 + slot]) * LOG2E
                cols = s[:, slot * n:(slot + 1) * n]
                m = jnp.maximum(jnp.max(cols, axis=-1, keepdims=True), sink)
                e.append(jnp.exp2(cols - m).astype(BF16))
                sink_term.append(jnp.exp2(sink - m))
            o = _dot(jnp.concatenate(e, axis=1), v_cat)
            o = o[:, :LANES] / (o[:, LANES:] + jnp.where(lo_g, sink_term[0], sink_term[1]))
            for t in range(2):
                tile = 2 * kv_head + t
                o_ref[b, :, tile * LANES:(tile + 1) * LANES] = o[t * n:(t + 1) * n].astype(o_ref.dtype)


def _ctx_attention(q, k, v, b, n, sink, group=4):
    return pl.pallas_call(
        functools.partial(_ctx_attn_kernel, n=n, group=group),
        grid=(b // group,),
        in_specs=[
            pl.BlockSpec(memory_space=pltpu.SMEM),
            pl.BlockSpec((group, n, ATTN_WIDTH), lambda i: (i, 0, 0)),
            pl.BlockSpec((group, n, KV_WIDTH), lambda i: (i, 0, 0)),
            pl.BlockSpec((group, n, KV_WIDTH), lambda i: (i, 0, 0)),
        ],
        out_specs=[pl.BlockSpec((group, n, ATTN_WIDTH), lambda i: (i, 0, 0)),
                   pl.BlockSpec((group, KV_WIDTH, n), lambda i: (i, 0, 0)),
                   pl.BlockSpec((group, KV_WIDTH, n), lambda i: (i, 0, 0))],
        out_shape=[jax.ShapeDtypeStruct((b, n, ATTN_WIDTH), BF16),
                   jax.ShapeDtypeStruct((b, KV_WIDTH, n), F32),
                   jax.ShapeDtypeStruct((b, KV_WIDTH, n), F32)],
        compiler_params=_cparams("parallel"),
        name="ctx_attention",
    )(sink, q, k, v)


def _rope(x, cos, sin_signed):
    lane = lax.broadcasted_iota(jnp.int32, x.shape, 1)
    first = (lane % (HEAD_DIM // 2)) < HEAD_DIM // 4
    partner = jnp.where(first, pltpu.roll(x, LANES - HEAD_DIM // 4, axis=1),
                        pltpu.roll(x, HEAD_DIM // 4, axis=1))
    return x * cos + partner * sin_signed


def _lat_attn_kernel(sink_ref, q_ref, k_ref, v_ref, ck_ref, cv_ref, cos_ref, sin_ref, o_ref,
                     q_scr, k_scr, v_scr, ck_scr, cv_scr, *, n, c_len):
    nb = n // BLOCK
    cb = c_len // BLOCK
    grp = 2 * BLOCK
    cos, sin = cos_ref[...], sin_ref[...]
    lo = lax.broadcasted_iota(jnp.int32, (n, LANES), 1) < HEAD_DIM
    lo_c = lax.broadcasted_iota(jnp.int32, (c_len, LANES), 1) < HEAD_DIM

    k = _rope(k_ref[0], cos, sin) * LOG2E
    v = v_ref[0]
    ck = ck_ref[0, 0] * LOG2E
    for kv_head in range(N_KV_HEADS):
        for t in range(2):
            tile = 2 * kv_head + t
            q = q_ref[0, :, tile * LANES:(tile + 1) * LANES].astype(F32)
            q = (_rope(q, cos, sin) * HEAD_DIM ** -0.5).astype(BF16)
            for i in range(nb):
                q_scr[kv_head, i * grp + t * BLOCK:i * grp + (t + 1) * BLOCK] = q[i * BLOCK:(i + 1) * BLOCK]
        for slot, (ks, vs, cks, cvs) in enumerate(zip(
                _head_slots(k, kv_head, lo), _head_slots(v, kv_head, lo),
                _head_slots(ck, kv_head, lo_c), _head_slots(cv_ref[0, 0], kv_head, lo_c))):
            ks, vs, cks, cvs = (a.astype(BF16) for a in (ks, vs, cks, cvs))
            ones = _slot_ones(BLOCK, slot)
            for j in range(nb):
                rows = slice(j * grp + slot * BLOCK, j * grp + (slot + 1) * BLOCK)
                k_scr[kv_head, rows] = ks[j * BLOCK:(j + 1) * BLOCK]
                v_scr[kv_head, rows, :LANES] = vs[j * BLOCK:(j + 1) * BLOCK]
                v_scr[kv_head, rows, LANES:] = ones
            for j in range(cb):
                rows = slice(j * grp + slot * BLOCK, j * grp + (slot + 1) * BLOCK)
                ck_scr[kv_head, rows] = cks[j * BLOCK:(j + 1) * BLOCK]
                cv_scr[kv_head, rows, :LANES] = cvs[j * BLOCK:(j + 1) * BLOCK]
                cv_scr[kv_head, rows, LANES:] = ones

    qi = lax.broadcasted_iota(jnp.int32, (grp, BLOCK), 0) % BLOCK
    kj = lax.broadcasted_iota(jnp.int32, (grp, BLOCK), 1)
    keep_prev = kj >= qi
    keep_next = kj <= qi
    top = lax.broadcasted_iota(jnp.int32, (grp, 1), 0) < BLOCK
    lo_g = lax.broadcasted_iota(jnp.int32, (grp, LANES), 1) < HEAD_DIM

    def lane_tiles(s):
        return [s[:, c * BLOCK:(c + 1) * BLOCK] for c in range(s.shape[1] // BLOCK)]

    for kv_head in range(N_KV_HEADS):
        h0 = 4 * kv_head
        sinks = [jnp.where(top, sink_ref[h0 + slot], sink_ref[h0 + 2 + slot]) * LOG2E for slot in range(2)]
        for i in range(nb):
            j0, j1 = max(i - 1, 0), min(i + 2, nb)
            q2 = q_scr[kv_head, i * grp:(i + 1) * grp]
            tiles = lane_tiles(_dot_nt(q2, k_scr[kv_head, j0 * grp:j1 * grp]))
            for b, j in enumerate(range(j0, j1)):
                for slot in range(2):
                    if j == i - 1:
                        tiles[2 * b + slot] = jnp.where(keep_prev, tiles[2 * b + slot], NEG_BIG)
                    elif j == i + 1:
                        tiles[2 * b + slot] = jnp.where(keep_next, tiles[2 * b + slot], NEG_BIG)
            tiles += lane_tiles(_dot_nt(q2, ck_scr[kv_head]))
            sink_term = []
            for slot in range(2):
                mine = tiles[slot::2]
                m = jnp.maximum(jnp.max(functools.reduce(jnp.maximum, mine), axis=-1, keepdims=True),
                                sinks[slot])
                tiles[slot::2] = [jnp.exp2(tl - m).astype(BF16) for tl in mine]
                sink_term.append(jnp.exp2(sinks[slot] - m))
            n_loc = 2 * (j1 - j0)
            e_loc = jnp.concatenate(tiles[:n_loc], axis=1)
            e_ctx = jnp.concatenate(tiles[n_loc:], axis=1)
            o = _dot(e_loc, v_scr[kv_head, j0 * grp:j1 * grp]) + _dot(e_ctx, cv_scr[kv_head])
            o = o[:, :LANES] / (o[:, LANES:] + jnp.where(lo_g, sink_term[0], sink_term[1]))
            o = o.astype(o_ref.dtype)
            for t in range(2):
                tile = 2 * kv_head + t
                o_ref[0, i * BLOCK:(i + 1) * BLOCK, tile * LANES:(tile + 1) * LANES] = o[t * BLOCK:(t + 1) * BLOCK]


def _lat_attention(q, k, v, b, first, n, sink, cache_k, cache_v, layer, cos, sin):
    c_len = cache_k.shape[2]
    return pl.pallas_call(
        functools.partial(_lat_attn_kernel, n=n, c_len=c_len),
        grid=(b,),
        in_specs=[
            pl.BlockSpec(memory_space=pltpu.SMEM),
            pl.BlockSpec((1, n, ATTN_WIDTH), lambda i: (i + first, 0, 0)),
            pl.BlockSpec((1, n, KV_WIDTH), lambda i: (i + first, 0, 0)),
            pl.BlockSpec((1, n, KV_WIDTH), lambda i: (i + first, 0, 0)),
            pl.BlockSpec((1, 1, c_len, KV_WIDTH), lambda i: (i, layer, 0, 0)),
            pl.BlockSpec((1, 1, c_len, KV_WIDTH), lambda i: (i, layer, 0, 0)),
            _resident((n, LANES), lambda i: (0, 0)),
            _resident((n, LANES), lambda i: (0, 0)),
        ],
        out_specs=pl.BlockSpec((1, n, ATTN_WIDTH), lambda i: (i, 0, 0)),
        out_shape=jax.ShapeDtypeStruct((b, n, ATTN_WIDTH), BF16),
        scratch_shapes=[
            pltpu.VMEM((N_KV_HEADS, 2 * n, LANES), BF16),
            pltpu.VMEM((N_KV_HEADS, 2 * n, LANES), BF16),
            pltpu.VMEM((N_KV_HEADS, 2 * n, 2 * LANES), BF16),
            pltpu.VMEM((N_KV_HEADS, 2 * c_len, LANES), BF16),
            pltpu.VMEM((N_KV_HEADS, 2 * c_len, 2 * LANES), BF16),
        ],
        compiler_params=_cparams("parallel"),
        name="latent_attention",
    )(sink, q, k, v, cache_k, cache_v, cos, sin)


def kernel(x_prompt, x_sample, cache_k, cache_v, c, c_ctx, w_mod, b_mod, norm_g, ffn_w_gate, ffn_w_up,
           ffn_w_down, w_in, w_out, hyena_conv_w, hyena_f_w1, hyena_f_b1, hyena_f_w2, hyena_f_b2,
           hyena_f_w3, hyena_f_freq, hyena_decay, hyena_bias, attn_sink):
    batch, seq, d = x_prompt.shape
    dec_batch, dec_seq, _ = x_sample.shape
    past_len = cache_k.shape[2]

    cvecs = jnp.concatenate([c_ctx[None], c], axis=0)
    cvecs = jnp.pad(cvecs, ((0, MOD_ROWS - cvecs.shape[0]), (0, 0)))
    mod = _modulation(cvecs, w_mod, b_mod).reshape(DEPTH, MOD_ROWS, N_SUB, 3, d)

    wg = ffn_w_gate.astype(BF16)
    wu = ffn_w_up.astype(BF16)
    wd = ffn_w_down.astype(BF16)
    w_in_b = w_in.astype(BF16)
    w_out_b = w_out.astype(BF16)
    ck = cache_k.reshape(dec_batch, DEPTH, past_len, KV_WIDTH)
    cv = cache_v.reshape(dec_batch, DEPTH, past_len, KV_WIDTH)
    rope_cos, rope_sin = (jnp.asarray(t) for t in _rope_tables(dec_seq))

    tables = {}
    for n in (seq, dec_seq):
        fw, iw = (jnp.asarray(t).astype(BF16) for t in _hyena_dft(n))
        pos, chan = (jnp.asarray(t).astype(BF16) for t in _fourier_tables(n))
        tables[n] = (fw, iw, pos, chan)

    ctx_rows, lat_rows = batch * seq, dec_batch * dec_seq
    rows = ctx_rows + lat_rows
    sizes = (ctx_rows, lat_rows, dec_seq)
    xs = (x_prompt.reshape(ctx_rows, d), x_sample.reshape(lat_rows, d))
    new_k, new_v = [], []
    for l in range(DEPTH):
        first, last = l == 0, l == DEPTH - 1
        x1, p_fh, q, k, v = _stage_a(xs, mod, norm_g, wg, wu, wd, w_in_b, l, *sizes,
                                     tm=EDGE_TILE if first else TOKEN_TILE)
        y_fh, y_att = [], []
        for n, nseq, first_seq, latent in ((seq, batch, 0, False), (dec_seq, dec_batch, ctx_rows // dec_seq, True)):
            fw, iw, pos, chan = tables[n]
            coef = _filter_spectrum(n, fw, hyena_f_w1[l], hyena_f_b1[l], hyena_f_w2[l], hyena_f_b2[l],
                                    hyena_f_w3[l], hyena_f_freq[l], hyena_decay[l])
            y = _fh_mix(p_fh.reshape(rows // n, n, FH_WIDTH), nseq, first_seq, n, 2 if latent else 8,
                        pos, chan, fw, iw, coef, hyena_conv_w[l], hyena_bias[l])
            y_fh.append(y.reshape(nseq * n, -1))
            qs, ks, vs = (a.reshape(rows // n, n, -1) for a in (q, k, v))
            if latent:
                y = _lat_attention(qs, ks, vs, nseq, first_seq, n, attn_sink[l], ck, cv, l, rope_cos, rope_sin)
            else:
                y, k_t, v_t = _ctx_attention(qs, ks, vs, nseq, n, attn_sink[l])
                new_k.append(k_t)
                new_v.append(v_t)
            y_att.append(y.reshape(nseq * n, -1))
        xs = _stage_b(x1, y_fh, y_att, mod, norm_g, w_out_b, wg, wu, wd, l, *sizes,
                      tm=EDGE_TILE if last else TOKEN_TILE, split=last)

    def cache_layout(per_layer):
        t = jnp.stack(per_layer, axis=0).reshape(DEPTH, batch, N_KV_HEADS, HEAD_DIM, seq)
        return jnp.transpose(t, (1, 0, 4, 2, 3))

    return (xs[0].reshape(batch, seq, d), xs[1].reshape(dec_batch, dec_seq, d),
            cache_layout(new_k), cache_layout(new_v))
```

```python
import functools
import math

import numpy as np
import jax
import jax.numpy as jnp
from jax import lax
from jax.experimental import pallas as pl
from jax.experimental.pallas import tpu as pltpu

F32 = jnp.float32
BF16 = jnp.bfloat16

D_MODEL = 1024
DEPTH = 2
GRID_W = 64
HEAD_DIM = 64
N_Q_HEADS = 8
N_KV_HEADS = 2
ATTN_WIDTH = N_Q_HEADS * HEAD_DIM
KV_WIDTH = N_KV_HEADS * HEAD_DIM
FOURIER_WIDTH = 256
FOURIER_GROUPS = 4
FOURIER_GROUP_DIM = 64
HYENA_WIDTH = 256
HYENA_ORDER = 2
HYENA_PROJ = 3 * HYENA_WIDTH
HYENA_EMB_DIM = 33
HYENA_BANDS = 16
HYENA_FILTER_WIDTH = 64
FH_WIDTH = FOURIER_WIDTH + HYENA_PROJ
ATT_WIDTH = ATTN_WIDTH + 2 * KV_WIDTH
IN_WIDTH = FH_WIDTH + ATT_WIDTH
BLOCK = 128
WINDOW = 128
ROPE_BASE = 10000.0
D_FF = 2816
N_SUB = 3
RMS_EPS = 1e-6

LANES = 128
MOD_ROWS = 16
VMEM_LIMIT = 56 * 1024 * 1024
NEG_BIG = -1e30
LOG2E = 1.4426950408889634


def _cparams(*sem):
    return pltpu.CompilerParams(dimension_semantics=sem, vmem_limit_bytes=VMEM_LIMIT)


def _rms(x, g):
    return x * lax.rsqrt(jnp.mean(x * x, axis=-1, keepdims=True) + RMS_EPS) * g


def _dot(a, b):
    return jnp.dot(a, b, preferred_element_type=F32)


def _dot_nt(a, b):
    return lax.dot_general(a, b, (((1,), (1,)), ((), ())), preferred_element_type=F32)


@functools.lru_cache(maxsize=None)
def _hyena_dft(n):
    f = np.arange(n, dtype=np.int64)[:, None]
    s = np.arange(n, dtype=np.int64)[None, :]
    ang = np.pi * ((f * s) % (2 * n)).astype(np.float64) / n
    c = np.cos(ang)
    sn = np.sin(ang)
    sn[0, :] = 1.0 - 2.0 * (np.arange(n) % 2)
    fw = np.concatenate([c, sn], axis=0).astype(np.float32)
    iw = np.concatenate([c, sn.T], axis=1).astype(np.float32)
    return fw, iw


@functools.lru_cache(maxsize=None)
def _fourier_tables(n):
    f = np.arange(n, dtype=np.int64)[:, None]
    s = np.arange(n, dtype=np.int64)[None, :]
    ang = 2.0 * np.pi * ((f * s) % n).astype(np.float64) / n
    pos = np.concatenate([np.cos(ang), -np.sin(ang)], axis=1) / math.sqrt(n)
    gd = FOURIER_GROUP_DIM
    a = np.arange(gd, dtype=np.int64)
    ang_c = 2.0 * np.pi * ((a[:, None] * a[None, :]) % gd).astype(np.float64) / gd
    bc = np.kron(np.eye(FOURIER_GROUPS), np.cos(ang_c)) / math.sqrt(gd)
    bs = np.kron(np.eye(FOURIER_GROUPS), np.sin(ang_c)) / math.sqrt(gd)
    chan = np.concatenate([bc, bs], axis=1)
    return pos.astype(np.float32), chan.astype(np.float32)


@functools.lru_cache(maxsize=None)
def _rope_tables(n):
    half = HEAD_DIM // 2
    inv = ROPE_BASE ** (-np.arange(0, half, 2, dtype=np.float64) / half)
    t = np.arange(n)
    row = (t // GRID_W).astype(np.float64)
    col = (t % GRID_W).astype(np.float64)
    lane = np.arange(LANES)
    d = lane % HEAD_DIM
    pos = np.where((d // half)[None, :] == 0, row[:, None], col[:, None])
    ang = pos * inv[d % (half // 2)][None, :]
    sign = np.where((d % half) < half // 2, -1.0, 1.0)[None, :]
    return np.cos(ang).astype(np.float32), (np.sin(ang) * sign).astype(np.float32)


def _hyena_feats(n):
    d = jnp.arange(n, dtype=F32)
    t = jnp.linspace(0.0, 1.0, n, dtype=F32)[:, None]
    f = jnp.linspace(1e-4, HYENA_BANDS - 1, HYENA_BANDS, dtype=F32)
    ang = (2.0 * math.pi / n) * d[:, None] * f[None, :]
    feats = jnp.concatenate([t, jnp.cos(ang), -jnp.sin(ang)], axis=-1)
    return jnp.pad(feats, ((0, 0), (0, LANES - HYENA_EMB_DIM))), t


def _mod_kernel(c_ref, w_ref, b_ref, o_ref):
    c = c_ref[...]
    s = c / (1.0 + jnp.exp(-c))
    o_ref[0] = _dot(s.astype(BF16), w_ref[0].astype(BF16)) + b_ref[0]


def _modulation(cvecs, w_mod, b_mod):
    depth, _, width = w_mod.shape
    tn = 1536
    return pl.pallas_call(
        _mod_kernel,
        grid=(depth, width // tn),
        in_specs=[
            pl.BlockSpec((MOD_ROWS, D_MODEL), lambda l, j: (0, 0)),
            pl.BlockSpec((1, D_MODEL, tn), lambda l, j: (l, 0, j)),
            pl.BlockSpec((1, 1, tn), lambda l, j: (l, 0, j)),
        ],
        out_specs=pl.BlockSpec((1, MOD_ROWS, tn), lambda l, j: (l, 0, j)),
        out_shape=jax.ShapeDtypeStruct((depth, MOD_ROWS, width), F32),
        compiler_params=_cparams("arbitrary", "arbitrary"),
        name="modulation",
    )(cvecs, w_mod, b_mod.reshape(depth, 1, width))


def _resident(block_shape, index_map):
    return pl.BlockSpec(block_shape, index_map, pipeline_mode=pl.Buffered(1))


ROW_GROUP = 256
TOKEN_TILE = 1024
EDGE_TILE = 512


def _modulate(x, g_row, mod_ref, sub):
    return (_rms(x, g_row) * (1.0 + mod_ref[0, sub, 1:2]) + mod_ref[0, sub, 0:1]).astype(BF16)


def _swiglu_rows(x, h, mod_ref, sub, g_ref, wg_ref, wu_ref, wd_ref):
    gate = _dot(h, wg_ref[...])
    up = _dot(h, wu_ref[...])
    act = (gate / (1.0 + jnp.exp(-gate))) * up
    y = _dot(act.astype(BF16), wd_ref[...])
    return x + (0.5 * mod_ref[0, sub, 2:3]) * _rms(y, g_ref[2 * sub + 1:2 * sub + 2])


def _software_pipeline(rows, phases):
    groups = [slice(r, r + ROW_GROUP) for r in range(0, rows, ROW_GROUP)]
    state = {}
    for turn in range(len(groups) + len(phases) - 1):
        for p, phase in enumerate(phases):
            g = turn - p
            if 0 <= g < len(groups):
                state[g] = phase(groups[g], state.get(g))


UP_CHUNK = 32
DOWN_CHUNK = 64


def _stage_weight(src_hbm, dst_ref, stage_ref, sem, rows):
    n_chunks = dst_ref.shape[0] // rows

    def chunk_copy(c):
        return pltpu.make_async_copy(src_hbm.at[pl.ds(c * rows, rows)], stage_ref.at[c % 2], sem.at[c % 2])

    chunk_copy(0).start()
    for c in range(n_chunks):
        if c + 1 < n_chunks:
            chunk_copy(c + 1).start()
        chunk_copy(c).wait()
        dst_ref[c * rows:(c + 1) * rows] = stage_ref[c % 2].astype(BF16)


def _stage_ffn_weights(hbm_refs, layer, which, vmem_refs, up_stage, down_stage, sem):
    @pl.when(pl.program_id(0) == 0)
    def _():
        wg_hbm, wu_hbm, wd_hbm = hbm_refs
        wg_ref, wu_ref, wd_ref = vmem_refs
        _stage_weight(wg_hbm.at[layer, which], wg_ref, up_stage, sem, UP_CHUNK)
        _stage_weight(wu_hbm.at[layer, which], wu_ref, up_stage, sem, UP_CHUNK)
        _stage_weight(wd_hbm.at[layer, which], wd_ref, down_stage, sem, DOWN_CHUNK)


_FFN_WEIGHT_SPECS = [pl.BlockSpec(memory_space=pl.ANY)] * 3
_FFN_WEIGHT_SCRATCH = [
    pltpu.VMEM((D_MODEL, D_FF), BF16), pltpu.VMEM((D_MODEL, D_FF), BF16), pltpu.VMEM((D_FF, D_MODEL), BF16),
    pltpu.VMEM((2, UP_CHUNK, D_FF), F32), pltpu.VMEM((2, DOWN_CHUNK, D_MODEL), F32),
    pltpu.SemaphoreType.DMA((2,)),
]


def _stage_a_kernel(*refs, n_in, ctx_tiles, layer):
    x_refs = refs[:n_in]
    (mod_ref, g_ref, wg_hbm, wu_hbm, wd_hbm, win_ref, x1_ref, fh_ref, q_ref, k_ref, v_ref,
     wg_ref, wu_ref, wd_ref, up_stage, down_stage, sem) = refs[n_in:]
    _stage_ffn_weights((wg_hbm, wu_hbm, wd_hbm), layer, 0, (wg_ref, wu_ref, wd_ref), up_stage, down_stage, sem)
    is_ctx = pl.program_id(0) < ctx_tiles
    k0 = FH_WIDTH + ATTN_WIDTH

    def modulated_input(sl, _):
        x = x_refs[0][sl] if n_in == 1 else jnp.where(is_ctx, x_refs[0][sl], x_refs[1][sl])
        return x, _modulate(x, g_ref[0:1], mod_ref, 0)

    def half_step(sl, xh):
        x1 = _swiglu_rows(*xh, mod_ref, 0, g_ref, wg_ref, wu_ref, wd_ref)
        x1_ref[sl] = x1
        return _modulate(x1, g_ref[2:3], mod_ref, 1)

    def in_projection(sl, h):
        p = _dot(h, win_ref[...])
        fh_ref[sl] = p[:, :FH_WIDTH].astype(fh_ref.dtype)
        q_ref[sl] = p[:, FH_WIDTH:k0].astype(q_ref.dtype)
        k_ref[sl] = p[:, k0:k0 + KV_WIDTH]
        v_ref[sl] = p[:, k0 + KV_WIDTH:]

    _software_pipeline(x1_ref.shape[0], [modulated_input, half_step, in_projection])


def _stage_b_kernel(x_ref, fhc_ref, fhl_ref, atc_ref, atl_ref, mod_ref, g_ref, wout_ref, wg_hbm, wu_hbm,
                    wd_hbm, *refs, ctx_tiles, layer):
    *o_refs, wg_ref, wu_ref, wd_ref, up_stage, down_stage, sem = refs
    _stage_ffn_weights((wg_hbm, wu_hbm, wd_hbm), layer, 1, (wg_ref, wu_ref, wd_ref), up_stage, down_stage, sem)
    is_ctx = pl.program_id(0) < ctx_tiles
    half = FOURIER_WIDTH + HYENA_WIDTH

    def body(mixed, o_ref):
        def out_projection(sl, _):
            y_fh, y_at = mixed(sl)
            y = _dot(y_fh, wout_ref[:half]) + _dot(y_at, wout_ref[half:])
            x2 = x_ref[sl] + mod_ref[0, 1, 2:3] * _rms(y, g_ref[3:4])
            return x2, _modulate(x2, g_ref[4:5], mod_ref, 2)

        def half_step(sl, xh):
            o_ref[sl] = _swiglu_rows(*xh, mod_ref, 2, g_ref, wg_ref, wu_ref, wd_ref)

        _software_pipeline(x_ref.shape[0], [out_projection, half_step])

    if len(o_refs) == 1:
        body(lambda sl: (jnp.where(is_ctx, fhc_ref[sl], fhl_ref[sl]),
                         jnp.where(is_ctx, atc_ref[sl], atl_ref[sl])), o_refs[0])
    else:
        pl.when(is_ctx)(lambda: body(lambda sl: (fhc_ref[sl], atc_ref[sl]), o_refs[0]))
        pl.when(jnp.logical_not(is_ctx))(lambda: body(lambda sl: (fhl_ref[sl], atl_ref[sl]), o_refs[1]))


def _token_specs(tm, ctx_rows, lat_rows, lat_seq):
    ctx_tiles = ctx_rows // tm
    tiles_per_seq = lat_seq // tm
    tile = lambda width: pl.BlockSpec((tm, width), lambda i: (i, 0))
    ctx_tile = lambda width: pl.BlockSpec((tm, width), lambda i: (jnp.minimum(i, ctx_tiles - 1), 0))
    lat_tile = lambda width: pl.BlockSpec((tm, width), lambda i: (jnp.maximum(i - ctx_tiles, 0), 0))
    request = lambda i: jnp.where(i < ctx_tiles, 0, 1 + (i - ctx_tiles) // tiles_per_seq)
    return ctx_tiles, (ctx_rows + lat_rows) // tm, tile, ctx_tile, lat_tile, request


def _stage_a(xs, mod, norm_g, wg, wu, wd, w_in, layer, ctx_rows, lat_rows, lat_seq, tm):
    ctx_tiles, tiles, tile, ctx_tile, lat_tile, request = _token_specs(tm, ctx_rows, lat_rows, lat_seq)
    t = ctx_rows + lat_rows
    widths = (D_MODEL, FH_WIDTH, ATTN_WIDTH, KV_WIDTH, KV_WIDTH)
    dtypes = (F32, BF16, BF16, F32, F32)
    x_specs = [tile(D_MODEL)] if len(xs) == 1 else [ctx_tile(D_MODEL), lat_tile(D_MODEL)]
    return pl.pallas_call(
        functools.partial(_stage_a_kernel, n_in=len(xs), ctx_tiles=ctx_tiles, layer=layer),
        grid=(tiles,),
        in_specs=x_specs + [
            pl.BlockSpec((None, 1, N_SUB, 3, D_MODEL), lambda i: (layer, request(i), 0, 0, 0)),
            _resident((None, 2 * N_SUB, D_MODEL), lambda i: (layer, 0, 0)),
            *_FFN_WEIGHT_SPECS,
            _resident((None, D_MODEL, IN_WIDTH), lambda i: (layer, 0, 0)),
        ],
        out_specs=[tile(w) for w in widths],
        out_shape=[jax.ShapeDtypeStruct((t, w), dt) for w, dt in zip(widths, dtypes)],
        scratch_shapes=_FFN_WEIGHT_SCRATCH,
        compiler_params=_cparams("arbitrary"),
        name="stage_a",
    )(*xs, mod, norm_g, wg, wu, wd, w_in)


def _stage_b(x1, y_fh, y_att, mod, norm_g, w_out, wg, wu, wd, layer, ctx_rows, lat_rows, lat_seq, tm, split):
    ctx_tiles, tiles, tile, ctx_tile, lat_tile, request = _token_specs(tm, ctx_rows, lat_rows, lat_seq)
    half = FOURIER_WIDTH + HYENA_WIDTH
    if split:
        out_specs = [ctx_tile(D_MODEL), lat_tile(D_MODEL)]
        out_shape = [jax.ShapeDtypeStruct((r, D_MODEL), F32) for r in (ctx_rows, lat_rows)]
    else:
        out_specs = [tile(D_MODEL)]
        out_shape = [jax.ShapeDtypeStruct((ctx_rows + lat_rows, D_MODEL), F32)]
    return pl.pallas_call(
        functools.partial(_stage_b_kernel, ctx_tiles=ctx_tiles, layer=layer),
        grid=(tiles,),
        in_specs=[
            tile(D_MODEL), ctx_tile(half), lat_tile(half), ctx_tile(ATTN_WIDTH), lat_tile(ATTN_WIDTH),
            pl.BlockSpec((None, 1, N_SUB, 3, D_MODEL), lambda i: (layer, request(i), 0, 0, 0)),
            _resident((None, 2 * N_SUB, D_MODEL), lambda i: (layer, 0, 0)),
            _resident((None, D_MODEL, D_MODEL), lambda i: (layer, 0, 0)),
            *_FFN_WEIGHT_SPECS,
        ],
        out_specs=out_specs,
        out_shape=out_shape,
        scratch_shapes=_FFN_WEIGHT_SCRATCH,
        compiler_params=_cparams("arbitrary"),
        name="stage_b",
    )(x1, y_fh[0], y_fh[1], y_att[0], y_att[1], mod, norm_g, w_out, wg, wu, wd)


def _filter_kernel(feats_ref, t_ref, w1_ref, b1_ref, w2_ref, b2_ref, w3_ref, fr_ref, decay_ref,
                   fw_ref, o_ref, *, n):
    fr = fr_ref[...]
    h = jnp.sin(fr * (_dot(feats_ref[...], w1_ref[...]) + b1_ref[...]))
    h = jnp.sin(fr * (_dot(h, w2_ref[...]) + b2_ref[...]))
    h = _dot(h, w3_ref[...])
    window = jnp.exp(-t_ref[...] * jnp.abs(decay_ref[...]))
    width = HYENA_ORDER * HYENA_WIDTH
    row = lax.broadcasted_iota(jnp.int32, (n, width), 0)
    fwd = h[:, :width] * window
    bwd = jnp.where(row == 0, 0.0, h[:, width:] * window)
    even = fwd + bwd
    odd = fwd - bwd
    k_re = _dot(fw_ref[:n], even.astype(BF16))
    k_sn = _dot(fw_ref[n:], odd.astype(BF16))
    sign = (1 - 2 * (row % 2)).astype(F32)
    k_ny = jnp.sum(even * sign, axis=0, keepdims=True)
    s0 = 1.0 / (4.0 * n * n)
    scale = jnp.where(row == 0, s0, 2.0 * s0)
    a = scale * k_re
    o_ref[0] = a
    o_ref[1] = jnp.where(row == 0, 0.0, scale * k_sn)
    o_ref[2] = jnp.where(row == 0, s0 * k_ny, a)


def _filter_spectrum(n, fw, w1, b1, w2, b2, w3, freq, decay):
    feats, t = _hyena_feats(n)
    pad_w = LANES - HYENA_FILTER_WIDTH
    w1p = jnp.pad(w1, ((0, LANES - HYENA_EMB_DIM), (0, pad_w)))
    w2p = jnp.pad(w2, ((0, pad_w), (0, pad_w)))
    w3p = jnp.pad(w3, ((0, pad_w), (0, 0)))
    padv = lambda v: jnp.pad(v, (0, pad_w)).reshape(1, LANES)
    width = HYENA_ORDER * HYENA_WIDTH
    args = (feats, t, w1p, padv(b1), w2p, padv(b2), w3p, padv(freq), decay.reshape(1, width), fw)
    return pl.pallas_call(
        functools.partial(_filter_kernel, n=n),
        out_shape=jax.ShapeDtypeStruct((3, n, width), F32),
        compiler_params=pltpu.CompilerParams(vmem_limit_bytes=VMEM_LIMIT),
        name="hyena_filter",
    )(*args)


def _fh_kernel(p_ref, pos_ref, chan_ref, fw_ref, iw_ref, coef_ref, convw_ref, bias_ref, o_ref,
               st_scr, *, n, group):
    c = HYENA_WIDTH
    cols = [slice(b * c, (b + 1) * c) for b in range(group)]

    for b in range(group):
        t = _dot(p_ref[b, :, :FOURIER_WIDTH], chan_ref[...])
        st_scr[:n, cols[b]] = t[:, :FOURIER_WIDTH].astype(BF16)
        st_scr[n:, cols[b]] = t[:, FOURIER_WIDTH:].astype(BF16)
    y_f = _dot(pos_ref[...], st_scr[...])
    for b in range(group):
        o_ref[b, :, :FOURIER_WIDTH] = y_f[:, cols[b]].astype(o_ref.dtype)

    row = lax.broadcasted_iota(jnp.int32, (n, HYENA_PROJ), 0)
    zs = []
    for b in range(group):
        z = p_ref[b, :, FOURIER_WIDTH:].astype(F32)
        z_prev = jnp.where(row == 0, 0.0, pltpu.roll(z, 1, axis=0))
        z_next = jnp.where(row == n - 1, 0.0, pltpu.roll(z, n - 1, axis=0))
        zs.append(z_prev * convw_ref[0:1] + z * convw_ref[1:2] + z_next * convw_ref[2:3])

    def long_conv(vs, order):
        oc = slice(order * c, (order + 1) * c)
        uf = _dot(fw_ref[...], jnp.concatenate([v.astype(BF16) for v in vs], axis=1))
        a, bb, a2 = coef_ref[0, :, oc], coef_ref[1, :, oc], coef_ref[2, :, oc]
        for b in range(group):
            u_re, u_sn = uf[:n, cols[b]], uf[n:, cols[b]]
            st_scr[:n, cols[b]] = (u_re * a - u_sn * bb).astype(BF16)
            st_scr[n:, cols[b]] = (u_re * bb + u_sn * a2).astype(BF16)
        y = _dot(iw_ref[...], st_scr[...])
        return [y[:, cols[b]] + vs[b] * bias_ref[order:order + 1] for b in range(group)]

    ys = long_conv([z[:, :c] for z in zs], 0)
    ys = long_conv([z[:, c:2 * c] * y for z, y in zip(zs, ys)], 1)
    for b in range(group):
        o_ref[b, :, FOURIER_WIDTH:] = (zs[b][:, 2 * c:] * ys[b]).astype(o_ref.dtype)


def _fh_mix(p_fh, b, first, n, group, pos, chan, fw, iw, coef, conv_w, hbias):
    width = FOURIER_WIDTH + HYENA_WIDTH
    const = lambda shape: _resident(shape, lambda i: (0,) * len(shape))
    return pl.pallas_call(
        functools.partial(_fh_kernel, n=n, group=group),
        grid=(b // group,),
        in_specs=[
            pl.BlockSpec((group, n, FH_WIDTH), lambda i: (i + first // group, 0, 0)),
            const((n, 2 * n)), const((FOURIER_WIDTH, 2 * FOURIER_WIDTH)),
            const((2 * n, n)), const((n, 2 * n)),
            const((3, n, HYENA_ORDER * HYENA_WIDTH)),
            const((3, HYENA_PROJ)), const((HYENA_ORDER, HYENA_WIDTH)),
        ],
        out_specs=pl.BlockSpec((group, n, width), lambda i: (i, 0, 0)),
        out_shape=jax.ShapeDtypeStruct((b, n, width), BF16),
        scratch_shapes=[pltpu.VMEM((2 * n, group * HYENA_WIDTH), BF16)],
        compiler_params=_cparams("parallel"),
        name="fourier_hyena",
    )(p_fh, pos, chan, fw, iw, coef, conv_w, hbias)


def _head_slots(x, kv_head, lo):
    xr = pltpu.roll(x, HEAD_DIM, axis=1)
    if kv_head == 0:
        return jnp.where(lo, x, 0.0), jnp.where(lo, 0.0, xr)
    return jnp.where(lo, xr, 0.0), jnp.where(lo, 0.0, x)


def _slot_ones(rows, slot):
    lo = lax.broadcasted_iota(jnp.int32, (rows, LANES), 1) < HEAD_DIM
    return jnp.where(lo if slot == 0 else jnp.logical_not(lo), 1.0, 0.0).astype(BF16)


def _ctx_attn_kernel(sink_ref, q_ref, k_ref, v_ref, o_ref, kt_ref, vt_ref, *, n, group):
    lo = lax.broadcasted_iota(jnp.int32, (n, LANES), 1) < HEAD_DIM
    top = lax.broadcasted_iota(jnp.int32, (2 * n, 1), 0) < n
    lo_g = lax.broadcasted_iota(jnp.int32, (2 * n, LANES), 1) < HEAD_DIM
    ones = jnp.concatenate([_slot_ones(n, 0), _slot_ones(n, 1)], axis=0)
    for b in range(group):
        kt_ref[b] = k_ref[b].T
        vt_ref[b] = v_ref[b].T
        k = k_ref[b] * LOG2E
        v = v_ref[b]
        for kv_head in range(N_KV_HEADS):
            h0 = 4 * kv_head
            k_cat = jnp.concatenate(_head_slots(k, kv_head, lo), axis=0).astype(BF16)
            v_cat = jnp.concatenate(_head_slots(v, kv_head, lo), axis=0).astype(BF16)
            v_cat = jnp.concatenate([v_cat, ones], axis=1)
            q2 = jnp.concatenate([q_ref[b, :, (2 * kv_head + t) * LANES:(2 * kv_head + t + 1) * LANES]
                                  for t in range(2)], axis=0) * HEAD_DIM ** -0.5
            s = _dot_nt(q2, k_cat)
            e, sink_term = [], []
            for slot in range(2):
                sink = jnp.where(top, sink_ref[h0 + slot], sink_ref[h0 + 2 + slot]) * LOG2E
                cols = s[:, slot * n:(slot + 1) * n]
                m = jnp.maximum(jnp.max(cols, axis=-1, keepdims=True), sink)
                e.append(jnp.exp2(cols - m).astype(BF16))
                sink_term.append(jnp.exp2(sink - m))
            o = _dot(jnp.concatenate(e, axis=1), v_cat)
            o = o[:, :LANES] / (o[:, LANES:] + jnp.where(lo_g, sink_term[0], sink_term[1]))
            for t in range(2):
                tile = 2 * kv_head + t
                o_ref[b, :, tile * LANES:(tile + 1) * LANES] = o[t * n:(t + 1) * n].astype(o_ref.dtype)


def _ctx_attention(q, k, v, b, n, sink, group=4):
    return pl.pallas_call(
        functools.partial(_ctx_attn_kernel, n=n, group=group),
        grid=(b // group,),
        in_specs=[
            pl.BlockSpec(memory_space=pltpu.SMEM),
            pl.BlockSpec((group, n, ATTN_WIDTH), lambda i: (i, 0, 0)),
            pl.BlockSpec((group, n, KV_WIDTH), lambda i: (i, 0, 0)),
            pl.BlockSpec((group, n, KV_WIDTH), lambda i: (i, 0, 0)),
        ],
        out_specs=[pl.BlockSpec((group, n, ATTN_WIDTH), lambda i: (i, 0, 0)),
                   pl.BlockSpec((group, KV_WIDTH, n), lambda i: (i, 0, 0)),
                   pl.BlockSpec((group, KV_WIDTH, n), lambda i: (i, 0, 0))],
        out_shape=[jax.ShapeDtypeStruct((b, n, ATTN_WIDTH), BF16),
                   jax.ShapeDtypeStruct((b, KV_WIDTH, n), F32),
                   jax.ShapeDtypeStruct((b, KV_WIDTH, n), F32)],
        compiler_params=_cparams("parallel"),
        name="ctx_attention",
    )(sink, q, k, v)


def _rope(x, cos, sin_signed):
    lane = lax.broadcasted_iota(jnp.int32, x.shape, 1)
    first = (lane % (HEAD_DIM // 2)) < HEAD_DIM // 4
    partner = jnp.where(first, pltpu.roll(x, LANES - HEAD_DIM // 4, axis=1),
                        pltpu.roll(x, HEAD_DIM // 4, axis=1))
    return x * cos + partner * sin_signed


def _lat_attn_kernel(sink_ref, q_ref, k_ref, v_ref, ck_ref, cv_ref, cos_ref, sin_ref, o_ref,
                     q_scr, k_scr, v_scr, ck_scr, cv_scr, *, n, c_len):
    nb = n // BLOCK
    cb = c_len // BLOCK
    grp = 2 * BLOCK
    cos, sin = cos_ref[...], sin_ref[...]
    lo = lax.broadcasted_iota(jnp.int32, (n, LANES), 1) < HEAD_DIM
    lo_c = lax.broadcasted_iota(jnp.int32, (c_len, LANES), 1) < HEAD_DIM

    k = _rope(k_ref[0], cos, sin) * LOG2E
    v = v_ref[0]
    ck = ck_ref[0, 0] * LOG2E
    for kv_head in range(N_KV_HEADS):
        for t in range(2):
            tile = 2 * kv_head + t
            q = q_ref[0, :, tile * LANES:(tile + 1) * LANES].astype(F32)
            q = (_rope(q, cos, sin) * HEAD_DIM ** -0.5).astype(BF16)
            for i in range(nb):
                q_scr[kv_head, i * grp + t * BLOCK:i * grp + (t + 1) * BLOCK] = q[i * BLOCK:(i + 1) * BLOCK]
        for slot, (ks, vs, cks, cvs) in enumerate(zip(
                _head_slots(k, kv_head, lo), _head_slots(v, kv_head, lo),
                _head_slots(ck, kv_head, lo_c), _head_slots(cv_ref[0, 0], kv_head, lo_c))):
            ks, vs, cks, cvs = (a.astype(BF16) for a in (ks, vs, cks, cvs))
            ones = _slot_ones(BLOCK, slot)
            for j in range(nb):
                rows = slice(j * grp + slot * BLOCK, j * grp + (slot + 1) * BLOCK)
                k_scr[kv_head, rows] = ks[j * BLOCK:(j + 1) * BLOCK]
                v_scr[kv_head, rows, :LANES] = vs[j * BLOCK:(j + 1) * BLOCK]
                v_scr[kv_head, rows, LANES:] = ones
            for j in range(cb):
                rows = slice(j * grp + slot * BLOCK, j * grp + (slot + 1) * BLOCK)
                ck_scr[kv_head, rows] = cks[j * BLOCK:(j + 1) * BLOCK]
                cv_scr[kv_head, rows, :LANES] = cvs[j * BLOCK:(j + 1) * BLOCK]
                cv_scr[kv_head, rows, LANES:] = ones

    qi = lax.broadcasted_iota(jnp.int32, (grp, BLOCK), 0) % BLOCK
    kj = lax.broadcasted_iota(jnp.int32, (grp, BLOCK), 1)
    keep_prev = kj >= qi
    keep_next = kj <= qi
    top = lax.broadcasted_iota(jnp.int32, (grp, 1), 0) < BLOCK
    lo_g = lax.broadcasted_iota(jnp.int32, (grp, LANES), 1) < HEAD_DIM

    def lane_tiles(s):
        return [s[:, c * BLOCK:(c + 1) * BLOCK] for c in range(s.shape[1] // BLOCK)]

    for kv_head in range(N_KV_HEADS):
        h0 = 4 * kv_head
        sinks = [jnp.where(top, sink_ref[h0 + slot], sink_ref[h0 + 2 + slot]) * LOG2E for slot in range(2)]
        for i in range(nb):
            j0, j1 = max(i - 1, 0), min(i + 2, nb)
            q2 = q_scr[kv_head, i * grp:(i + 1) * grp]
            tiles = lane_tiles(_dot_nt(q2, k_scr[kv_head, j0 * grp:j1 * grp]))
            for b, j in enumerate(range(j0, j1)):
                for slot in range(2):
                    if j == i - 1:
                        tiles[2 * b + slot] = jnp.where(keep_prev, tiles[2 * b + slot], NEG_BIG)
                    elif j == i + 1:
                        tiles[2 * b + slot] = jnp.where(keep_next, tiles[2 * b + slot], NEG_BIG)
            tiles += lane_tiles(_dot_nt(q2, ck_scr[kv_head]))
            sink_term = []
            for slot in range(2):
                mine = tiles[slot::2]
                m = jnp.maximum(jnp.max(functools.reduce(jnp.maximum, mine), axis=-1, keepdims=True),
                                sinks[slot])
                tiles[slot::2] = [jnp.exp2(tl - m).astype(BF16) for tl in mine]
                sink_term.append(jnp.exp2(sinks[slot] - m))
            n_loc = 2 * (j1 - j0)
            e_loc = jnp.concatenate(tiles[:n_loc], axis=1)
            e_ctx = jnp.concatenate(tiles[n_loc:], axis=1)
            o = _dot(e_loc, v_scr[kv_head, j0 * grp:j1 * grp]) + _dot(e_ctx, cv_scr[kv_head])
            o = o[:, :LANES] / (o[:, LANES:] + jnp.where(lo_g, sink_term[0], sink_term[1]))
            o = o.astype(o_ref.dtype)
            for t in range(2):
                tile = 2 * kv_head + t
                o_ref[0, i * BLOCK:(i + 1) * BLOCK, tile * LANES:(tile + 1) * LANES] = o[t * BLOCK:(t + 1) * BLOCK]


def _lat_attention(q, k, v, b, first, n, sink, cache_k, cache_v, layer, cos, sin):
    c_len = cache_k.shape[2]
    return pl.pallas_call(
        functools.partial(_lat_attn_kernel, n=n, c_len=c_len),
        grid=(b,),
        in_specs=[
            pl.BlockSpec(memory_space=pltpu.SMEM),
            pl.BlockSpec((1, n, ATTN_WIDTH), lambda i: (i + first, 0, 0)),
            pl.BlockSpec((1, n, KV_WIDTH), lambda i: (i + first, 0, 0)),
            pl.BlockSpec((1, n, KV_WIDTH), lambda i: (i + first, 0, 0)),
            pl.BlockSpec((1, 1, c_len, KV_WIDTH), lambda i: (i, layer, 0, 0)),
            pl.BlockSpec((1, 1, c_len, KV_WIDTH), lambda i: (i, layer, 0, 0)),
            _resident((n, LANES), lambda i: (0, 0)),
            _resident((n, LANES), lambda i: (0, 0)),
        ],
        out_specs=pl.BlockSpec((1, n, ATTN_WIDTH), lambda i: (i, 0, 0)),
        out_shape=jax.ShapeDtypeStruct((b, n, ATTN_WIDTH), BF16),
        scratch_shapes=[
            pltpu.VMEM((N_KV_HEADS, 2 * n, LANES), BF16),
            pltpu.VMEM((N_KV_HEADS, 2 * n, LANES), BF16),
            pltpu.VMEM((N_KV_HEADS, 2 * n, 2 * LANES), BF16),
            pltpu.VMEM((N_KV_HEADS, 2 * c_len, LANES), BF16),
            pltpu.VMEM((N_KV_HEADS, 2 * c_len, 2 * LANES), BF16),
        ],
        compiler_params=_cparams("parallel"),
        name="latent_attention",
    )(sink, q, k, v, cache_k, cache_v, cos, sin)


def kernel(x_prompt, x_sample, cache_k, cache_v, c, c_ctx, w_mod, b_mod, norm_g, ffn_w_gate, ffn_w_up,
           ffn_w_down, w_in, w_out, hyena_conv_w, hyena_f_w1, hyena_f_b1, hyena_f_w2, hyena_f_b2,
           hyena_f_w3, hyena_f_freq, hyena_decay, hyena_bias, attn_sink):
    batch, seq, d = x_prompt.shape
    dec_batch, dec_seq, _ = x_sample.shape
    past_len = cache_k.shape[2]

    cvecs = jnp.concatenate([c_ctx[None], c], axis=0)
    cvecs = jnp.pad(cvecs, ((0, MOD_ROWS - cvecs.shape[0]), (0, 0)))
    mod = _modulation(cvecs, w_mod, b_mod).reshape(DEPTH, MOD_ROWS, N_SUB, 3, d)

    wg, wu, wd = ffn_w_gate, ffn_w_up, ffn_w_down
    w_in_b = w_in.astype(BF16)
    w_out_b = w_out.astype(BF16)
    ck = cache_k.reshape(dec_batch, DEPTH, past_len, KV_WIDTH)
    cv = cache_v.reshape(dec_batch, DEPTH, past_len, KV_WIDTH)
    rope_cos, rope_sin = (jnp.asarray(t) for t in _rope_tables(dec_seq))

    tables = {}
    for n in (seq, dec_seq):
        fw, iw = (jnp.asarray(t).astype(BF16) for t in _hyena_dft(n))
        pos, chan = (jnp.asarray(t).astype(BF16) for t in _fourier_tables(n))
        tables[n] = (fw, iw, pos, chan)

    ctx_rows, lat_rows = batch * seq, dec_batch * dec_seq
    rows = ctx_rows + lat_rows
    sizes = (ctx_rows, lat_rows, dec_seq)
    xs = (x_prompt.reshape(ctx_rows, d), x_sample.reshape(lat_rows, d))
    new_k, new_v = [], []
    for l in range(DEPTH):
        first, last = l == 0, l == DEPTH - 1
        x1, p_fh, q, k, v = _stage_a(xs, mod, norm_g, wg, wu, wd, w_in_b, l, *sizes,
                                     tm=EDGE_TILE if first else TOKEN_TILE)
        y_fh, y_att = [], []
        for n, nseq, first_seq, latent in ((seq, batch, 0, False), (dec_seq, dec_batch, ctx_rows // dec_seq, True)):
            fw, iw, pos, chan = tables[n]
            coef = _filter_spectrum(n, fw, hyena_f_w1[l], hyena_f_b1[l], hyena_f_w2[l], hyena_f_b2[l],
                                    hyena_f_w3[l], hyena_f_freq[l], hyena_decay[l])
            y = _fh_mix(p_fh.reshape(rows // n, n, FH_WIDTH), nseq, first_seq, n, 2 if latent else 8,
                        pos, chan, fw, iw, coef, hyena_conv_w[l], hyena_bias[l])
            y_fh.append(y.reshape(nseq * n, -1))
            qs, ks, vs = (a.reshape(rows // n, n, -1) for a in (q, k, v))
            if latent:
                y = _lat_attention(qs, ks, vs, nseq, first_seq, n, attn_sink[l], ck, cv, l, rope_cos, rope_sin)
            else:
                y, k_t, v_t = _ctx_attention(qs, ks, vs, nseq, n, attn_sink[l])
                new_k.append(k_t)
                new_v.append(v_t)
            y_att.append(y.reshape(nseq * n, -1))
        xs = _stage_b(x1, y_fh, y_att, mod, norm_g, w_out_b, wg, wu, wd, l, *sizes,
                      tm=EDGE_TILE if last else TOKEN_TILE, split=last)

    def cache_layout(per_layer):
        t = jnp.stack(per_layer, axis=0).reshape(DEPTH, batch, N_KV_HEADS, HEAD_DIM, seq)
        return jnp.transpose(t, (1, 0, 4, 2, 3))

    return (xs[0].reshape(batch, seq, d), xs[1].reshape(dec_batch, dec_seq, d),
            cache_layout(new_k), cache_layout(new_v))
```

```python
import functools
import math

import numpy as np
import jax
import jax.numpy as jnp
from jax import lax
from jax.experimental import pallas as pl
from jax.experimental.pallas import tpu as pltpu

F32 = jnp.float32
BF16 = jnp.bfloat16

D_MODEL = 1024
DEPTH = 2
GRID_W = 64
HEAD_DIM = 64
N_Q_HEADS = 8
N_KV_HEADS = 2
ATTN_WIDTH = N_Q_HEADS * HEAD_DIM
KV_WIDTH = N_KV_HEADS * HEAD_DIM
FOURIER_WIDTH = 256
FOURIER_GROUPS = 4
FOURIER_GROUP_DIM = 64
HYENA_WIDTH = 256
HYENA_ORDER = 2
HYENA_PROJ = 3 * HYENA_WIDTH
HYENA_EMB_DIM = 33
HYENA_BANDS = 16
HYENA_FILTER_WIDTH = 64
FH_WIDTH = FOURIER_WIDTH + HYENA_PROJ
ATT_WIDTH = ATTN_WIDTH + 2 * KV_WIDTH
IN_WIDTH = FH_WIDTH + ATT_WIDTH
BLOCK = 128
WINDOW = 128
ROPE_BASE = 10000.0
D_FF = 2816
N_SUB = 3
RMS_EPS = 1e-6

LANES = 128
MOD_ROWS = 16
VMEM_LIMIT = 56 * 1024 * 1024
NEG_BIG = -1e30
LOG2E = 1.4426950408889634

TOKEN_TILE = 1024
EDGE_TILE = 512
ROW_GROUP = 256
MOD_TILE = 3072
FH_GROUP = {False: 8, True: 2}
CTX_ATTN_GROUP = 4


def _cparams(*sem):
    return pltpu.CompilerParams(dimension_semantics=sem, vmem_limit_bytes=VMEM_LIMIT)


def _rms(x, g):
    return x * lax.rsqrt(jnp.mean(x * x, axis=-1, keepdims=True) + RMS_EPS) * g


def _dot(a, b):
    return jnp.dot(a, b, preferred_element_type=F32)


def _dot_nt(a, b):
    return lax.dot_general(a, b, (((1,), (1,)), ((), ())), preferred_element_type=F32)


@functools.lru_cache(maxsize=None)
def _hyena_dft(n):
    f = np.arange(n, dtype=np.int64)[:, None]
    s = np.arange(n, dtype=np.int64)[None, :]
    ang = np.pi * ((f * s) % (2 * n)).astype(np.float64) / n
    c = np.cos(ang)
    sn = np.sin(ang)
    sn[0, :] = 1.0 - 2.0 * (np.arange(n) % 2)
    fw = np.concatenate([c, sn], axis=0).astype(np.float32)
    iw = np.concatenate([c, sn.T], axis=1).astype(np.float32)
    return fw, iw


@functools.lru_cache(maxsize=None)
def _fourier_tables(n):
    f = np.arange(n, dtype=np.int64)[:, None]
    s = np.arange(n, dtype=np.int64)[None, :]
    ang = 2.0 * np.pi * ((f * s) % n).astype(np.float64) / n
    pos = np.concatenate([np.cos(ang), -np.sin(ang)], axis=1) / math.sqrt(n)
    gd = FOURIER_GROUP_DIM
    a = np.arange(gd, dtype=np.int64)
    ang_c = 2.0 * np.pi * ((a[:, None] * a[None, :]) % gd).astype(np.float64) / gd
    bc = np.kron(np.eye(FOURIER_GROUPS), np.cos(ang_c)) / math.sqrt(gd)
    bs = np.kron(np.eye(FOURIER_GROUPS), np.sin(ang_c)) / math.sqrt(gd)
    chan = np.concatenate([bc, bs], axis=1)
    return pos.astype(np.float32), chan.astype(np.float32)


@functools.lru_cache(maxsize=None)
def _rope_tables(n):
    half = HEAD_DIM // 2
    inv = ROPE_BASE ** (-np.arange(0, half, 2, dtype=np.float64) / half)
    t = np.arange(n)
    row = (t // GRID_W).astype(np.float64)
    col = (t % GRID_W).astype(np.float64)
    lane = np.arange(LANES)
    d = lane % HEAD_DIM
    pos = np.where((d // half)[None, :] == 0, row[:, None], col[:, None])
    ang = pos * inv[d % (half // 2)][None, :]
    sign = np.where((d % half) < half // 2, -1.0, 1.0)[None, :]
    return np.cos(ang).astype(np.float32), (np.sin(ang) * sign).astype(np.float32)


def _hyena_feats(n):
    d = jnp.arange(n, dtype=F32)
    t = jnp.linspace(0.0, 1.0, n, dtype=F32)[:, None]
    f = jnp.linspace(1e-4, HYENA_BANDS - 1, HYENA_BANDS, dtype=F32)
    ang = (2.0 * math.pi / n) * d[:, None] * f[None, :]
    feats = jnp.concatenate([t, jnp.cos(ang), -jnp.sin(ang)], axis=-1)
    return jnp.pad(feats, ((0, 0), (0, LANES - HYENA_EMB_DIM))), t


def _mod_kernel(c_ref, w_ref, b_ref, o_ref):
    c = c_ref[...]
    s = c / (1.0 + jnp.exp(-c))
    o_ref[0] = _dot(s.astype(BF16), w_ref[0].astype(BF16)) + b_ref[0]


def _modulation(cvecs, w_mod, b_mod):
    depth, _, width = w_mod.shape
    tn = MOD_TILE
    return pl.pallas_call(
        _mod_kernel,
        grid=(depth, width // tn),
        in_specs=[
            pl.BlockSpec((MOD_ROWS, D_MODEL), lambda l, j: (0, 0)),
            pl.BlockSpec((1, D_MODEL, tn), lambda l, j: (l, 0, j)),
            pl.BlockSpec((1, 1, tn), lambda l, j: (l, 0, j)),
        ],
        out_specs=pl.BlockSpec((1, MOD_ROWS, tn), lambda l, j: (l, 0, j)),
        out_shape=jax.ShapeDtypeStruct((depth, MOD_ROWS, width), F32),
        compiler_params=_cparams("arbitrary", "arbitrary"),
        name="modulation",
    )(cvecs, w_mod, b_mod.reshape(depth, 1, width))


def _resident(block_shape, index_map):
    return pl.BlockSpec(block_shape, index_map, pipeline_mode=pl.Buffered(1))


def _modulate(x, g_row, mod_ref, sub):
    return (_rms(x, g_row) * (1.0 + mod_ref[0, sub, 1:2]) + mod_ref[0, sub, 0:1]).astype(BF16)


def _swiglu_rows(x, h, mod_ref, sub, g_ref, wg_ref, wu_ref, wd_ref):
    gate = _dot(h, wg_ref[...])
    up = _dot(h, wu_ref[...])
    act = (gate / (1.0 + jnp.exp(-gate))) * up
    y = _dot(act.astype(BF16), wd_ref[...])
    return x + (0.5 * mod_ref[0, sub, 2:3]) * _rms(y, g_ref[2 * sub + 1:2 * sub + 2])


def _software_pipeline(rows, phases):
    groups = [slice(r, r + ROW_GROUP) for r in range(0, rows, ROW_GROUP)]
    state = {}
    for turn in range(len(groups) + len(phases) - 1):
        for p, phase in enumerate(phases):
            g = turn - p
            if 0 <= g < len(groups):
                state[g] = phase(groups[g], state.get(g))


def _stage_a_kernel(*refs, n_in, ctx_tiles):
    x_refs = refs[:n_in]
    mod_ref, g_ref, wg_ref, wu_ref, wd_ref, win_ref, x1_ref, fh_ref, q_ref, k_ref, v_ref = refs[n_in:]
    is_ctx = pl.program_id(0) < ctx_tiles
    k0 = FH_WIDTH + ATTN_WIDTH

    def modulated_input(sl, _):
        x = x_refs[0][sl] if n_in == 1 else jnp.where(is_ctx, x_refs[0][sl], x_refs[1][sl])
        return x, _modulate(x, g_ref[0:1], mod_ref, 0)

    def half_step(sl, xh):
        x1 = _swiglu_rows(*xh, mod_ref, 0, g_ref, wg_ref, wu_ref, wd_ref)
        x1_ref[sl] = x1
        return _modulate(x1, g_ref[2:3], mod_ref, 1)

    def in_projection(sl, h):
        p = _dot(h, win_ref[...])
        fh_ref[sl] = p[:, :FH_WIDTH].astype(fh_ref.dtype)
        q_ref[sl] = p[:, FH_WIDTH:k0].astype(q_ref.dtype)
        k_ref[sl] = p[:, k0:k0 + KV_WIDTH]
        v_ref[sl] = p[:, k0 + KV_WIDTH:]

    _software_pipeline(x1_ref.shape[0], [modulated_input, half_step, in_projection])


def _stage_b_kernel(x_ref, fhc_ref, fhl_ref, atc_ref, atl_ref, mod_ref, g_ref, wout_ref, wg_ref, wu_ref,
                    wd_ref, *o_refs, ctx_tiles):
    is_ctx = pl.program_id(0) < ctx_tiles
    half = FOURIER_WIDTH + HYENA_WIDTH

    def body(mixed, o_ref):
        def out_projection(sl, _):
            y_fh, y_at = mixed(sl)
            y = _dot(y_fh, wout_ref[:half]) + _dot(y_at, wout_ref[half:])
            x2 = x_ref[sl] + mod_ref[0, 1, 2:3] * _rms(y, g_ref[3:4])
            return x2, _modulate(x2, g_ref[4:5], mod_ref, 2)

        def half_step(sl, xh):
            o_ref[sl] = _swiglu_rows(*xh, mod_ref, 2, g_ref, wg_ref, wu_ref, wd_ref)

        _software_pipeline(x_ref.shape[0], [out_projection, half_step])

    if len(o_refs) == 1:
        body(lambda sl: (jnp.where(is_ctx, fhc_ref[sl], fhl_ref[sl]),
                         jnp.where(is_ctx, atc_ref[sl], atl_ref[sl])), o_refs[0])
    else:
        pl.when(is_ctx)(lambda: body(lambda sl: (fhc_ref[sl], atc_ref[sl]), o_refs[0]))
        pl.when(jnp.logical_not(is_ctx))(lambda: body(lambda sl: (fhl_ref[sl], atl_ref[sl]), o_refs[1]))


def _token_specs(tm, ctx_rows, lat_rows, lat_seq):
    ctx_tiles = ctx_rows // tm
    tiles_per_seq = lat_seq // tm
    tile = lambda width: pl.BlockSpec((tm, width), lambda i: (i, 0))
    ctx_tile = lambda width: pl.BlockSpec((tm, width), lambda i: (jnp.minimum(i, ctx_tiles - 1), 0))
    lat_tile = lambda width: pl.BlockSpec((tm, width), lambda i: (jnp.maximum(i - ctx_tiles, 0), 0))
    request = lambda i: jnp.where(i < ctx_tiles, 0, 1 + (i - ctx_tiles) // tiles_per_seq)
    return ctx_tiles, (ctx_rows + lat_rows) // tm, tile, ctx_tile, lat_tile, request


def _stage_a(xs, mod, norm_g, wg, wu, wd, w_in, layer, ctx_rows, lat_rows, lat_seq, tm):
    ctx_tiles, tiles, tile, ctx_tile, lat_tile, request = _token_specs(tm, ctx_rows, lat_rows, lat_seq)
    t = ctx_rows + lat_rows
    widths = (D_MODEL, FH_WIDTH, ATTN_WIDTH, KV_WIDTH, KV_WIDTH)
    dtypes = (F32, BF16, BF16, F32, F32)
    x_specs = [tile(D_MODEL)] if len(xs) == 1 else [ctx_tile(D_MODEL), lat_tile(D_MODEL)]
    return pl.pallas_call(
        functools.partial(_stage_a_kernel, n_in=len(xs), ctx_tiles=ctx_tiles),
        grid=(tiles,),
        in_specs=x_specs + [
            pl.BlockSpec((None, 1, N_SUB, 3, D_MODEL), lambda i: (layer, request(i), 0, 0, 0)),
            _resident((None, 2 * N_SUB, D_MODEL), lambda i: (layer, 0, 0)),
            _resident((None, None, D_MODEL, D_FF), lambda i: (layer, 0, 0, 0)),
            _resident((None, None, D_MODEL, D_FF), lambda i: (layer, 0, 0, 0)),
            _resident((None, None, D_FF, D_MODEL), lambda i: (layer, 0, 0, 0)),
            _resident((None, D_MODEL, IN_WIDTH), lambda i: (layer, 0, 0)),
        ],
        out_specs=[tile(w) for w in widths],
        out_shape=[jax.ShapeDtypeStruct((t, w), dt) for w, dt in zip(widths, dtypes)],
        compiler_params=_cparams("parallel"),
        name="stage_a",
    )(*xs, mod, norm_g, wg, wu, wd, w_in)


def _stage_b(x1, y_fh, y_att, mod, norm_g, w_out, wg, wu, wd, layer, ctx_rows, lat_rows, lat_seq, tm, split):
    ctx_tiles, tiles, tile, ctx_tile, lat_tile, request = _token_specs(tm, ctx_rows, lat_rows, lat_seq)
    half = FOURIER_WIDTH + HYENA_WIDTH
    if split:
        out_specs = [ctx_tile(D_MODEL), lat_tile(D_MODEL)]
        out_shape = [jax.ShapeDtypeStruct((r, D_MODEL), F32) for r in (ctx_rows, lat_rows)]
    else:
        out_specs = [tile(D_MODEL)]
        out_shape = [jax.ShapeDtypeStruct((ctx_rows + lat_rows, D_MODEL), F32)]
    return pl.pallas_call(
        functools.partial(_stage_b_kernel, ctx_tiles=ctx_tiles),
        grid=(tiles,),
        in_specs=[
            tile(D_MODEL), ctx_tile(half), lat_tile(half), ctx_tile(ATTN_WIDTH), lat_tile(ATTN_WIDTH),
            pl.BlockSpec((None, 1, N_SUB, 3, D_MODEL), lambda i: (layer, request(i), 0, 0, 0)),
            _resident((None, 2 * N_SUB, D_MODEL), lambda i: (layer, 0, 0)),
            _resident((None, D_MODEL, D_MODEL), lambda i: (layer, 0, 0)),
            _resident((None, None, D_MODEL, D_FF), lambda i: (layer, 1, 0, 0)),
            _resident((None, None, D_MODEL, D_FF), lambda i: (layer, 1, 0, 0)),
            _resident((None, None, D_FF, D_MODEL), lambda i: (layer, 1, 0, 0)),
        ],
        out_specs=out_specs,
        out_shape=out_shape,
        compiler_params=_cparams("arbitrary" if split else "parallel"),
        name="stage_b",
    )(x1, y_fh[0], y_fh[1], y_att[0], y_att[1], mod, norm_g, w_out, wg, wu, wd)


def _filter_kernel(feats_ref, t_ref, w1_ref, b1_ref, w2_ref, b2_ref, w3_ref, fr_ref, decay_ref,
                   fw_ref, o_ref, *, n):
    fr = fr_ref[...]
    h = jnp.sin(fr * (_dot(feats_ref[...], w1_ref[...]) + b1_ref[...]))
    h = jnp.sin(fr * (_dot(h, w2_ref[...]) + b2_ref[...]))
    h = _dot(h, w3_ref[...])
    window = jnp.exp(-t_ref[...] * jnp.abs(decay_ref[...]))
    width = HYENA_ORDER * HYENA_WIDTH
    row = lax.broadcasted_iota(jnp.int32, (n, width), 0)
    fwd = h[:, :width] * window
    bwd = jnp.where(row == 0, 0.0, h[:, width:] * window)
    even = fwd + bwd
    odd = fwd - bwd
    k_re = _dot(fw_ref[:n], even.astype(BF16))
    k_sn = _dot(fw_ref[n:], odd.astype(BF16))
    sign = (1 - 2 * (row % 2)).astype(F32)
    k_ny = jnp.sum(even * sign, axis=0, keepdims=True)
    s0 = 1.0 / (4.0 * n * n)
    scale = jnp.where(row == 0, s0, 2.0 * s0)
    a = scale * k_re
    o_ref[0] = a
    o_ref[1] = jnp.where(row == 0, 0.0, scale * k_sn)
    o_ref[2] = jnp.where(row == 0, s0 * k_ny, a)


def _filter_spectrum(n, fw, w1, b1, w2, b2, w3, freq, decay):
    feats, t = _hyena_feats(n)
    pad_w = LANES - HYENA_FILTER_WIDTH
    w1p = jnp.pad(w1, ((0, LANES - HYENA_EMB_DIM), (0, pad_w)))
    w2p = jnp.pad(w2, ((0, pad_w), (0, pad_w)))
    w3p = jnp.pad(w3, ((0, pad_w), (0, 0)))
    padv = lambda v: jnp.pad(v, (0, pad_w)).reshape(1, LANES)
    width = HYENA_ORDER * HYENA_WIDTH
    args = (feats, t, w1p, padv(b1), w2p, padv(b2), w3p, padv(freq), decay.reshape(1, width), fw)
    return pl.pallas_call(
        functools.partial(_filter_kernel, n=n),
        out_shape=jax.ShapeDtypeStruct((3, n, width), F32),
        compiler_params=pltpu.CompilerParams(vmem_limit_bytes=VMEM_LIMIT),
        name="hyena_filter",
    )(*args)


def _fh_kernel(p_ref, pos_ref, chan_ref, fw_ref, iw_ref, coef_ref, convw_ref, bias_ref, o_ref,
               st_scr, *, n, group):
    c = HYENA_WIDTH
    cols = [slice(b * c, (b + 1) * c) for b in range(group)]

    for b in range(group):
        t = _dot(p_ref[b, :, :FOURIER_WIDTH], chan_ref[...])
        st_scr[:n, cols[b]] = t[:, :FOURIER_WIDTH].astype(BF16)
        st_scr[n:, cols[b]] = t[:, FOURIER_WIDTH:].astype(BF16)
    y_f = _dot(pos_ref[...], st_scr[...])
    for b in range(group):
        o_ref[b, :, :FOURIER_WIDTH] = y_f[:, cols[b]].astype(o_ref.dtype)

    row = lax.broadcasted_iota(jnp.int32, (n, HYENA_PROJ), 0)
    zs = []
    for b in range(group):
        z = p_ref[b, :, FOURIER_WIDTH:].astype(F32)
        z_prev = jnp.where(row == 0, 0.0, pltpu.roll(z, 1, axis=0))
        z_next = jnp.where(row == n - 1, 0.0, pltpu.roll(z, n - 1, axis=0))
        zs.append(z_prev * convw_ref[0:1] + z * convw_ref[1:2] + z_next * convw_ref[2:3])

    def long_conv(vs, order):
        oc = slice(order * c, (order + 1) * c)
        uf = _dot(fw_ref[...], jnp.concatenate([v.astype(BF16) for v in vs], axis=1))
        a, bb, a2 = coef_ref[0, :, oc], coef_ref[1, :, oc], coef_ref[2, :, oc]
        for b in range(group):
            u_re, u_sn = uf[:n, cols[b]], uf[n:, cols[b]]
            st_scr[:n, cols[b]] = (u_re * a - u_sn * bb).astype(BF16)
            st_scr[n:, cols[b]] = (u_re * bb + u_sn * a2).astype(BF16)
        y = _dot(iw_ref[...], st_scr[...])
        return [y[:, cols[b]] + vs[b] * bias_ref[order:order + 1] for b in range(group)]

    ys = long_conv([z[:, :c] for z in zs], 0)
    ys = long_conv([z[:, c:2 * c] * y for z, y in zip(zs, ys)], 1)
    for b in range(group):
        o_ref[b, :, FOURIER_WIDTH:] = (zs[b][:, 2 * c:] * ys[b]).astype(o_ref.dtype)


def _fh_mix(p_fh, b, first, n, group, pos, chan, fw, iw, coef, conv_w, hbias):
    width = FOURIER_WIDTH + HYENA_WIDTH
    const = lambda shape: _resident(shape, lambda i: (0,) * len(shape))
    return pl.pallas_call(
        functools.partial(_fh_kernel, n=n, group=group),
        grid=(b // group,),
        in_specs=[
            pl.BlockSpec((group, n, FH_WIDTH), lambda i: (i + first // group, 0, 0)),
            const((n, 2 * n)), const((FOURIER_WIDTH, 2 * FOURIER_WIDTH)),
            const((2 * n, n)), const((n, 2 * n)),
            const((3, n, HYENA_ORDER * HYENA_WIDTH)),
            const((3, HYENA_PROJ)), const((HYENA_ORDER, HYENA_WIDTH)),
        ],
        out_specs=pl.BlockSpec((group, n, width), lambda i: (i, 0, 0)),
        out_shape=jax.ShapeDtypeStruct((b, n, width), BF16),
        scratch_shapes=[pltpu.VMEM((2 * n, group * HYENA_WIDTH), BF16)],
        compiler_params=_cparams("parallel"),
        name="fourier_hyena",
    )(p_fh, pos, chan, fw, iw, coef, conv_w, hbias)


def _head_slots(x, kv_head, lo):
    xr = pltpu.roll(x, HEAD_DIM, axis=1)
    if kv_head == 0:
        return jnp.where(lo, x, 0.0), jnp.where(lo, 0.0, xr)
    return jnp.where(lo, xr, 0.0), jnp.where(lo, 0.0, x)


def _slot_ones(rows, slot):
    lo = lax.broadcasted_iota(jnp.int32, (rows, LANES), 1) < HEAD_DIM
    return jnp.where(lo if slot == 0 else jnp.logical_not(lo), 1.0, 0.0).astype(BF16)


def _ctx_attn_kernel(sink_ref, q_ref, k_ref, v_ref, o_ref, kt_ref, vt_ref, *, n, group):
    lo = lax.broadcasted_iota(jnp.int32, (n, LANES), 1) < HEAD_DIM
    top = lax.broadcasted_iota(jnp.int32, (2 * n, 1), 0) < n
    lo_g = lax.broadcasted_iota(jnp.int32, (2 * n, LANES), 1) < HEAD_DIM
    ones = jnp.concatenate([_slot_ones(n, 0), _slot_ones(n, 1)], axis=0)
    for b in range(group):
        kt_ref[b] = k_ref[b].T
        vt_ref[b] = v_ref[b].T
        k = k_ref[b] * LOG2E
        v = v_ref[b]
        for kv_head in range(N_KV_HEADS):
            h0 = 4 * kv_head
            k_cat = jnp.concatenate(_head_slots(k, kv_head, lo), axis=0).astype(BF16)
            v_cat = jnp.concatenate(_head_slots(v, kv_head, lo), axis=0).astype(BF16)
            v_cat = jnp.concatenate([v_cat, ones], axis=1)
            q2 = jnp.concatenate([q_ref[b, :, (2 * kv_head + t) * LANES:(2 * kv_head + t + 1) * LANES]
                                  for t in range(2)], axis=0) * HEAD_DIM ** -0.5
            s = _dot_nt(q2, k_cat)
            e, sink_term = [], []
            for slot in range(2):
                sink = jnp.where(top, sink_ref[h0 + slot], sink_ref[h0 + 2 + slot]) * LOG2E
                cols = s[:, slot * n:(slot + 1) * n]
                m = jnp.maximum(jnp.max(cols, axis=-1, keepdims=True), sink)
                e.append(jnp.exp2(cols - m).astype(BF16))
                sink_term.append(jnp.exp2(sink - m))
            o = _dot(jnp.concatenate(e, axis=1), v_cat)
            o = o[:, :LANES] / (o[:, LANES:] + jnp.where(lo_g, sink_term[0], sink_term[1]))
            for t in range(2):
                tile = 2 * kv_head + t
                o_ref[b, :, tile * LANES:(tile + 1) * LANES] = o[t * n:(t + 1) * n].astype(o_ref.dtype)


def _ctx_attention(q, k, v, b, n, sink):
    group = CTX_ATTN_GROUP
    return pl.pallas_call(
        functools.partial(_ctx_attn_kernel, n=n, group=group),
        grid=(b // group,),
        in_specs=[
            pl.BlockSpec(memory_space=pltpu.SMEM),
            pl.BlockSpec((group, n, ATTN_WIDTH), lambda i: (i, 0, 0)),
            pl.BlockSpec((group, n, KV_WIDTH), lambda i: (i, 0, 0)),
            pl.BlockSpec((group, n, KV_WIDTH), lambda i: (i, 0, 0)),
        ],
        out_specs=[pl.BlockSpec((group, n, ATTN_WIDTH), lambda i: (i, 0, 0)),
                   pl.BlockSpec((group, KV_WIDTH, n), lambda i: (i, 0, 0)),
                   pl.BlockSpec((group, KV_WIDTH, n), lambda i: (i, 0, 0))],
        out_shape=[jax.ShapeDtypeStruct((b, n, ATTN_WIDTH), BF16),
                   jax.ShapeDtypeStruct((b, KV_WIDTH, n), F32),
                   jax.ShapeDtypeStruct((b, KV_WIDTH, n), F32)],
        compiler_params=_cparams("parallel"),
        name="ctx_attention",
    )(sink, q, k, v)


def _rope(x, cos, sin_signed):
    lane = lax.broadcasted_iota(jnp.int32, x.shape, 1)
    first = (lane % (HEAD_DIM // 2)) < HEAD_DIM // 4
    partner = jnp.where(first, pltpu.roll(x, LANES - HEAD_DIM // 4, axis=1),
                        pltpu.roll(x, HEAD_DIM // 4, axis=1))
    return x * cos + partner * sin_signed


def _lat_attn_kernel(sink_ref, q_ref, k_ref, v_ref, ck_ref, cv_ref, cos_ref, sin_ref, o_ref,
                     q_scr, k_scr, v_scr, ck_scr, cv_scr, *, n, c_len):
    nb = n // BLOCK
    cb = c_len // BLOCK
    grp = 2 * BLOCK
    cos, sin = cos_ref[...], sin_ref[...]
    lo = lax.broadcasted_iota(jnp.int32, (n, LANES), 1) < HEAD_DIM
    lo_c = lax.broadcasted_iota(jnp.int32, (c_len, LANES), 1) < HEAD_DIM

    k = _rope(k_ref[0], cos, sin) * LOG2E
    v = v_ref[0]
    ck = ck_ref[0, 0] * LOG2E
    for kv_head in range(N_KV_HEADS):
        for t in range(2):
            tile = 2 * kv_head + t
            q = q_ref[0, :, tile * LANES:(tile + 1) * LANES].astype(F32)
            q = (_rope(q, cos, sin) * HEAD_DIM ** -0.5).astype(BF16)
            for i in range(nb):
                q_scr[kv_head, i * grp + t * BLOCK:i * grp + (t + 1) * BLOCK] = q[i * BLOCK:(i + 1) * BLOCK]
        for slot, (ks, vs, cks, cvs) in enumerate(zip(
                _head_slots(k, kv_head, lo), _head_slots(v, kv_head, lo),
                _head_slots(ck, kv_head, lo_c), _head_slots(cv_ref[0, 0], kv_head, lo_c))):
            ks, vs, cks, cvs = (a.astype(BF16) for a in (ks, vs, cks, cvs))
            ones = _slot_ones(BLOCK, slot)
            for j in range(nb):
                rows = slice(j * grp + slot * BLOCK, j * grp + (slot + 1) * BLOCK)
                k_scr[kv_head, rows] = ks[j * BLOCK:(j + 1) * BLOCK]
                v_scr[kv_head, rows, :LANES] = vs[j * BLOCK:(j + 1) * BLOCK]
                v_scr[kv_head, rows, LANES:] = ones
            for j in range(cb):
                rows = slice(j * grp + slot * BLOCK, j * grp + (slot + 1) * BLOCK)
                ck_scr[kv_head, rows] = cks[j * BLOCK:(j + 1) * BLOCK]
                cv_scr[kv_head, rows, :LANES] = cvs[j * BLOCK:(j + 1) * BLOCK]
                cv_scr[kv_head, rows, LANES:] = ones

    qi = lax.broadcasted_iota(jnp.int32, (grp, BLOCK), 0) % BLOCK
    kj = lax.broadcasted_iota(jnp.int32, (grp, BLOCK), 1)
    keep_prev = kj >= qi
    keep_next = kj <= qi
    top = lax.broadcasted_iota(jnp.int32, (grp, 1), 0) < BLOCK
    lo_g = lax.broadcasted_iota(jnp.int32, (grp, LANES), 1) < HEAD_DIM

    def lane_tiles(s):
        return [s[:, c * BLOCK:(c + 1) * BLOCK] for c in range(s.shape[1] // BLOCK)]

    for kv_head in range(N_KV_HEADS):
        h0 = 4 * kv_head
        sinks = [jnp.where(top, sink_ref[h0 + slot], sink_ref[h0 + 2 + slot]) * LOG2E for slot in range(2)]
        for i in range(nb):
            j0, j1 = max(i - 1, 0), min(i + 2, nb)
            q2 = q_scr[kv_head, i * grp:(i + 1) * grp]
            tiles = lane_tiles(_dot_nt(q2, k_scr[kv_head, j0 * grp:j1 * grp]))
            for b, j in enumerate(range(j0, j1)):
                for slot in range(2):
                    if j == i - 1:
                        tiles[2 * b + slot] = jnp.where(keep_prev, tiles[2 * b + slot], NEG_BIG)
                    elif j == i + 1:
                        tiles[2 * b + slot] = jnp.where(keep_next, tiles[2 * b + slot], NEG_BIG)
            tiles += lane_tiles(_dot_nt(q2, ck_scr[kv_head]))
            sink_term = []
            for slot in range(2):
                mine = tiles[slot::2]
                m = jnp.maximum(jnp.max(functools.reduce(jnp.maximum, mine), axis=-1, keepdims=True),
                                sinks[slot])
                tiles[slot::2] = [jnp.exp2(tl - m).astype(BF16) for tl in mine]
                sink_term.append(jnp.exp2(sinks[slot] - m))
            n_loc = 2 * (j1 - j0)
            e_loc = jnp.concatenate(tiles[:n_loc], axis=1)
            e_ctx = jnp.concatenate(tiles[n_loc:], axis=1)
            o = _dot(e_loc, v_scr[kv_head, j0 * grp:j1 * grp]) + _dot(e_ctx, cv_scr[kv_head])
            o = o[:, :LANES] / (o[:, LANES:] + jnp.where(lo_g, sink_term[0], sink_term[1]))
            o = o.astype(o_ref.dtype)
            for t in range(2):
                tile = 2 * kv_head + t
                o_ref[0, i * BLOCK:(i + 1) * BLOCK, tile * LANES:(tile + 1) * LANES] = o[t * BLOCK:(t + 1) * BLOCK]


def _lat_attention(q, k, v, b, first, n, sink, cache_k, cache_v, layer, cos, sin):
    c_len = cache_k.shape[2]
    return pl.pallas_call(
        functools.partial(_lat_attn_kernel, n=n, c_len=c_len),
        grid=(b,),
        in_specs=[
            pl.BlockSpec(memory_space=pltpu.SMEM),
            pl.BlockSpec((1, n, ATTN_WIDTH), lambda i: (i + first, 0, 0)),
            pl.BlockSpec((1, n, KV_WIDTH), lambda i: (i + first, 0, 0)),
            pl.BlockSpec((1, n, KV_WIDTH), lambda i: (i + first, 0, 0)),
            pl.BlockSpec((1, 1, c_len, KV_WIDTH), lambda i: (i, layer, 0, 0)),
            pl.BlockSpec((1, 1, c_len, KV_WIDTH), lambda i: (i, layer, 0, 0)),
            _resident((n, LANES), lambda i: (0, 0)),
            _resident((n, LANES), lambda i: (0, 0)),
        ],
        out_specs=pl.BlockSpec((1, n, ATTN_WIDTH), lambda i: (i, 0, 0)),
        out_shape=jax.ShapeDtypeStruct((b, n, ATTN_WIDTH), BF16),
        scratch_shapes=[
            pltpu.VMEM((N_KV_HEADS, 2 * n, LANES), BF16),
            pltpu.VMEM((N_KV_HEADS, 2 * n, LANES), BF16),
            pltpu.VMEM((N_KV_HEADS, 2 * n, 2 * LANES), BF16),
            pltpu.VMEM((N_KV_HEADS, 2 * c_len, LANES), BF16),
            pltpu.VMEM((N_KV_HEADS, 2 * c_len, 2 * LANES), BF16),
        ],
        compiler_params=_cparams("parallel"),
        name="latent_attention",
    )(sink, q, k, v, cache_k, cache_v, cos, sin)


def kernel(x_prompt, x_sample, cache_k, cache_v, c, c_ctx, w_mod, b_mod, norm_g, ffn_w_gate, ffn_w_up,
           ffn_w_down, w_in, w_out, hyena_conv_w, hyena_f_w1, hyena_f_b1, hyena_f_w2, hyena_f_b2,
           hyena_f_w3, hyena_f_freq, hyena_decay, hyena_bias, attn_sink):
    batch, seq, d = x_prompt.shape
    dec_batch, dec_seq, _ = x_sample.shape
    past_len = cache_k.shape[2]

    cvecs = jnp.concatenate([c_ctx[None], c], axis=0)
    cvecs = jnp.pad(cvecs, ((0, MOD_ROWS - cvecs.shape[0]), (0, 0)))
    mod = _modulation(cvecs, w_mod, b_mod).reshape(DEPTH, MOD_ROWS, N_SUB, 3, d)

    wg = ffn_w_gate.astype(BF16)
    wu = ffn_w_up.astype(BF16)
    wd = ffn_w_down.astype(BF16)
    w_in_b = w_in.astype(BF16)
    w_out_b = w_out.astype(BF16)
    ck = cache_k.reshape(dec_batch, DEPTH, past_len, KV_WIDTH)
    cv = cache_v.reshape(dec_batch, DEPTH, past_len, KV_WIDTH)
    rope_cos, rope_sin = (jnp.asarray(t) for t in _rope_tables(dec_seq))

    tables = {}
    for n in (seq, dec_seq):
        fw, iw = (jnp.asarray(t).astype(BF16) for t in _hyena_dft(n))
        pos, chan = (jnp.asarray(t).astype(BF16) for t in _fourier_tables(n))
        tables[n] = (fw, iw, pos, chan)

    ctx_rows, lat_rows = batch * seq, dec_batch * dec_seq
    rows = ctx_rows + lat_rows
    sizes = (ctx_rows, lat_rows, dec_seq)
    xs = (x_prompt.reshape(ctx_rows, d), x_sample.reshape(lat_rows, d))
    new_k, new_v = [], []
    for l in range(DEPTH):
        first, last = l == 0, l == DEPTH - 1
        x1, p_fh, q, k, v = _stage_a(xs, mod, norm_g, wg, wu, wd, w_in_b, l, *sizes,
                                     tm=EDGE_TILE if first else TOKEN_TILE)
        y_fh, y_att = [], []
        for n, nseq, first_seq, latent in ((seq, batch, 0, False), (dec_seq, dec_batch, ctx_rows // dec_seq, True)):
            fw, iw, pos, chan = tables[n]
            coef = _filter_spectrum(n, fw, hyena_f_w1[l], hyena_f_b1[l], hyena_f_w2[l], hyena_f_b2[l],
                                    hyena_f_w3[l], hyena_f_freq[l], hyena_decay[l])
            y = _fh_mix(p_fh.reshape(rows // n, n, FH_WIDTH), nseq, first_seq, n, FH_GROUP[latent],
                        pos, chan, fw, iw, coef, hyena_conv_w[l], hyena_bias[l])
            y_fh.append(y.reshape(nseq * n, -1))
            qs, ks, vs = (a.reshape(rows // n, n, -1) for a in (q, k, v))
            if latent:
                y = _lat_attention(qs, ks, vs, nseq, first_seq, n, attn_sink[l], ck, cv, l, rope_cos, rope_sin)
            else:
                y, k_t, v_t = _ctx_attention(qs, ks, vs, nseq, n, attn_sink[l])
                new_k.append(k_t)
                new_v.append(v_t)
            y_att.append(y.reshape(nseq * n, -1))
        xs = _stage_b(x1, y_fh, y_att, mod, norm_g, w_out_b, wg, wu, wd, l, *sizes,
                      tm=EDGE_TILE if last else TOKEN_TILE, split=last)

    def cache_layout(per_layer):
        t = jnp.stack(per_layer, axis=0).reshape(DEPTH, batch, N_KV_HEADS, HEAD_DIM, seq)
        return jnp.transpose(t, (1, 0, 4, 2, 3))

    return (xs[0].reshape(batch, seq, d), xs[1].reshape(dec_batch, dec_seq, d),
            cache_layout(new_k), cache_layout(new_v))
```

```python
import functools
import math

import numpy as np
import jax
import jax.numpy as jnp
from jax import lax
from jax.experimental import pallas as pl
from jax.experimental.pallas import tpu as pltpu

F32 = jnp.float32
BF16 = jnp.bfloat16

D_MODEL = 1024
DEPTH = 2
GRID_W = 64
HEAD_DIM = 64
N_Q_HEADS = 8
N_KV_HEADS = 2
ATTN_WIDTH = N_Q_HEADS * HEAD_DIM
KV_WIDTH = N_KV_HEADS * HEAD_DIM
FOURIER_WIDTH = 256
FOURIER_GROUPS = 4
FOURIER_GROUP_DIM = 64
HYENA_WIDTH = 256
HYENA_ORDER = 2
HYENA_PROJ = 3 * HYENA_WIDTH
HYENA_EMB_DIM = 33
HYENA_BANDS = 16
HYENA_FILTER_WIDTH = 64
FH_WIDTH = FOURIER_WIDTH + HYENA_PROJ
ATT_WIDTH = ATTN_WIDTH + 2 * KV_WIDTH
IN_WIDTH = FH_WIDTH + ATT_WIDTH
BLOCK = 128
WINDOW = 128
ROPE_BASE = 10000.0
D_FF = 2816
N_SUB = 3
RMS_EPS = 1e-6

LANES = 128
MOD_ROWS = 16
VMEM_LIMIT = 56 * 1024 * 1024
NEG_BIG = -1e30
LOG2E = 1.4426950408889634

TOKEN_TILE = 1024
EDGE_TILE = 512
ROW_GROUP = 256
MOD_TILE = 3072
FH_GROUP = {False: 8, True: 2}
CTX_ATTN_GROUP = 4


def _cparams(*sem):
    return pltpu.CompilerParams(dimension_semantics=sem, vmem_limit_bytes=VMEM_LIMIT)


def _rms(x, g):
    return x * lax.rsqrt(jnp.mean(x * x, axis=-1, keepdims=True) + RMS_EPS) * g


def _dot(a, b):
    return jnp.dot(a, b, preferred_element_type=F32)


def _dot_nt(a, b):
    return lax.dot_general(a, b, (((1,), (1,)), ((), ())), preferred_element_type=F32)


@functools.lru_cache(maxsize=None)
def _hyena_dft(n):
    f = np.arange(n, dtype=np.int64)[:, None]
    s = np.arange(n, dtype=np.int64)[None, :]
    ang = np.pi * ((f * s) % (2 * n)).astype(np.float64) / n
    c = np.cos(ang)
    sn = np.sin(ang)
    sn[0, :] = 1.0 - 2.0 * (np.arange(n) % 2)
    fw = np.concatenate([c, sn], axis=0).astype(np.float32)
    iw = np.concatenate([c, sn.T], axis=1).astype(np.float32)
    return fw, iw


@functools.lru_cache(maxsize=None)
def _fourier_tables(n):
    f = np.arange(n, dtype=np.int64)[:, None]
    s = np.arange(n, dtype=np.int64)[None, :]
    ang = 2.0 * np.pi * ((f * s) % n).astype(np.float64) / n
    pos = np.concatenate([np.cos(ang), -np.sin(ang)], axis=1) / math.sqrt(n)
    gd = FOURIER_GROUP_DIM
    a = np.arange(gd, dtype=np.int64)
    ang_c = 2.0 * np.pi * ((a[:, None] * a[None, :]) % gd).astype(np.float64) / gd
    bc = np.kron(np.eye(FOURIER_GROUPS), np.cos(ang_c)) / math.sqrt(gd)
    bs = np.kron(np.eye(FOURIER_GROUPS), np.sin(ang_c)) / math.sqrt(gd)
    chan = np.concatenate([bc, bs], axis=1)
    return pos.astype(np.float32), chan.astype(np.float32)


@functools.lru_cache(maxsize=None)
def _rope_tables(n):
    half = HEAD_DIM // 2
    inv = ROPE_BASE ** (-np.arange(0, half, 2, dtype=np.float64) / half)
    t = np.arange(n)
    row = (t // GRID_W).astype(np.float64)
    col = (t % GRID_W).astype(np.float64)
    lane = np.arange(LANES)
    d = lane % HEAD_DIM
    pos = np.where((d // half)[None, :] == 0, row[:, None], col[:, None])
    ang = pos * inv[d % (half // 2)][None, :]
    sign = np.where((d % half) < half // 2, -1.0, 1.0)[None, :]
    return np.cos(ang).astype(np.float32), (np.sin(ang) * sign).astype(np.float32)


def _hyena_feats(n):
    d = jnp.arange(n, dtype=F32)
    t = jnp.linspace(0.0, 1.0, n, dtype=F32)[:, None]
    f = jnp.linspace(1e-4, HYENA_BANDS - 1, HYENA_BANDS, dtype=F32)
    ang = (2.0 * math.pi / n) * d[:, None] * f[None, :]
    feats = jnp.concatenate([t, jnp.cos(ang), -jnp.sin(ang)], axis=-1)
    return jnp.pad(feats, ((0, 0), (0, LANES - HYENA_EMB_DIM))), t


def _mod_kernel(c_ref, w_ref, b_ref, o_ref):
    c = c_ref[...]
    s = c / (1.0 + jnp.exp(-c))
    o_ref[0] = _dot(s.astype(BF16), w_ref[0].astype(BF16)) + b_ref[0]


def _modulation(cvecs, w_mod, b_mod):
    depth, _, width = w_mod.shape
    tn = MOD_TILE
    return pl.pallas_call(
        _mod_kernel,
        grid=(depth, width // tn),
        in_specs=[
            pl.BlockSpec((MOD_ROWS, D_MODEL), lambda l, j: (0, 0)),
            pl.BlockSpec((1, D_MODEL, tn), lambda l, j: (l, 0, j)),
            pl.BlockSpec((1, 1, tn), lambda l, j: (l, 0, j)),
        ],
        out_specs=pl.BlockSpec((1, MOD_ROWS, tn), lambda l, j: (l, 0, j)),
        out_shape=jax.ShapeDtypeStruct((depth, MOD_ROWS, width), F32),
        compiler_params=_cparams("arbitrary", "arbitrary"),
        name="modulation",
    )(cvecs, w_mod, b_mod.reshape(depth, 1, width))


def _resident(block_shape, index_map):
    return pl.BlockSpec(block_shape, index_map, pipeline_mode=pl.Buffered(1))


def _modulate(x, g_row, mod_ref, sub):
    return (_rms(x, g_row) * (1.0 + mod_ref[0, sub, 1:2]) + mod_ref[0, sub, 0:1]).astype(BF16)


def _swiglu_rows(x, h, mod_ref, sub, g_ref, wg_ref, wu_ref, wd_ref):
    gate = _dot(h, wg_ref[...])
    up = _dot(h, wu_ref[...])
    act = (gate / (1.0 + jnp.exp(-gate))) * up
    y = _dot(act.astype(BF16), wd_ref[...])
    return x + (0.5 * mod_ref[0, sub, 2:3]) * _rms(y, g_ref[2 * sub + 1:2 * sub + 2])


def _software_pipeline(rows, phases):
    groups = [slice(r, r + ROW_GROUP) for r in range(0, rows, ROW_GROUP)]
    state = {}
    for turn in range(len(groups) + len(phases) - 1):
        for p, phase in enumerate(phases):
            g = turn - p
            if 0 <= g < len(groups):
                state[g] = phase(groups[g], state.get(g))


def _stage_weight(src_hbm, dst_ref, stage_ref, sem):
    depth, rows = stage_ref.shape[0], stage_ref.shape[1]
    n_chunks = dst_ref.shape[0] // rows

    def chunk_copy(c):
        slot = c % depth
        return pltpu.make_async_copy(src_hbm.at[pl.ds(c * rows, rows)], stage_ref.at[slot], sem.at[slot])

    for c in range(depth - 1):
        chunk_copy(c).start()
    for c in range(n_chunks):
        if c + depth - 1 < n_chunks:
            chunk_copy(c + depth - 1).start()
        chunk_copy(c).wait()
        dst_ref[c * rows:(c + 1) * rows] = stage_ref[c % depth].astype(BF16)


def _stage_ffn_weights(hbm_refs, layer, which, vmem_refs, up_stage, down_stage, sem):
    @pl.when(pl.program_id(0) == 0)
    def _():
        wg_hbm, wu_hbm, wd_hbm = hbm_refs
        wg_ref, wu_ref, wd_ref = vmem_refs
        _stage_weight(wg_hbm.at[layer, which], wg_ref, up_stage, sem)
        _stage_weight(wu_hbm.at[layer, which], wu_ref, up_stage, sem)
        _stage_weight(wd_hbm.at[layer, which], wd_ref, down_stage, sem)


STAGE_DEPTH = 6
UP_CHUNK = 32
DOWN_CHUNK = 64
_FFN_WEIGHT_SPECS = [pl.BlockSpec(memory_space=pl.ANY)] * 3
_FFN_WEIGHT_SCRATCH = [
    pltpu.VMEM((D_MODEL, D_FF), BF16), pltpu.VMEM((D_MODEL, D_FF), BF16), pltpu.VMEM((D_FF, D_MODEL), BF16),
    pltpu.VMEM((STAGE_DEPTH, UP_CHUNK, D_FF), F32), pltpu.VMEM((STAGE_DEPTH, DOWN_CHUNK, D_MODEL), F32),
    pltpu.SemaphoreType.DMA((STAGE_DEPTH,)),
]


def _stage_a_kernel(*refs, n_in, ctx_tiles, layer):
    x_refs = refs[:n_in]
    (mod_ref, g_ref, wg_hbm, wu_hbm, wd_hbm, win_ref, x1_ref, fh_ref, q_ref, k_ref, v_ref,
     wg_ref, wu_ref, wd_ref, up_stage, down_stage, sem) = refs[n_in:]
    _stage_ffn_weights((wg_hbm, wu_hbm, wd_hbm), layer, 0, (wg_ref, wu_ref, wd_ref), up_stage, down_stage, sem)
    is_ctx = pl.program_id(0) < ctx_tiles
    k0 = FH_WIDTH + ATTN_WIDTH

    def modulated_input(sl, _):
        x = x_refs[0][sl] if n_in == 1 else jnp.where(is_ctx, x_refs[0][sl], x_refs[1][sl])
        return x, _modulate(x, g_ref[0:1], mod_ref, 0)

    def half_step(sl, xh):
        x1 = _swiglu_rows(*xh, mod_ref, 0, g_ref, wg_ref, wu_ref, wd_ref)
        x1_ref[sl] = x1
        return _modulate(x1, g_ref[2:3], mod_ref, 1)

    def in_projection(sl, h):
        p = _dot(h, win_ref[...])
        fh_ref[sl] = p[:, :FH_WIDTH].astype(fh_ref.dtype)
        q_ref[sl] = p[:, FH_WIDTH:k0].astype(q_ref.dtype)
        k_ref[sl] = p[:, k0:k0 + KV_WIDTH]
        v_ref[sl] = p[:, k0 + KV_WIDTH:]

    _software_pipeline(x1_ref.shape[0], [modulated_input, half_step, in_projection])


def _stage_b_kernel(x_ref, fhc_ref, fhl_ref, atc_ref, atl_ref, mod_ref, g_ref, wout_ref, wg_hbm, wu_hbm,
                    wd_hbm, *refs, ctx_tiles, layer):
    *o_refs, wg_ref, wu_ref, wd_ref, up_stage, down_stage, sem = refs
    _stage_ffn_weights((wg_hbm, wu_hbm, wd_hbm), layer, 1, (wg_ref, wu_ref, wd_ref), up_stage, down_stage, sem)
    is_ctx = pl.program_id(0) < ctx_tiles
    half = FOURIER_WIDTH + HYENA_WIDTH

    def body(mixed, o_ref):
        def out_projection(sl, _):
            y_fh, y_at = mixed(sl)
            y = _dot(y_fh, wout_ref[:half]) + _dot(y_at, wout_ref[half:])
            x2 = x_ref[sl] + mod_ref[0, 1, 2:3] * _rms(y, g_ref[3:4])
            return x2, _modulate(x2, g_ref[4:5], mod_ref, 2)

        def half_step(sl, xh):
            o_ref[sl] = _swiglu_rows(*xh, mod_ref, 2, g_ref, wg_ref, wu_ref, wd_ref)

        _software_pipeline(x_ref.shape[0], [out_projection, half_step])

    if len(o_refs) == 1:
        body(lambda sl: (jnp.where(is_ctx, fhc_ref[sl], fhl_ref[sl]),
                         jnp.where(is_ctx, atc_ref[sl], atl_ref[sl])), o_refs[0])
    else:
        pl.when(is_ctx)(lambda: body(lambda sl: (fhc_ref[sl], atc_ref[sl]), o_refs[0]))
        pl.when(jnp.logical_not(is_ctx))(lambda: body(lambda sl: (fhl_ref[sl], atl_ref[sl]), o_refs[1]))


def _token_specs(tm, ctx_rows, lat_rows, lat_seq):
    ctx_tiles = ctx_rows // tm
    tiles_per_seq = lat_seq // tm
    tile = lambda width: pl.BlockSpec((tm, width), lambda i: (i, 0))
    ctx_tile = lambda width: pl.BlockSpec((tm, width), lambda i: (jnp.minimum(i, ctx_tiles - 1), 0))
    lat_tile = lambda width: pl.BlockSpec((tm, width), lambda i: (jnp.maximum(i - ctx_tiles, 0), 0))
    request = lambda i: jnp.where(i < ctx_tiles, 0, 1 + (i - ctx_tiles) // tiles_per_seq)
    return ctx_tiles, (ctx_rows + lat_rows) // tm, tile, ctx_tile, lat_tile, request


def _stage_a(xs, mod, norm_g, wg, wu, wd, w_in, layer, ctx_rows, lat_rows, lat_seq, tm):
    ctx_tiles, tiles, tile, ctx_tile, lat_tile, request = _token_specs(tm, ctx_rows, lat_rows, lat_seq)
    t = ctx_rows + lat_rows
    widths = (D_MODEL, FH_WIDTH, ATTN_WIDTH, KV_WIDTH, KV_WIDTH)
    dtypes = (F32, BF16, BF16, F32, F32)
    x_specs = [tile(D_MODEL)] if len(xs) == 1 else [ctx_tile(D_MODEL), lat_tile(D_MODEL)]
    return pl.pallas_call(
        functools.partial(_stage_a_kernel, n_in=len(xs), ctx_tiles=ctx_tiles, layer=layer),
        grid=(tiles,),
        in_specs=x_specs + [
            pl.BlockSpec((None, 1, N_SUB, 3, D_MODEL), lambda i: (layer, request(i), 0, 0, 0)),
            _resident((None, 2 * N_SUB, D_MODEL), lambda i: (layer, 0, 0)),
            *_FFN_WEIGHT_SPECS,
            _resident((None, D_MODEL, IN_WIDTH), lambda i: (layer, 0, 0)),
        ],
        out_specs=[tile(w) for w in widths],
        out_shape=[jax.ShapeDtypeStruct((t, w), dt) for w, dt in zip(widths, dtypes)],
        scratch_shapes=_FFN_WEIGHT_SCRATCH,
        compiler_params=_cparams("arbitrary"),
        name="stage_a",
    )(*xs, mod, norm_g, wg, wu, wd, w_in)


def _stage_b(x1, y_fh, y_att, mod, norm_g, w_out, wg, wu, wd, layer, ctx_rows, lat_rows, lat_seq, tm, split):
    ctx_tiles, tiles, tile, ctx_tile, lat_tile, request = _token_specs(tm, ctx_rows, lat_rows, lat_seq)
    half = FOURIER_WIDTH + HYENA_WIDTH
    if split:
        out_specs = [ctx_tile(D_MODEL), lat_tile(D_MODEL)]
        out_shape = [jax.ShapeDtypeStruct((r, D_MODEL), F32) for r in (ctx_rows, lat_rows)]
    else:
        out_specs = [tile(D_MODEL)]
        out_shape = [jax.ShapeDtypeStruct((ctx_rows + lat_rows, D_MODEL), F32)]
    return pl.pallas_call(
        functools.partial(_stage_b_kernel, ctx_tiles=ctx_tiles, layer=layer),
        grid=(tiles,),
        in_specs=[
            tile(D_MODEL), ctx_tile(half), lat_tile(half), ctx_tile(ATTN_WIDTH), lat_tile(ATTN_WIDTH),
            pl.BlockSpec((None, 1, N_SUB, 3, D_MODEL), lambda i: (layer, request(i), 0, 0, 0)),
            _resident((None, 2 * N_SUB, D_MODEL), lambda i: (layer, 0, 0)),
            _resident((None, D_MODEL, D_MODEL), lambda i: (layer, 0, 0)),
            *_FFN_WEIGHT_SPECS,
        ],
        out_specs=out_specs,
        out_shape=out_shape,
        scratch_shapes=_FFN_WEIGHT_SCRATCH,
        compiler_params=_cparams("arbitrary"),
        name="stage_b",
    )(x1, y_fh[0], y_fh[1], y_att[0], y_att[1], mod, norm_g, w_out, wg, wu, wd)


def _filter_kernel(feats_ref, t_ref, w1_ref, b1_ref, w2_ref, b2_ref, w3_ref, fr_ref, decay_ref,
                   fw_ref, o_ref, *, n):
    fr = fr_ref[...]
    h = jnp.sin(fr * (_dot(feats_ref[...], w1_ref[...]) + b1_ref[...]))
    h = jnp.sin(fr * (_dot(h, w2_ref[...]) + b2_ref[...]))
    h = _dot(h, w3_ref[...])
    window = jnp.exp(-t_ref[...] * jnp.abs(decay_ref[...]))
    width = HYENA_ORDER * HYENA_WIDTH
    row = lax.broadcasted_iota(jnp.int32, (n, width), 0)
    fwd = h[:, :width] * window
    bwd = jnp.where(row == 0, 0.0, h[:, width:] * window)
    even = fwd + bwd
    odd = fwd - bwd
    k_re = _dot(fw_ref[:n], even.astype(BF16))
    k_sn = _dot(fw_ref[n:], odd.astype(BF16))
    sign = (1 - 2 * (row % 2)).astype(F32)
    k_ny = jnp.sum(even * sign, axis=0, keepdims=True)
    s0 = 1.0 / (4.0 * n * n)
    scale = jnp.where(row == 0, s0, 2.0 * s0)
    a = scale * k_re
    o_ref[0] = a
    o_ref[1] = jnp.where(row == 0, 0.0, scale * k_sn)
    o_ref[2] = jnp.where(row == 0, s0 * k_ny, a)


def _filter_spectrum(n, fw, w1, b1, w2, b2, w3, freq, decay):
    feats, t = _hyena_feats(n)
    pad_w = LANES - HYENA_FILTER_WIDTH
    w1p = jnp.pad(w1, ((0, LANES - HYENA_EMB_DIM), (0, pad_w)))
    w2p = jnp.pad(w2, ((0, pad_w), (0, pad_w)))
    w3p = jnp.pad(w3, ((0, pad_w), (0, 0)))
    padv = lambda v: jnp.pad(v, (0, pad_w)).reshape(1, LANES)
    width = HYENA_ORDER * HYENA_WIDTH
    args = (feats, t, w1p, padv(b1), w2p, padv(b2), w3p, padv(freq), decay.reshape(1, width), fw)
    return pl.pallas_call(
        functools.partial(_filter_kernel, n=n),
        out_shape=jax.ShapeDtypeStruct((3, n, width), F32),
        compiler_params=pltpu.CompilerParams(vmem_limit_bytes=VMEM_LIMIT),
        name="hyena_filter",
    )(*args)


def _fh_kernel(p_ref, pos_ref, chan_ref, fw_ref, iw_ref, coef_ref, convw_ref, bias_ref, o_ref,
               st_scr, *, n, group):
    c = HYENA_WIDTH
    cols = [slice(b * c, (b + 1) * c) for b in range(group)]

    for b in range(group):
        t = _dot(p_ref[b, :, :FOURIER_WIDTH], chan_ref[...])
        st_scr[:n, cols[b]] = t[:, :FOURIER_WIDTH].astype(BF16)
        st_scr[n:, cols[b]] = t[:, FOURIER_WIDTH:].astype(BF16)
    y_f = _dot(pos_ref[...], st_scr[...])
    for b in range(group):
        o_ref[b, :, :FOURIER_WIDTH] = y_f[:, cols[b]].astype(o_ref.dtype)

    row = lax.broadcasted_iota(jnp.int32, (n, HYENA_PROJ), 0)
    zs = []
    for b in range(group):
        z = p_ref[b, :, FOURIER_WIDTH:].astype(F32)
        z_prev = jnp.where(row == 0, 0.0, pltpu.roll(z, 1, axis=0))
        z_next = jnp.where(row == n - 1, 0.0, pltpu.roll(z, n - 1, axis=0))
        zs.append(z_prev * convw_ref[0:1] + z * convw_ref[1:2] + z_next * convw_ref[2:3])

    def long_conv(vs, order):
        oc = slice(order * c, (order + 1) * c)
        uf = _dot(fw_ref[...], jnp.concatenate([v.astype(BF16) for v in vs], axis=1))
        a, bb, a2 = coef_ref[0, :, oc], coef_ref[1, :, oc], coef_ref[2, :, oc]
        for b in range(group):
            u_re, u_sn = uf[:n, cols[b]], uf[n:, cols[b]]
            st_scr[:n, cols[b]] = (u_re * a - u_sn * bb).astype(BF16)
            st_scr[n:, cols[b]] = (u_re * bb + u_sn * a2).astype(BF16)
        y = _dot(iw_ref[...], st_scr[...])
        return [y[:, cols[b]] + vs[b] * bias_ref[order:order + 1] for b in range(group)]

    ys = long_conv([z[:, :c] for z in zs], 0)
    ys = long_conv([z[:, c:2 * c] * y for z, y in zip(zs, ys)], 1)
    for b in range(group):
        o_ref[b, :, FOURIER_WIDTH:] = (zs[b][:, 2 * c:] * ys[b]).astype(o_ref.dtype)


def _fh_mix(p_fh, b, first, n, group, pos, chan, fw, iw, coef, conv_w, hbias):
    width = FOURIER_WIDTH + HYENA_WIDTH
    const = lambda shape: _resident(shape, lambda i: (0,) * len(shape))
    return pl.pallas_call(
        functools.partial(_fh_kernel, n=n, group=group),
        grid=(b // group,),
        in_specs=[
            pl.BlockSpec((group, n, FH_WIDTH), lambda i: (i + first // group, 0, 0)),
            const((n, 2 * n)), const((FOURIER_WIDTH, 2 * FOURIER_WIDTH)),
            const((2 * n, n)), const((n, 2 * n)),
            const((3, n, HYENA_ORDER * HYENA_WIDTH)),
            const((3, HYENA_PROJ)), const((HYENA_ORDER, HYENA_WIDTH)),
        ],
        out_specs=pl.BlockSpec((group, n, width), lambda i: (i, 0, 0)),
        out_shape=jax.ShapeDtypeStruct((b, n, width), BF16),
        scratch_shapes=[pltpu.VMEM((2 * n, group * HYENA_WIDTH), BF16)],
        compiler_params=_cparams("parallel"),
        name="fourier_hyena",
    )(p_fh, pos, chan, fw, iw, coef, conv_w, hbias)


def _head_slots(x, kv_head, lo):
    xr = pltpu.roll(x, HEAD_DIM, axis=1)
    if kv_head == 0:
        return jnp.where(lo, x, 0.0), jnp.where(lo, 0.0, xr)
    return jnp.where(lo, xr, 0.0), jnp.where(lo, 0.0, x)


def _slot_ones(rows, slot):
    lo = lax.broadcasted_iota(jnp.int32, (rows, LANES), 1) < HEAD_DIM
    return jnp.where(lo if slot == 0 else jnp.logical_not(lo), 1.0, 0.0).astype(BF16)


def _ctx_attn_kernel(sink_ref, q_ref, k_ref, v_ref, o_ref, kt_ref, vt_ref, *, n, group):
    lo = lax.broadcasted_iota(jnp.int32, (n, LANES), 1) < HEAD_DIM
    top = lax.broadcasted_iota(jnp.int32, (2 * n, 1), 0) < n
    lo_g = lax.broadcasted_iota(jnp.int32, (2 * n, LANES), 1) < HEAD_DIM
    ones = jnp.concatenate([_slot_ones(n, 0), _slot_ones(n, 1)], axis=0)
    for b in range(group):
        kt_ref[b] = k_ref[b].T
        vt_ref[b] = v_ref[b].T
        k = k_ref[b] * LOG2E
        v = v_ref[b]
        for kv_head in range(N_KV_HEADS):
            h0 = 4 * kv_head
            k_cat = jnp.concatenate(_head_slots(k, kv_head, lo), axis=0).astype(BF16)
            v_cat = jnp.concatenate(_head_slots(v, kv_head, lo), axis=0).astype(BF16)
            v_cat = jnp.concatenate([v_cat, ones], axis=1)
            q2 = jnp.concatenate([q_ref[b, :, (2 * kv_head + t) * LANES:(2 * kv_head + t + 1) * LANES]
                                  for t in range(2)], axis=0) * HEAD_DIM ** -0.5
            s = _dot_nt(q2, k_cat)
            e, sink_term = [], []
            for slot in range(2):
                sink = jnp.where(top, sink_ref[h0 + slot], sink_ref[h0 + 2 + slot]) * LOG2E
                cols = s[:, slot * n:(slot + 1) * n]
                m = jnp.maximum(jnp.max(cols, axis=-1, keepdims=True), sink)
                e.append(jnp.exp2(cols - m).astype(BF16))
                sink_term.append(jnp.exp2(sink - m))
            o = _dot(jnp.concatenate(e, axis=1), v_cat)
            o = o[:, :LANES] / (o[:, LANES:] + jnp.where(lo_g, sink_term[0], sink_term[1]))
            for t in range(2):
                tile = 2 * kv_head + t
                o_ref[b, :, tile * LANES:(tile + 1) * LANES] = o[t * n:(t + 1) * n].astype(o_ref.dtype)


def _ctx_attention(q, k, v, b, n, sink):
    group = CTX_ATTN_GROUP
    return pl.pallas_call(
        functools.partial(_ctx_attn_kernel, n=n, group=group),
        grid=(b // group,),
        in_specs=[
            pl.BlockSpec(memory_space=pltpu.SMEM),
            pl.BlockSpec((group, n, ATTN_WIDTH), lambda i: (i, 0, 0)),
            pl.BlockSpec((group, n, KV_WIDTH), lambda i: (i, 0, 0)),
            pl.BlockSpec((group, n, KV_WIDTH), lambda i: (i, 0, 0)),
        ],
        out_specs=[pl.BlockSpec((group, n, ATTN_WIDTH), lambda i: (i, 0, 0)),
                   pl.BlockSpec((group, KV_WIDTH, n), lambda i: (i, 0, 0)),
                   pl.BlockSpec((group, KV_WIDTH, n), lambda i: (i, 0, 0))],
        out_shape=[jax.ShapeDtypeStruct((b, n, ATTN_WIDTH), BF16),
                   jax.ShapeDtypeStruct((b, KV_WIDTH, n), F32),
                   jax.ShapeDtypeStruct((b, KV_WIDTH, n), F32)],
        compiler_params=_cparams("parallel"),
        name="ctx_attention",
    )(sink, q, k, v)


def _rope(x, cos, sin_signed):
    lane = lax.broadcasted_iota(jnp.int32, x.shape, 1)
    first = (lane % (HEAD_DIM // 2)) < HEAD_DIM // 4
    partner = jnp.where(first, pltpu.roll(x, LANES - HEAD_DIM // 4, axis=1),
                        pltpu.roll(x, HEAD_DIM // 4, axis=1))
    return x * cos + partner * sin_signed


def _lat_attn_kernel(sink_ref, q_ref, k_ref, v_ref, ck_ref, cv_ref, cos_ref, sin_ref, o_ref,
                     q_scr, k_scr, v_scr, ck_scr, cv_scr, *, n, c_len):
    nb = n // BLOCK
    cb = c_len // BLOCK
    grp = 2 * BLOCK
    cos, sin = cos_ref[...], sin_ref[...]
    lo = lax.broadcasted_iota(jnp.int32, (n, LANES), 1) < HEAD_DIM
    lo_c = lax.broadcasted_iota(jnp.int32, (c_len, LANES), 1) < HEAD_DIM

    k = _rope(k_ref[0], cos, sin) * LOG2E
    v = v_ref[0]
    ck = ck_ref[0, 0] * LOG2E
    for kv_head in range(N_KV_HEADS):
        for t in range(2):
            tile = 2 * kv_head + t
            q = q_ref[0, :, tile * LANES:(tile + 1) * LANES].astype(F32)
            q = (_rope(q, cos, sin) * HEAD_DIM ** -0.5).astype(BF16)
            for i in range(nb):
                q_scr[kv_head, i * grp + t * BLOCK:i * grp + (t + 1) * BLOCK] = q[i * BLOCK:(i + 1) * BLOCK]
        for slot, (ks, vs, cks, cvs) in enumerate(zip(
                _head_slots(k, kv_head, lo), _head_slots(v, kv_head, lo),
                _head_slots(ck, kv_head, lo_c), _head_slots(cv_ref[0, 0], kv_head, lo_c))):
            ks, vs, cks, cvs = (a.astype(BF16) for a in (ks, vs, cks, cvs))
            ones = _slot_ones(BLOCK, slot)
            for j in range(nb):
                rows = slice(j * grp + slot * BLOCK, j * grp + (slot + 1) * BLOCK)
                k_scr[kv_head, rows] = ks[j * BLOCK:(j + 1) * BLOCK]
                v_scr[kv_head, rows, :LANES] = vs[j * BLOCK:(j + 1) * BLOCK]
                v_scr[kv_head, rows, LANES:] = ones
            for j in range(cb):
                rows = slice(j * grp + slot * BLOCK, j * grp + (slot + 1) * BLOCK)
                ck_scr[kv_head, rows] = cks[j * BLOCK:(j + 1) * BLOCK]
                cv_scr[kv_head, rows, :LANES] = cvs[j * BLOCK:(j + 1) * BLOCK]
                cv_scr[kv_head, rows, LANES:] = ones

    qi = lax.broadcasted_iota(jnp.int32, (grp, BLOCK), 0) % BLOCK
    kj = lax.broadcasted_iota(jnp.int32, (grp, BLOCK), 1)
    keep_prev = kj >= qi
    keep_next = kj <= qi
    top = lax.broadcasted_iota(jnp.int32, (grp, 1), 0) < BLOCK
    lo_g = lax.broadcasted_iota(jnp.int32, (grp, LANES), 1) < HEAD_DIM

    def lane_tiles(s):
        return [s[:, c * BLOCK:(c + 1) * BLOCK] for c in range(s.shape[1] // BLOCK)]

    for kv_head in range(N_KV_HEADS):
        h0 = 4 * kv_head
        sinks = [jnp.where(top, sink_ref[h0 + slot], sink_ref[h0 + 2 + slot]) * LOG2E for slot in range(2)]
        for i in range(nb):
            j0, j1 = max(i - 1, 0), min(i + 2, nb)
            q2 = q_scr[kv_head, i * grp:(i + 1) * grp]
            tiles = lane_tiles(_dot_nt(q2, k_scr[kv_head, j0 * grp:j1 * grp]))
            for b, j in enumerate(range(j0, j1)):
                for slot in range(2):
                    if j == i - 1:
                        tiles[2 * b + slot] = jnp.where(keep_prev, tiles[2 * b + slot], NEG_BIG)
                    elif j == i + 1:
                        tiles[2 * b + slot] = jnp.where(keep_next, tiles[2 * b + slot], NEG_BIG)
            tiles += lane_tiles(_dot_nt(q2, ck_scr[kv_head]))
            sink_term = []
            for slot in range(2):
                mine = tiles[slot::2]
                m = jnp.maximum(jnp.max(functools.reduce(jnp.maximum, mine), axis=-1, keepdims=True),
                                sinks[slot])
                tiles[slot::2] = [jnp.exp2(tl - m).astype(BF16) for tl in mine]
                sink_term.append(jnp.exp2(sinks[slot] - m))
            n_loc = 2 * (j1 - j0)
            e_loc = jnp.concatenate(tiles[:n_loc], axis=1)
            e_ctx = jnp.concatenate(tiles[n_loc:], axis=1)
            o = _dot(e_loc, v_scr[kv_head, j0 * grp:j1 * grp]) + _dot(e_ctx, cv_scr[kv_head])
            o = o[:, :LANES] / (o[:, LANES:] + jnp.where(lo_g, sink_term[0], sink_term[1]))
            o = o.astype(o_ref.dtype)
            for t in range(2):
                tile = 2 * kv_head + t
                o_ref[0, i * BLOCK:(i + 1) * BLOCK, tile * LANES:(tile + 1) * LANES] = o[t * BLOCK:(t + 1) * BLOCK]


def _lat_attention(q, k, v, b, first, n, sink, cache_k, cache_v, layer, cos, sin):
    c_len = cache_k.shape[2]
    return pl.pallas_call(
        functools.partial(_lat_attn_kernel, n=n, c_len=c_len),
        grid=(b,),
        in_specs=[
            pl.BlockSpec(memory_space=pltpu.SMEM),
            pl.BlockSpec((1, n, ATTN_WIDTH), lambda i: (i + first, 0, 0)),
            pl.BlockSpec((1, n, KV_WIDTH), lambda i: (i + first, 0, 0)),
            pl.BlockSpec((1, n, KV_WIDTH), lambda i: (i + first, 0, 0)),
            pl.BlockSpec((1, 1, c_len, KV_WIDTH), lambda i: (i, layer, 0, 0)),
            pl.BlockSpec((1, 1, c_len, KV_WIDTH), lambda i: (i, layer, 0, 0)),
            _resident((n, LANES), lambda i: (0, 0)),
            _resident((n, LANES), lambda i: (0, 0)),
        ],
        out_specs=pl.BlockSpec((1, n, ATTN_WIDTH), lambda i: (i, 0, 0)),
        out_shape=jax.ShapeDtypeStruct((b, n, ATTN_WIDTH), BF16),
        scratch_shapes=[
            pltpu.VMEM((N_KV_HEADS, 2 * n, LANES), BF16),
            pltpu.VMEM((N_KV_HEADS, 2 * n, LANES), BF16),
            pltpu.VMEM((N_KV_HEADS, 2 * n, 2 * LANES), BF16),
            pltpu.VMEM((N_KV_HEADS, 2 * c_len, LANES), BF16),
            pltpu.VMEM((N_KV_HEADS, 2 * c_len, 2 * LANES), BF16),
        ],
        compiler_params=_cparams("parallel"),
        name="latent_attention",
    )(sink, q, k, v, cache_k, cache_v, cos, sin)


def kernel(x_prompt, x_sample, cache_k, cache_v, c, c_ctx, w_mod, b_mod, norm_g, ffn_w_gate, ffn_w_up,
           ffn_w_down, w_in, w_out, hyena_conv_w, hyena_f_w1, hyena_f_b1, hyena_f_w2, hyena_f_b2,
           hyena_f_w3, hyena_f_freq, hyena_decay, hyena_bias, attn_sink):
    batch, seq, d = x_prompt.shape
    dec_batch, dec_seq, _ = x_sample.shape
    past_len = cache_k.shape[2]

    cvecs = jnp.concatenate([c_ctx[None], c], axis=0)
    cvecs = jnp.pad(cvecs, ((0, MOD_ROWS - cvecs.shape[0]), (0, 0)))
    mod = _modulation(cvecs, w_mod, b_mod).reshape(DEPTH, MOD_ROWS, N_SUB, 3, d)

    wg, wu, wd = ffn_w_gate, ffn_w_up, ffn_w_down
    w_in_b = w_in.astype(BF16)
    w_out_b = w_out.astype(BF16)
    ck = cache_k.reshape(dec_batch, DEPTH, past_len, KV_WIDTH)
    cv = cache_v.reshape(dec_batch, DEPTH, past_len, KV_WIDTH)
    rope_cos, rope_sin = (jnp.asarray(t) for t in _rope_tables(dec_seq))

    tables = {}
    for n in (seq, dec_seq):
        fw, iw = (jnp.asarray(t).astype(BF16) for t in _hyena_dft(n))
        pos, chan = (jnp.asarray(t).astype(BF16) for t in _fourier_tables(n))
        tables[n] = (fw, iw, pos, chan)

    ctx_rows, lat_rows = batch * seq, dec_batch * dec_seq
    rows = ctx_rows + lat_rows
    sizes = (ctx_rows, lat_rows, dec_seq)
    xs = (x_prompt.reshape(ctx_rows, d), x_sample.reshape(lat_rows, d))
    new_k, new_v = [], []
    for l in range(DEPTH):
        first, last = l == 0, l == DEPTH - 1
        x1, p_fh, q, k, v = _stage_a(xs, mod, norm_g, wg, wu, wd, w_in_b, l, *sizes,
                                     tm=EDGE_TILE if first else TOKEN_TILE)
        y_fh, y_att = [], []
        for n, nseq, first_seq, latent in ((seq, batch, 0, False), (dec_seq, dec_batch, ctx_rows // dec_seq, True)):
            fw, iw, pos, chan = tables[n]
            coef = _filter_spectrum(n, fw, hyena_f_w1[l], hyena_f_b1[l], hyena_f_w2[l], hyena_f_b2[l],
                                    hyena_f_w3[l], hyena_f_freq[l], hyena_decay[l])
            y = _fh_mix(p_fh.reshape(rows // n, n, FH_WIDTH), nseq, first_seq, n, FH_GROUP[latent],
                        pos, chan, fw, iw, coef, hyena_conv_w[l], hyena_bias[l])
            y_fh.append(y.reshape(nseq * n, -1))
            qs, ks, vs = (a.reshape(rows // n, n, -1) for a in (q, k, v))
            if latent:
                y = _lat_attention(qs, ks, vs, nseq, first_seq, n, attn_sink[l], ck, cv, l, rope_cos, rope_sin)
            else:
                y, k_t, v_t = _ctx_attention(qs, ks, vs, nseq, n, attn_sink[l])
                new_k.append(k_t)
                new_v.append(v_t)
            y_att.append(y.reshape(nseq * n, -1))
        xs = _stage_b(x1, y_fh, y_att, mod, norm_g, w_out_b, wg, wu, wd, l, *sizes,
                      tm=EDGE_TILE if last else TOKEN_TILE, split=last)

    def cache_layout(per_layer):
        t = jnp.stack(per_layer, axis=0).reshape(DEPTH, batch, N_KV_HEADS, HEAD_DIM, seq)
        return jnp.transpose(t, (1, 0, 4, 2, 3))

    return (xs[0].reshape(batch, seq, d), xs[1].reshape(dec_batch, dec_seq, d),
            cache_layout(new_k), cache_layout(new_v))
```

```python
import functools
import math

import numpy as np
import jax
import jax.numpy as jnp
from jax import lax
from jax.experimental import pallas as pl
from jax.experimental.pallas import tpu as pltpu

F32 = jnp.float32
BF16 = jnp.bfloat16

D_MODEL = 1024
DEPTH = 2
GRID_W = 64
HEAD_DIM = 64
N_Q_HEADS = 8
N_KV_HEADS = 2
ATTN_WIDTH = N_Q_HEADS * HEAD_DIM
KV_WIDTH = N_KV_HEADS * HEAD_DIM
FOURIER_WIDTH = 256
FOURIER_GROUPS = 4
FOURIER_GROUP_DIM = 64
HYENA_WIDTH = 256
HYENA_ORDER = 2
HYENA_PROJ = 3 * HYENA_WIDTH
HYENA_EMB_DIM = 33
HYENA_BANDS = 16
HYENA_FILTER_WIDTH = 64
FH_WIDTH = FOURIER_WIDTH + HYENA_PROJ
ATT_WIDTH = ATTN_WIDTH + 2 * KV_WIDTH
IN_WIDTH = FH_WIDTH + ATT_WIDTH
BLOCK = 128
WINDOW = 128
ROPE_BASE = 10000.0
D_FF = 2816
N_SUB = 3
RMS_EPS = 1e-6

LANES = 128
MOD_ROWS = 16
VMEM_LIMIT = 56 * 1024 * 1024
NEG_BIG = -1e30
LOG2E = 1.4426950408889634

TOKEN_TILE = 512
ROW_GROUP = 256
MOD_TILE = 3072
FH_GROUP = {False: 8, True: 2}
CTX_ATTN_GROUP = 4


def _cparams(*sem):
    return pltpu.CompilerParams(dimension_semantics=sem, vmem_limit_bytes=VMEM_LIMIT)


def _rms(x, g):
    return x * lax.rsqrt(jnp.mean(x * x, axis=-1, keepdims=True) + RMS_EPS) * g


def _dot(a, b):
    return jnp.dot(a, b, preferred_element_type=F32)


def _dot_nt(a, b):
    return lax.dot_general(a, b, (((1,), (1,)), ((), ())), preferred_element_type=F32)


@functools.lru_cache(maxsize=None)
def _hyena_dft(n):
    f = np.arange(n, dtype=np.int64)[:, None]
    s = np.arange(n, dtype=np.int64)[None, :]
    ang = np.pi * ((f * s) % (2 * n)).astype(np.float64) / n
    c = np.cos(ang)
    sn = np.sin(ang)
    sn[0, :] = 1.0 - 2.0 * (np.arange(n) % 2)
    fw = np.concatenate([c, sn], axis=0).astype(np.float32)
    iw = np.concatenate([c, sn.T], axis=1).astype(np.float32)
    return fw, iw


@functools.lru_cache(maxsize=None)
def _fourier_tables(n):
    f = np.arange(n, dtype=np.int64)[:, None]
    s = np.arange(n, dtype=np.int64)[None, :]
    ang = 2.0 * np.pi * ((f * s) % n).astype(np.float64) / n
    pos = np.concatenate([np.cos(ang), -np.sin(ang)], axis=1) / math.sqrt(n)
    gd = FOURIER_GROUP_DIM
    a = np.arange(gd, dtype=np.int64)
    ang_c = 2.0 * np.pi * ((a[:, None] * a[None, :]) % gd).astype(np.float64) / gd
    bc = np.kron(np.eye(FOURIER_GROUPS), np.cos(ang_c)) / math.sqrt(gd)
    bs = np.kron(np.eye(FOURIER_GROUPS), np.sin(ang_c)) / math.sqrt(gd)
    chan = np.concatenate([bc, bs], axis=1)
    return pos.astype(np.float32), chan.astype(np.float32)


@functools.lru_cache(maxsize=None)
def _rope_tables(n):
    half = HEAD_DIM // 2
    inv = ROPE_BASE ** (-np.arange(0, half, 2, dtype=np.float64) / half)
    t = np.arange(n)
    row = (t // GRID_W).astype(np.float64)
    col = (t % GRID_W).astype(np.float64)
    lane = np.arange(LANES)
    d = lane % HEAD_DIM
    pos = np.where((d // half)[None, :] == 0, row[:, None], col[:, None])
    ang = pos * inv[d % (half // 2)][None, :]
    sign = np.where((d % half) < half // 2, -1.0, 1.0)[None, :]
    return np.cos(ang).astype(np.float32), (np.sin(ang) * sign).astype(np.float32)


def _hyena_feats(n):
    d = jnp.arange(n, dtype=F32)
    t = jnp.linspace(0.0, 1.0, n, dtype=F32)[:, None]
    f = jnp.linspace(1e-4, HYENA_BANDS - 1, HYENA_BANDS, dtype=F32)
    ang = (2.0 * math.pi / n) * d[:, None] * f[None, :]
    feats = jnp.concatenate([t, jnp.cos(ang), -jnp.sin(ang)], axis=-1)
    return jnp.pad(feats, ((0, 0), (0, LANES - HYENA_EMB_DIM))), t


def _mod_kernel(c_ref, w_ref, b_ref, o_ref):
    c = c_ref[...]
    s = c / (1.0 + jnp.exp(-c))
    o_ref[0] = _dot(s.astype(BF16), w_ref[0].astype(BF16)) + b_ref[0]


def _modulation(cvecs, w_mod, b_mod):
    depth, _, width = w_mod.shape
    tn = MOD_TILE
    return pl.pallas_call(
        _mod_kernel,
        grid=(depth, width // tn),
        in_specs=[
            pl.BlockSpec((MOD_ROWS, D_MODEL), lambda l, j: (0, 0)),
            pl.BlockSpec((1, D_MODEL, tn), lambda l, j: (l, 0, j)),
            pl.BlockSpec((1, 1, tn), lambda l, j: (l, 0, j)),
        ],
        out_specs=pl.BlockSpec((1, MOD_ROWS, tn), lambda l, j: (l, 0, j)),
        out_shape=jax.ShapeDtypeStruct((depth, MOD_ROWS, width), F32),
        compiler_params=_cparams("arbitrary", "arbitrary"),
        name="modulation",
    )(cvecs, w_mod, b_mod.reshape(depth, 1, width))


def _resident(block_shape, index_map):
    return pl.BlockSpec(block_shape, index_map, pipeline_mode=pl.Buffered(1))


def _modulate(x, g_row, mod_ref, sub):
    return (_rms(x, g_row) * (1.0 + mod_ref[0, sub, 1:2]) + mod_ref[0, sub, 0:1]).astype(BF16)


def _swiglu_rows(x, h, mod_ref, sub, g_ref, wg_ref, wu_ref, wd_ref):
    gate = _dot(h, wg_ref[...])
    up = _dot(h, wu_ref[...])
    act = (gate / (1.0 + jnp.exp(-gate))) * up
    y = _dot(act.astype(BF16), wd_ref[...])
    return x + (0.5 * mod_ref[0, sub, 2:3]) * _rms(y, g_ref[2 * sub + 1:2 * sub + 2])


def _software_pipeline(rows, phases):
    groups = [slice(r, r + ROW_GROUP) for r in range(0, rows, ROW_GROUP)]
    state = {}
    for turn in range(len(groups) + len(phases) - 1):
        for p, phase in enumerate(phases):
            g = turn - p
            if 0 <= g < len(groups):
                state[g] = phase(groups[g], state.get(g))


def _stage_a_kernel(*refs, n_in, ctx_tiles):
    x_refs = refs[:n_in]
    mod_ref, g_ref, wg_ref, wu_ref, wd_ref, win_ref, x1_ref, fh_ref, q_ref, k_ref, v_ref = refs[n_in:]
    is_ctx = pl.program_id(0) < ctx_tiles
    k0 = FH_WIDTH + ATTN_WIDTH

    def modulated_input(sl, _):
        x = x_refs[0][sl] if n_in == 1 else jnp.where(is_ctx, x_refs[0][sl], x_refs[1][sl])
        return x, _modulate(x, g_ref[0:1], mod_ref, 0)

    def half_step(sl, xh):
        x1 = _swiglu_rows(*xh, mod_ref, 0, g_ref, wg_ref, wu_ref, wd_ref)
        x1_ref[sl] = x1
        return _modulate(x1, g_ref[2:3], mod_ref, 1)

    def in_projection(sl, h):
        p = _dot(h, win_ref[...])
        fh_ref[sl] = p[:, :FH_WIDTH].astype(fh_ref.dtype)
        q_ref[sl] = p[:, FH_WIDTH:k0].astype(q_ref.dtype)
        k_ref[sl] = p[:, k0:k0 + KV_WIDTH]
        v_ref[sl] = p[:, k0 + KV_WIDTH:]

    _software_pipeline(x1_ref.shape[0], [modulated_input, half_step, in_projection])


def _stage_b_kernel(x_ref, fhc_ref, fhl_ref, atc_ref, atl_ref, mod_ref, g_ref, wout_ref, wg_ref, wu_ref,
                    wd_ref, *o_refs, ctx_tiles):
    is_ctx = pl.program_id(0) < ctx_tiles
    half = FOURIER_WIDTH + HYENA_WIDTH

    def body(mixed, o_ref):
        def out_projection(sl, _):
            y_fh, y_at = mixed(sl)
            y = _dot(y_fh, wout_ref[:half]) + _dot(y_at, wout_ref[half:])
            x2 = x_ref[sl] + mod_ref[0, 1, 2:3] * _rms(y, g_ref[3:4])
            return x2, _modulate(x2, g_ref[4:5], mod_ref, 2)

        def half_step(sl, xh):
            o_ref[sl] = _swiglu_rows(*xh, mod_ref, 2, g_ref, wg_ref, wu_ref, wd_ref)

        _software_pipeline(x_ref.shape[0], [out_projection, half_step])

    if len(o_refs) == 1:
        body(lambda sl: (jnp.where(is_ctx, fhc_ref[sl], fhl_ref[sl]),
                         jnp.where(is_ctx, atc_ref[sl], atl_ref[sl])), o_refs[0])
    else:
        pl.when(is_ctx)(lambda: body(lambda sl: (fhc_ref[sl], atc_ref[sl]), o_refs[0]))
        pl.when(jnp.logical_not(is_ctx))(lambda: body(lambda sl: (fhl_ref[sl], atl_ref[sl]), o_refs[1]))


def _token_specs(tm, ctx_rows, lat_rows, lat_seq):
    ctx_tiles = ctx_rows // tm
    tiles_per_seq = lat_seq // tm
    tile = lambda width: pl.BlockSpec((tm, width), lambda i: (i, 0))
    ctx_tile = lambda width: pl.BlockSpec((tm, width), lambda i: (jnp.minimum(i, ctx_tiles - 1), 0))
    lat_tile = lambda width: pl.BlockSpec((tm, width), lambda i: (jnp.maximum(i - ctx_tiles, 0), 0))
    request = lambda i: jnp.where(i < ctx_tiles, 0, 1 + (i - ctx_tiles) // tiles_per_seq)
    return ctx_tiles, (ctx_rows + lat_rows) // tm, tile, ctx_tile, lat_tile, request


def _stage_a(xs, mod, norm_g, wg, wu, wd, w_in, layer, ctx_rows, lat_rows, lat_seq, tm):
    ctx_tiles, tiles, tile, ctx_tile, lat_tile, request = _token_specs(tm, ctx_rows, lat_rows, lat_seq)
    t = ctx_rows + lat_rows
    widths = (D_MODEL, FH_WIDTH, ATTN_WIDTH, KV_WIDTH, KV_WIDTH)
    dtypes = (F32, BF16, BF16, F32, F32)
    x_specs = [tile(D_MODEL)] if len(xs) == 1 else [ctx_tile(D_MODEL), lat_tile(D_MODEL)]
    return pl.pallas_call(
        functools.partial(_stage_a_kernel, n_in=len(xs), ctx_tiles=ctx_tiles),
        grid=(tiles,),
        in_specs=x_specs + [
            pl.BlockSpec((None, 1, N_SUB, 3, D_MODEL), lambda i: (layer, request(i), 0, 0, 0)),
            _resident((None, 2 * N_SUB, D_MODEL), lambda i: (layer, 0, 0)),
            _resident((None, None, D_MODEL, D_FF), lambda i: (layer, 0, 0, 0)),
            _resident((None, None, D_MODEL, D_FF), lambda i: (layer, 0, 0, 0)),
            _resident((None, None, D_FF, D_MODEL), lambda i: (layer, 0, 0, 0)),
            _resident((None, D_MODEL, IN_WIDTH), lambda i: (layer, 0, 0)),
        ],
        out_specs=[tile(w) for w in widths],
        out_shape=[jax.ShapeDtypeStruct((t, w), dt) for w, dt in zip(widths, dtypes)],
        compiler_params=_cparams("parallel"),
        name="stage_a",
    )(*xs, mod, norm_g, wg, wu, wd, w_in)


def _stage_b(x1, y_fh, y_att, mod, norm_g, w_out, wg, wu, wd, layer, ctx_rows, lat_rows, lat_seq, tm, split):
    ctx_tiles, tiles, tile, ctx_tile, lat_tile, request = _token_specs(tm, ctx_rows, lat_rows, lat_seq)
    half = FOURIER_WIDTH + HYENA_WIDTH
    if split:
        out_specs = [ctx_tile(D_MODEL), lat_tile(D_MODEL)]
        out_shape = [jax.ShapeDtypeStruct((r, D_MODEL), F32) for r in (ctx_rows, lat_rows)]
    else:
        out_specs = [tile(D_MODEL)]
        out_shape = [jax.ShapeDtypeStruct((ctx_rows + lat_rows, D_MODEL), F32)]
    return pl.pallas_call(
        functools.partial(_stage_b_kernel, ctx_tiles=ctx_tiles),
        grid=(tiles,),
        in_specs=[
            tile(D_MODEL), ctx_tile(half), lat_tile(half), ctx_tile(ATTN_WIDTH), lat_tile(ATTN_WIDTH),
            pl.BlockSpec((None, 1, N_SUB, 3, D_MODEL), lambda i: (layer, request(i), 0, 0, 0)),
            _resident((None, 2 * N_SUB, D_MODEL), lambda i: (layer, 0, 0)),
            _resident((None, D_MODEL, D_MODEL), lambda i: (layer, 0, 0)),
            _resident((None, None, D_MODEL, D_FF), lambda i: (layer, 1, 0, 0)),
            _resident((None, None, D_MODEL, D_FF), lambda i: (layer, 1, 0, 0)),
            _resident((None, None, D_FF, D_MODEL), lambda i: (layer, 1, 0, 0)),
        ],
        out_specs=out_specs,
        out_shape=out_shape,
        compiler_params=_cparams("arbitrary" if split else "parallel"),
        name="stage_b",
    )(x1, y_fh[0], y_fh[1], y_att[0], y_att[1], mod, norm_g, w_out, wg, wu, wd)


def _filter_kernel(feats_ref, t_ref, w1_ref, b1_ref, w2_ref, b2_ref, w3_ref, fr_ref, decay_ref,
                   fw_ref, o_ref, *, n):
    fr = fr_ref[...]
    h = jnp.sin(fr * (_dot(feats_ref[...], w1_ref[...]) + b1_ref[...]))
    h = jnp.sin(fr * (_dot(h, w2_ref[...]) + b2_ref[...]))
    h = _dot(h, w3_ref[...])
    window = jnp.exp(-t_ref[...] * jnp.abs(decay_ref[...]))
    width = HYENA_ORDER * HYENA_WIDTH
    row = lax.broadcasted_iota(jnp.int32, (n, width), 0)
    fwd = h[:, :width] * window
    bwd = jnp.where(row == 0, 0.0, h[:, width:] * window)
    even = fwd + bwd
    odd = fwd - bwd
    k_re = _dot(fw_ref[:n], even.astype(BF16))
    k_sn = _dot(fw_ref[n:], odd.astype(BF16))
    sign = (1 - 2 * (row % 2)).astype(F32)
    k_ny = jnp.sum(even * sign, axis=0, keepdims=True)
    s0 = 1.0 / (4.0 * n * n)
    scale = jnp.where(row == 0, s0, 2.0 * s0)
    a = scale * k_re
    o_ref[0] = a
    o_ref[1] = jnp.where(row == 0, 0.0, scale * k_sn)
    o_ref[2] = jnp.where(row == 0, s0 * k_ny, a)


def _filter_spectrum(n, fw, w1, b1, w2, b2, w3, freq, decay):
    feats, t = _hyena_feats(n)
    pad_w = LANES - HYENA_FILTER_WIDTH
    w1p = jnp.pad(w1, ((0, LANES - HYENA_EMB_DIM), (0, pad_w)))
    w2p = jnp.pad(w2, ((0, pad_w), (0, pad_w)))
    w3p = jnp.pad(w3, ((0, pad_w), (0, 0)))
    padv = lambda v: jnp.pad(v, (0, pad_w)).reshape(1, LANES)
    width = HYENA_ORDER * HYENA_WIDTH
    args = (feats, t, w1p, padv(b1), w2p, padv(b2), w3p, padv(freq), decay.reshape(1, width), fw)
    return pl.pallas_call(
        functools.partial(_filter_kernel, n=n),
        out_shape=jax.ShapeDtypeStruct((3, n, width), F32),
        compiler_params=pltpu.CompilerParams(vmem_limit_bytes=VMEM_LIMIT),
        name="hyena_filter",
    )(*args)


def _fh_kernel(p_ref, pos_ref, chan_ref, fw_ref, iw_ref, coef_ref, convw_ref, bias_ref, o_ref,
               st_scr, *, n, group):
    c = HYENA_WIDTH
    cols = [slice(b * c, (b + 1) * c) for b in range(group)]

    for b in range(group):
        t = _dot(p_ref[b, :, :FOURIER_WIDTH], chan_ref[...])
        st_scr[:n, cols[b]] = t[:, :FOURIER_WIDTH].astype(BF16)
        st_scr[n:, cols[b]] = t[:, FOURIER_WIDTH:].astype(BF16)
    y_f = _dot(pos_ref[...], st_scr[...])
    for b in range(group):
        o_ref[b, :, :FOURIER_WIDTH] = y_f[:, cols[b]].astype(o_ref.dtype)

    row = lax.broadcasted_iota(jnp.int32, (n, HYENA_PROJ), 0)
    zs = []
    for b in range(group):
        z = p_ref[b, :, FOURIER_WIDTH:].astype(F32)
        z_prev = jnp.where(row == 0, 0.0, pltpu.roll(z, 1, axis=0))
        z_next = jnp.where(row == n - 1, 0.0, pltpu.roll(z, n - 1, axis=0))
        zs.append(z_prev * convw_ref[0:1] + z * convw_ref[1:2] + z_next * convw_ref[2:3])

    def long_conv(vs, order):
        oc = slice(order * c, (order + 1) * c)
        uf = _dot(fw_ref[...], jnp.concatenate([v.astype(BF16) for v in vs], axis=1))
        a, bb, a2 = coef_ref[0, :, oc], coef_ref[1, :, oc], coef_ref[2, :, oc]
        for b in range(group):
            u_re, u_sn = uf[:n, cols[b]], uf[n:, cols[b]]
            st_scr[:n, cols[b]] = (u_re * a - u_sn * bb).astype(BF16)
            st_scr[n:, cols[b]] = (u_re * bb + u_sn * a2).astype(BF16)
        y = _dot(iw_ref[...], st_scr[...])
        return [y[:, cols[b]] + vs[b] * bias_ref[order:order + 1] for b in range(group)]

    ys = long_conv([z[:, :c] for z in zs], 0)
    ys = long_conv([z[:, c:2 * c] * y for z, y in zip(zs, ys)], 1)
    for b in range(group):
        o_ref[b, :, FOURIER_WIDTH:] = (zs[b][:, 2 * c:] * ys[b]).astype(o_ref.dtype)


def _fh_mix(p_fh, b, first, n, group, pos, chan, fw, iw, coef, conv_w, hbias):
    width = FOURIER_WIDTH + HYENA_WIDTH
    const = lambda shape: _resident(shape, lambda i: (0,) * len(shape))
    return pl.pallas_call(
        functools.partial(_fh_kernel, n=n, group=group),
        grid=(b // group,),
        in_specs=[
            pl.BlockSpec((group, n, FH_WIDTH), lambda i: (i + first // group, 0, 0)),
            const((n, 2 * n)), const((FOURIER_WIDTH, 2 * FOURIER_WIDTH)),
            const((2 * n, n)), const((n, 2 * n)),
            const((3, n, HYENA_ORDER * HYENA_WIDTH)),
            const((3, HYENA_PROJ)), const((HYENA_ORDER, HYENA_WIDTH)),
        ],
        out_specs=pl.BlockSpec((group, n, width), lambda i: (i, 0, 0)),
        out_shape=jax.ShapeDtypeStruct((b, n, width), BF16),
        scratch_shapes=[pltpu.VMEM((2 * n, group * HYENA_WIDTH), BF16)],
        compiler_params=_cparams("parallel"),
        name="fourier_hyena",
    )(p_fh, pos, chan, fw, iw, coef, conv_w, hbias)


def _head_slots(x, kv_head, lo):
    xr = pltpu.roll(x, HEAD_DIM, axis=1)
    if kv_head == 0:
        return jnp.where(lo, x, 0.0), jnp.where(lo, 0.0, xr)
    return jnp.where(lo, xr, 0.0), jnp.where(lo, 0.0, x)


def _slot_ones(rows, slot):
    lo = lax.broadcasted_iota(jnp.int32, (rows, LANES), 1) < HEAD_DIM
    return jnp.where(lo if slot == 0 else jnp.logical_not(lo), 1.0, 0.0).astype(BF16)


def _ctx_attn_kernel(sink_ref, q_ref, k_ref, v_ref, o_ref, kt_ref, vt_ref, *, n, group):
    lo = lax.broadcasted_iota(jnp.int32, (n, LANES), 1) < HEAD_DIM
    top = lax.broadcasted_iota(jnp.int32, (2 * n, 1), 0) < n
    lo_g = lax.broadcasted_iota(jnp.int32, (2 * n, LANES), 1) < HEAD_DIM
    ones = jnp.concatenate([_slot_ones(n, 0), _slot_ones(n, 1)], axis=0)
    for b in range(group):
        kt_ref[b] = k_ref[b].T
        vt_ref[b] = v_ref[b].T
        k = k_ref[b] * LOG2E
        v = v_ref[b]
        for kv_head in range(N_KV_HEADS):
            h0 = 4 * kv_head
            k_cat = jnp.concatenate(_head_slots(k, kv_head, lo), axis=0).astype(BF16)
            v_cat = jnp.concatenate(_head_slots(v, kv_head, lo), axis=0).astype(BF16)
            v_cat = jnp.concatenate([v_cat, ones], axis=1)
            q2 = jnp.concatenate([q_ref[b, :, (2 * kv_head + t) * LANES:(2 * kv_head + t + 1) * LANES]
                                  for t in range(2)], axis=0) * HEAD_DIM ** -0.5
            s = _dot_nt(q2, k_cat)
            e, sink_term = [], []
            for slot in range(2):
                sink = jnp.where(top, sink_ref[h0 + slot], sink_ref[h0 + 2 + slot]) * LOG2E
                cols = s[:, slot * n:(slot + 1) * n]
                m = jnp.maximum(jnp.max(cols, axis=-1, keepdims=True), sink)
                e.append(jnp.exp2(cols - m).astype(BF16))
                sink_term.append(jnp.exp2(sink - m))
            o = _dot(jnp.concatenate(e, axis=1), v_cat)
            o = o[:, :LANES] / (o[:, LANES:] + jnp.where(lo_g, sink_term[0], sink_term[1]))
            for t in range(2):
                tile = 2 * kv_head + t
                o_ref[b, :, tile * LANES:(tile + 1) * LANES] = o[t * n:(t + 1) * n].astype(o_ref.dtype)


def _ctx_attention(q, k, v, b, n, sink):
    group = CTX_ATTN_GROUP
    return pl.pallas_call(
        functools.partial(_ctx_attn_kernel, n=n, group=group),
        grid=(b // group,),
        in_specs=[
            pl.BlockSpec(memory_space=pltpu.SMEM),
            pl.BlockSpec((group, n, ATTN_WIDTH), lambda i: (i, 0, 0)),
            pl.BlockSpec((group, n, KV_WIDTH), lambda i: (i, 0, 0)),
            pl.BlockSpec((group, n, KV_WIDTH), lambda i: (i, 0, 0)),
        ],
        out_specs=[pl.BlockSpec((group, n, ATTN_WIDTH), lambda i: (i, 0, 0)),
                   pl.BlockSpec((group, KV_WIDTH, n), lambda i: (i, 0, 0)),
                   pl.BlockSpec((group, KV_WIDTH, n), lambda i: (i, 0, 0))],
        out_shape=[jax.ShapeDtypeStruct((b, n, ATTN_WIDTH), BF16),
                   jax.ShapeDtypeStruct((b, KV_WIDTH, n), F32),
                   jax.ShapeDtypeStruct((b, KV_WIDTH, n), F32)],
        compiler_params=_cparams("parallel"),
        name="ctx_attention",
    )(sink, q, k, v)


def _rope(x, cos, sin_signed):
    lane = lax.broadcasted_iota(jnp.int32, x.shape, 1)
    first = (lane % (HEAD_DIM // 2)) < HEAD_DIM // 4
    partner = jnp.where(first, pltpu.roll(x, LANES - HEAD_DIM // 4, axis=1),
                        pltpu.roll(x, HEAD_DIM // 4, axis=1))
    return x * cos + partner * sin_signed


def _lat_attn_kernel(sink_ref, q_ref, k_ref, v_ref, ck_ref, cv_ref, cos_ref, sin_ref, o_ref,
                     q_scr, k_scr, v_scr, ck_scr, cv_scr, *, n, c_len):
    nb = n // BLOCK
    cb = c_len // BLOCK
    grp = 2 * BLOCK
    cos, sin = cos_ref[...], sin_ref[...]
    lo = lax.broadcasted_iota(jnp.int32, (n, LANES), 1) < HEAD_DIM
    lo_c = lax.broadcasted_iota(jnp.int32, (c_len, LANES), 1) < HEAD_DIM

    k = _rope(k_ref[0], cos, sin) * LOG2E
    v = v_ref[0]
    ck = ck_ref[0, 0] * LOG2E
    for kv_head in range(N_KV_HEADS):
        for t in range(2):
            tile = 2 * kv_head + t
            q = q_ref[0, :, tile * LANES:(tile + 1) * LANES].astype(F32)
            q = (_rope(q, cos, sin) * HEAD_DIM ** -0.5).astype(BF16)
            for i in range(nb):
                q_scr[kv_head, i * grp + t * BLOCK:i * grp + (t + 1) * BLOCK] = q[i * BLOCK:(i + 1) * BLOCK]
        for slot, (ks, vs, cks, cvs) in enumerate(zip(
                _head_slots(k, kv_head, lo), _head_slots(v, kv_head, lo),
                _head_slots(ck, kv_head, lo_c), _head_slots(cv_ref[0, 0], kv_head, lo_c))):
            ks, vs, cks, cvs = (a.astype(BF16) for a in (ks, vs, cks, cvs))
            ones = _slot_ones(BLOCK, slot)
            for j in range(nb):
                rows = slice(j * grp + slot * BLOCK, j * grp + (slot + 1) * BLOCK)
                k_scr[kv_head, rows] = ks[j * BLOCK:(j + 1) * BLOCK]
                v_scr[kv_head, rows, :LANES] = vs[j * BLOCK:(j + 1) * BLOCK]
                v_scr[kv_head, rows, LANES:] = ones
            for j in range(cb):
                rows = slice(j * grp + slot * BLOCK, j * grp + (slot + 1) * BLOCK)
                ck_scr[kv_head, rows] = cks[j * BLOCK:(j + 1) * BLOCK]
                cv_scr[kv_head, rows, :LANES] = cvs[j * BLOCK:(j + 1) * BLOCK]
                cv_scr[kv_head, rows, LANES:] = ones

    qi = lax.broadcasted_iota(jnp.int32, (grp, BLOCK), 0) % BLOCK
    kj = lax.broadcasted_iota(jnp.int32, (grp, BLOCK), 1)
    keep_prev = kj >= qi
    keep_next = kj <= qi
    top = lax.broadcasted_iota(jnp.int32, (grp, 1), 0) < BLOCK
    lo_g = lax.broadcasted_iota(jnp.int32, (grp, LANES), 1) < HEAD_DIM

    def lane_tiles(s):
        return [s[:, c * BLOCK:(c + 1) * BLOCK] for c in range(s.shape[1] // BLOCK)]

    for kv_head in range(N_KV_HEADS):
        h0 = 4 * kv_head
        sinks = [jnp.where(top, sink_ref[h0 + slot], sink_ref[h0 + 2 + slot]) * LOG2E for slot in range(2)]
        for i in range(nb):
            j0, j1 = max(i - 1, 0), min(i + 2, nb)
            q2 = q_scr[kv_head, i * grp:(i + 1) * grp]
            tiles = lane_tiles(_dot_nt(q2, k_scr[kv_head, j0 * grp:j1 * grp]))
            for b, j in enumerate(range(j0, j1)):
                for slot in range(2):
                    if j == i - 1:
                        tiles[2 * b + slot] = jnp.where(keep_prev, tiles[2 * b + slot], NEG_BIG)
                    elif j == i + 1:
                        tiles[2 * b + slot] = jnp.where(keep_next, tiles[2 * b + slot], NEG_BIG)
            tiles += lane_tiles(_dot_nt(q2, ck_scr[kv_head]))
            sink_term = []
            for slot in range(2):
                mine = tiles[slot::2]
                m = jnp.maximum(jnp.max(functools.reduce(jnp.maximum, mine), axis=-1, keepdims=True),
                                sinks[slot])
                tiles[slot::2] = [jnp.exp2(tl - m).astype(BF16) for tl in mine]
                sink_term.append(jnp.exp2(sinks[slot] - m))
            n_loc = 2 * (j1 - j0)
            e_loc = jnp.concatenate(tiles[:n_loc], axis=1)
            e_ctx = jnp.concatenate(tiles[n_loc:], axis=1)
            o = _dot(e_loc, v_scr[kv_head, j0 * grp:j1 * grp]) + _dot(e_ctx, cv_scr[kv_head])
            o = o[:, :LANES] / (o[:, LANES:] + jnp.where(lo_g, sink_term[0], sink_term[1]))
            o = o.astype(o_ref.dtype)
            for t in range(2):
                tile = 2 * kv_head + t
                o_ref[0, i * BLOCK:(i + 1) * BLOCK, tile * LANES:(tile + 1) * LANES] = o[t * BLOCK:(t + 1) * BLOCK]


def _lat_attention(q, k, v, b, first, n, sink, cache_k, cache_v, layer, cos, sin):
    c_len = cache_k.shape[2]
    return pl.pallas_call(
        functools.partial(_lat_attn_kernel, n=n, c_len=c_len),
        grid=(b,),
        in_specs=[
            pl.BlockSpec(memory_space=pltpu.SMEM),
            pl.BlockSpec((1, n, ATTN_WIDTH), lambda i: (i + first, 0, 0)),
            pl.BlockSpec((1, n, KV_WIDTH), lambda i: (i + first, 0, 0)),
            pl.BlockSpec((1, n, KV_WIDTH), lambda i: (i + first, 0, 0)),
            pl.BlockSpec((1, 1, c_len, KV_WIDTH), lambda i: (i, layer, 0, 0)),
            pl.BlockSpec((1, 1, c_len, KV_WIDTH), lambda i: (i, layer, 0, 0)),
            _resident((n, LANES), lambda i: (0, 0)),
            _resident((n, LANES), lambda i: (0, 0)),
        ],
        out_specs=pl.BlockSpec((1, n, ATTN_WIDTH), lambda i: (i, 0, 0)),
        out_shape=jax.ShapeDtypeStruct((b, n, ATTN_WIDTH), BF16),
        scratch_shapes=[
            pltpu.VMEM((N_KV_HEADS, 2 * n, LANES), BF16),
            pltpu.VMEM((N_KV_HEADS, 2 * n, LANES), BF16),
            pltpu.VMEM((N_KV_HEADS, 2 * n, 2 * LANES), BF16),
            pltpu.VMEM((N_KV_HEADS, 2 * c_len, LANES), BF16),
            pltpu.VMEM((N_KV_HEADS, 2 * c_len, 2 * LANES), BF16),
        ],
        compiler_params=_cparams("parallel"),
        name="latent_attention",
    )(sink, q, k, v, cache_k, cache_v, cos, sin)


def kernel(x_prompt, x_sample, cache_k, cache_v, c, c_ctx, w_mod, b_mod, norm_g, ffn_w_gate, ffn_w_up,
           ffn_w_down, w_in, w_out, hyena_conv_w, hyena_f_w1, hyena_f_b1, hyena_f_w2, hyena_f_b2,
           hyena_f_w3, hyena_f_freq, hyena_decay, hyena_bias, attn_sink):
    batch, seq, d = x_prompt.shape
    dec_batch, dec_seq, _ = x_sample.shape
    past_len = cache_k.shape[2]

    cvecs = jnp.concatenate([c_ctx[None], c], axis=0)
    cvecs = jnp.pad(cvecs, ((0, MOD_ROWS - cvecs.shape[0]), (0, 0)))
    mod = _modulation(cvecs, w_mod, b_mod).reshape(DEPTH, MOD_ROWS, N_SUB, 3, d)

    wg = ffn_w_gate.astype(BF16)
    wu = ffn_w_up.astype(BF16)
    wd = ffn_w_down.astype(BF16)
    w_in_b = w_in.astype(BF16)
    w_out_b = w_out.astype(BF16)
    ck = cache_k.reshape(dec_batch, DEPTH, past_len, KV_WIDTH)
    cv = cache_v.reshape(dec_batch, DEPTH, past_len, KV_WIDTH)
    rope_cos, rope_sin = (jnp.asarray(t) for t in _rope_tables(dec_seq))

    tables = {}
    for n in (seq, dec_seq):
        fw, iw = (jnp.asarray(t).astype(BF16) for t in _hyena_dft(n))
        pos, chan = (jnp.asarray(t).astype(BF16) for t in _fourier_tables(n))
        tables[n] = (fw, iw, pos, chan)

    ctx_rows, lat_rows = batch * seq, dec_batch * dec_seq
    rows = ctx_rows + lat_rows
    sizes = (ctx_rows, lat_rows, dec_seq)
    xs = (x_prompt.reshape(ctx_rows, d), x_sample.reshape(lat_rows, d))
    new_k, new_v = [], []
    for l in range(DEPTH):
        x1, p_fh, q, k, v = _stage_a(xs, mod, norm_g, wg, wu, wd, w_in_b, l, *sizes, tm=TOKEN_TILE)
        y_fh, y_att = [], []
        for n, nseq, first_seq, latent in ((seq, batch, 0, False), (dec_seq, dec_batch, ctx_rows // dec_seq, True)):
            fw, iw, pos, chan = tables[n]
            coef = _filter_spectrum(n, fw, hyena_f_w1[l], hyena_f_b1[l], hyena_f_w2[l], hyena_f_b2[l],
                                    hyena_f_w3[l], hyena_f_freq[l], hyena_decay[l])
            y = _fh_mix(p_fh.reshape(rows // n, n, FH_WIDTH), nseq, first_seq, n, FH_GROUP[latent],
                        pos, chan, fw, iw, coef, hyena_conv_w[l], hyena_bias[l])
            y_fh.append(y.reshape(nseq * n, -1))
            qs, ks, vs = (a.reshape(rows // n, n, -1) for a in (q, k, v))
            if latent:
                y = _lat_attention(qs, ks, vs, nseq, first_seq, n, attn_sink[l], ck, cv, l, rope_cos, rope_sin)
            else:
                y, k_t, v_t = _ctx_attention(qs, ks, vs, nseq, n, attn_sink[l])
                new_k.append(k_t)
                new_v.append(v_t)
            y_att.append(y.reshape(nseq * n, -1))
        xs = _stage_b(x1, y_fh, y_att, mod, norm_g, w_out_b, wg, wu, wd, l, *sizes, tm=TOKEN_TILE,
                      split=l == DEPTH - 1)

    def cache_layout(per_layer):
        t = jnp.stack(per_layer, axis=0).reshape(DEPTH, batch, N_KV_HEADS, HEAD_DIM, seq)
        return jnp.transpose(t, (1, 0, 4, 2, 3))

    return (xs[0].reshape(batch, seq, d), xs[1].reshape(dec_batch, dec_seq, d),
            cache_layout(new_k), cache_layout(new_v))
```

```python
import functools
import math

import numpy as np
import jax
import jax.numpy as jnp
from jax import lax
from jax.experimental import pallas as pl
from jax.experimental.pallas import tpu as pltpu

F32 = jnp.float32
BF16 = jnp.bfloat16

D_MODEL = 1024
DEPTH = 2
GRID_W = 64
HEAD_DIM = 64
N_Q_HEADS = 8
N_KV_HEADS = 2
ATTN_WIDTH = N_Q_HEADS * HEAD_DIM
KV_WIDTH = N_KV_HEADS * HEAD_DIM
FOURIER_WIDTH = 256
FOURIER_GROUPS = 4
FOURIER_GROUP_DIM = 64
HYENA_WIDTH = 256
HYENA_ORDER = 2
HYENA_PROJ = 3 * HYENA_WIDTH
HYENA_EMB_DIM = 33
HYENA_BANDS = 16
HYENA_FILTER_WIDTH = 64
FH_WIDTH = FOURIER_WIDTH + HYENA_PROJ
ATT_WIDTH = ATTN_WIDTH + 2 * KV_WIDTH
IN_WIDTH = FH_WIDTH + ATT_WIDTH
BLOCK = 128
WINDOW = 128
ROPE_BASE = 10000.0
D_FF = 2816
N_SUB = 3
RMS_EPS = 1e-6

LANES = 128
MOD_ROWS = 16
VMEM_LIMIT = 56 * 1024 * 1024
NEG_BIG = -1e30
LOG2E = 1.4426950408889634

TOKEN_TILE = 512
ROW_GROUP = 256
MOD_TILE = 3072
FH_GROUP = {False: 8, True: 2}
CTX_ATTN_GROUP = 4


def _cparams(*sem):
    return pltpu.CompilerParams(dimension_semantics=sem, vmem_limit_bytes=VMEM_LIMIT)


def _rms(x, g):
    return x * lax.rsqrt(jnp.mean(x * x, axis=-1, keepdims=True) + RMS_EPS) * g


def _dot(a, b):
    return jnp.dot(a, b, preferred_element_type=F32)


def _dot_nt(a, b):
    return lax.dot_general(a, b, (((1,), (1,)), ((), ())), preferred_element_type=F32)


@functools.lru_cache(maxsize=None)
def _hyena_dft(n):
    f = np.arange(n, dtype=np.int64)[:, None]
    s = np.arange(n, dtype=np.int64)[None, :]
    ang = np.pi * ((f * s) % (2 * n)).astype(np.float64) / n
    c = np.cos(ang)
    sn = np.sin(ang)
    sn[0, :] = 1.0 - 2.0 * (np.arange(n) % 2)
    fw = np.concatenate([c, sn], axis=0).astype(np.float32)
    iw = np.concatenate([c, sn.T], axis=1).astype(np.float32)
    return fw, iw


@functools.lru_cache(maxsize=None)
def _fourier_tables(n):
    f = np.arange(n, dtype=np.int64)[:, None]
    s = np.arange(n, dtype=np.int64)[None, :]
    ang = 2.0 * np.pi * ((f * s) % n).astype(np.float64) / n
    pos = np.concatenate([np.cos(ang), -np.sin(ang)], axis=1) / math.sqrt(n)
    gd = FOURIER_GROUP_DIM
    a = np.arange(gd, dtype=np.int64)
    ang_c = 2.0 * np.pi * ((a[:, None] * a[None, :]) % gd).astype(np.float64) / gd
    bc = np.kron(np.eye(FOURIER_GROUPS), np.cos(ang_c)) / math.sqrt(gd)
    bs = np.kron(np.eye(FOURIER_GROUPS), np.sin(ang_c)) / math.sqrt(gd)
    chan = np.concatenate([bc, bs], axis=1)
    return pos.astype(np.float32), chan.astype(np.float32)


@functools.lru_cache(maxsize=None)
def _rope_tables(n):
    half = HEAD_DIM // 2
    inv = ROPE_BASE ** (-np.arange(0, half, 2, dtype=np.float64) / half)
    t = np.arange(n)
    row = (t // GRID_W).astype(np.float64)
    col = (t % GRID_W).astype(np.float64)
    lane = np.arange(LANES)
    d = lane % HEAD_DIM
    pos = np.where((d // half)[None, :] == 0, row[:, None], col[:, None])
    ang = pos * inv[d % (half // 2)][None, :]
    sign = np.where((d % half) < half // 2, -1.0, 1.0)[None, :]
    return np.cos(ang).astype(np.float32), (np.sin(ang) * sign).astype(np.float32)


def _hyena_feats(n):
    d = jnp.arange(n, dtype=F32)
    t = jnp.linspace(0.0, 1.0, n, dtype=F32)[:, None]
    f = jnp.linspace(1e-4, HYENA_BANDS - 1, HYENA_BANDS, dtype=F32)
    ang = (2.0 * math.pi / n) * d[:, None] * f[None, :]
    feats = jnp.concatenate([t, jnp.cos(ang), -jnp.sin(ang)], axis=-1)
    return jnp.pad(feats, ((0, 0), (0, LANES - HYENA_EMB_DIM))), t


def _mod_kernel(c_ref, w_ref, b_ref, o_ref):
    c = c_ref[...]
    s = c / (1.0 + jnp.exp(-c))
    o_ref[0] = _dot(s.astype(BF16), w_ref[0].astype(BF16)) + b_ref[0]


def _modulation(cvecs, w_mod, b_mod):
    depth, _, width = w_mod.shape
    tn = MOD_TILE
    return pl.pallas_call(
        _mod_kernel,
        grid=(depth, width // tn),
        in_specs=[
            pl.BlockSpec((MOD_ROWS, D_MODEL), lambda l, j: (0, 0)),
            pl.BlockSpec((1, D_MODEL, tn), lambda l, j: (l, 0, j)),
            pl.BlockSpec((1, 1, tn), lambda l, j: (l, 0, j)),
        ],
        out_specs=pl.BlockSpec((1, MOD_ROWS, tn), lambda l, j: (l, 0, j)),
        out_shape=jax.ShapeDtypeStruct((depth, MOD_ROWS, width), F32),
        compiler_params=_cparams("arbitrary", "arbitrary"),
        name="modulation",
    )(cvecs, w_mod, b_mod.reshape(depth, 1, width))


def _resident(block_shape, index_map):
    return pl.BlockSpec(block_shape, index_map, pipeline_mode=pl.Buffered(1))


def _modulate(x, g_row, mod_ref, sub):
    return (_rms(x, g_row) * (1.0 + mod_ref[0, sub, 1:2]) + mod_ref[0, sub, 0:1]).astype(BF16)


def _swiglu_rows(x, h, mod_ref, sub, g_ref, wg_ref, wu_ref, wd_ref):
    gate = _dot(h, wg_ref[...])
    up = _dot(h, wu_ref[...])
    act = (gate / (1.0 + jnp.exp(-gate))) * up
    y = _dot(act.astype(BF16), wd_ref[...])
    return x + (0.5 * mod_ref[0, sub, 2:3]) * _rms(y, g_ref[2 * sub + 1:2 * sub + 2])


def _software_pipeline(rows, phases):
    groups = [slice(r, r + ROW_GROUP) for r in range(0, rows, ROW_GROUP)]
    state = {}
    for turn in range(len(groups) + len(phases) - 1):
        for p, phase in enumerate(phases):
            g = turn - p
            if 0 <= g < len(groups):
                state[g] = phase(groups[g], state.get(g))


def _stage_weight(src_hbm, dst_ref, stage_ref, sem):
    depth, rows = stage_ref.shape[0], stage_ref.shape[1]
    n_chunks = dst_ref.shape[0] // rows

    def chunk_copy(c):
        slot = c % depth
        return pltpu.make_async_copy(src_hbm.at[pl.ds(c * rows, rows)], stage_ref.at[slot], sem.at[slot])

    for c in range(depth - 1):
        chunk_copy(c).start()
    for c in range(n_chunks):
        if c + depth - 1 < n_chunks:
            chunk_copy(c + depth - 1).start()
        chunk_copy(c).wait()
        dst_ref[c * rows:(c + 1) * rows] = stage_ref[c % depth].astype(BF16)


def _stage_ffn_weights(hbm_refs, layer, which, vmem_refs, up_stage, down_stage, sem):
    @pl.when(pl.program_id(0) == 0)
    def _():
        wg_hbm, wu_hbm, wd_hbm = hbm_refs
        wg_ref, wu_ref, wd_ref = vmem_refs
        _stage_weight(wg_hbm.at[layer, which], wg_ref, up_stage, sem)
        _stage_weight(wu_hbm.at[layer, which], wu_ref, up_stage, sem)
        _stage_weight(wd_hbm.at[layer, which], wd_ref, down_stage, sem)


STAGE_DEPTH = 16
UP_CHUNK = 32
DOWN_CHUNK = 64
_FFN_WEIGHT_SPECS = [pl.BlockSpec(memory_space=pl.ANY)] * 3
_FFN_WEIGHT_SCRATCH = [
    pltpu.VMEM((D_MODEL, D_FF), BF16), pltpu.VMEM((D_MODEL, D_FF), BF16), pltpu.VMEM((D_FF, D_MODEL), BF16),
    pltpu.VMEM((STAGE_DEPTH, UP_CHUNK, D_FF), F32), pltpu.VMEM((STAGE_DEPTH, DOWN_CHUNK, D_MODEL), F32),
    pltpu.SemaphoreType.DMA((STAGE_DEPTH,)),
]


def _stage_a_kernel(*refs, n_in, ctx_tiles, layer):
    x_refs = refs[:n_in]
    (mod_ref, g_ref, wg_hbm, wu_hbm, wd_hbm, win_ref, x1_ref, fh_ref, q_ref, k_ref, v_ref,
     wg_ref, wu_ref, wd_ref, up_stage, down_stage, sem) = refs[n_in:]
    _stage_ffn_weights((wg_hbm, wu_hbm, wd_hbm), layer, 0, (wg_ref, wu_ref, wd_ref), up_stage, down_stage, sem)
    is_ctx = pl.program_id(0) < ctx_tiles
    k0 = FH_WIDTH + ATTN_WIDTH

    def modulated_input(sl, _):
        x = x_refs[0][sl] if n_in == 1 else jnp.where(is_ctx, x_refs[0][sl], x_refs[1][sl])
        return x, _modulate(x, g_ref[0:1], mod_ref, 0)

    def half_step(sl, xh):
        x1 = _swiglu_rows(*xh, mod_ref, 0, g_ref, wg_ref, wu_ref, wd_ref)
        x1_ref[sl] = x1
        return _modulate(x1, g_ref[2:3], mod_ref, 1)

    def in_projection(sl, h):
        p = _dot(h, win_ref[...])
        fh_ref[sl] = p[:, :FH_WIDTH].astype(fh_ref.dtype)
        q_ref[sl] = p[:, FH_WIDTH:k0].astype(q_ref.dtype)
        k_ref[sl] = p[:, k0:k0 + KV_WIDTH]
        v_ref[sl] = p[:, k0 + KV_WIDTH:]

    _software_pipeline(x1_ref.shape[0], [modulated_input, half_step, in_projection])


def _stage_b_kernel(x_ref, fhc_ref, fhl_ref, atc_ref, atl_ref, mod_ref, g_ref, wout_ref, wg_hbm, wu_hbm,
                    wd_hbm, *refs, ctx_tiles, layer):
    *o_refs, wg_ref, wu_ref, wd_ref, up_stage, down_stage, sem = refs
    _stage_ffn_weights((wg_hbm, wu_hbm, wd_hbm), layer, 1, (wg_ref, wu_ref, wd_ref), up_stage, down_stage, sem)
    is_ctx = pl.program_id(0) < ctx_tiles
    half = FOURIER_WIDTH + HYENA_WIDTH

    def body(mixed, o_ref):
        def out_projection(sl, _):
            y_fh, y_at = mixed(sl)
            y = _dot(y_fh, wout_ref[:half]) + _dot(y_at, wout_ref[half:])
            x2 = x_ref[sl] + mod_ref[0, 1, 2:3] * _rms(y, g_ref[3:4])
            return x2, _modulate(x2, g_ref[4:5], mod_ref, 2)

        def half_step(sl, xh):
            o_ref[sl] = _swiglu_rows(*xh, mod_ref, 2, g_ref, wg_ref, wu_ref, wd_ref)

        _software_pipeline(x_ref.shape[0], [out_projection, half_step])

    if len(o_refs) == 1:
        body(lambda sl: (jnp.where(is_ctx, fhc_ref[sl], fhl_ref[sl]),
                         jnp.where(is_ctx, atc_ref[sl], atl_ref[sl])), o_refs[0])
    else:
        pl.when(is_ctx)(lambda: body(lambda sl: (fhc_ref[sl], atc_ref[sl]), o_refs[0]))
        pl.when(jnp.logical_not(is_ctx))(lambda: body(lambda sl: (fhl_ref[sl], atl_ref[sl]), o_refs[1]))


def _token_specs(tm, ctx_rows, lat_rows, lat_seq):
    ctx_tiles = ctx_rows // tm
    tiles_per_seq = lat_seq // tm
    tile = lambda width: pl.BlockSpec((tm, width), lambda i: (i, 0))
    ctx_tile = lambda width: pl.BlockSpec((tm, width), lambda i: (jnp.minimum(i, ctx_tiles - 1), 0))
    lat_tile = lambda width: pl.BlockSpec((tm, width), lambda i: (jnp.maximum(i - ctx_tiles, 0), 0))
    request = lambda i: jnp.where(i < ctx_tiles, 0, 1 + (i - ctx_tiles) // tiles_per_seq)
    return ctx_tiles, (ctx_rows + lat_rows) // tm, tile, ctx_tile, lat_tile, request


def _stage_a(xs, mod, norm_g, wg, wu, wd, w_in, layer, ctx_rows, lat_rows, lat_seq, tm):
    ctx_tiles, tiles, tile, ctx_tile, lat_tile, request = _token_specs(tm, ctx_rows, lat_rows, lat_seq)
    t = ctx_rows + lat_rows
    widths = (D_MODEL, FH_WIDTH, ATTN_WIDTH, KV_WIDTH, KV_WIDTH)
    dtypes = (F32, BF16, BF16, F32, F32)
    x_specs = [tile(D_MODEL)] if len(xs) == 1 else [ctx_tile(D_MODEL), lat_tile(D_MODEL)]
    return pl.pallas_call(
        functools.partial(_stage_a_kernel, n_in=len(xs), ctx_tiles=ctx_tiles, layer=layer),
        grid=(tiles,),
        in_specs=x_specs + [
            pl.BlockSpec((None, 1, N_SUB, 3, D_MODEL), lambda i: (layer, request(i), 0, 0, 0)),
            _resident((None, 2 * N_SUB, D_MODEL), lambda i: (layer, 0, 0)),
            *_FFN_WEIGHT_SPECS,
            _resident((None, D_MODEL, IN_WIDTH), lambda i: (layer, 0, 0)),
        ],
        out_specs=[tile(w) for w in widths],
        out_shape=[jax.ShapeDtypeStruct((t, w), dt) for w, dt in zip(widths, dtypes)],
        scratch_shapes=_FFN_WEIGHT_SCRATCH,
        compiler_params=_cparams("arbitrary"),
        name="stage_a",
    )(*xs, mod, norm_g, wg, wu, wd, w_in)


def _stage_b(x1, y_fh, y_att, mod, norm_g, w_out, wg, wu, wd, layer, ctx_rows, lat_rows, lat_seq, tm, split):
    ctx_tiles, tiles, tile, ctx_tile, lat_tile, request = _token_specs(tm, ctx_rows, lat_rows, lat_seq)
    half = FOURIER_WIDTH + HYENA_WIDTH
    if split:
        out_specs = [ctx_tile(D_MODEL), lat_tile(D_MODEL)]
        out_shape = [jax.ShapeDtypeStruct((r, D_MODEL), F32) for r in (ctx_rows, lat_rows)]
    else:
        out_specs = [tile(D_MODEL)]
        out_shape = [jax.ShapeDtypeStruct((ctx_rows + lat_rows, D_MODEL), F32)]
    return pl.pallas_call(
        functools.partial(_stage_b_kernel, ctx_tiles=ctx_tiles, layer=layer),
        grid=(tiles,),
        in_specs=[
            tile(D_MODEL), ctx_tile(half), lat_tile(half), ctx_tile(ATTN_WIDTH), lat_tile(ATTN_WIDTH),
            pl.BlockSpec((None, 1, N_SUB, 3, D_MODEL), lambda i: (layer, request(i), 0, 0, 0)),
            _resident((None, 2 * N_SUB, D_MODEL), lambda i: (layer, 0, 0)),
            _resident((None, D_MODEL, D_MODEL), lambda i: (layer, 0, 0)),
            *_FFN_WEIGHT_SPECS,
        ],
        out_specs=out_specs,
        out_shape=out_shape,
        scratch_shapes=_FFN_WEIGHT_SCRATCH,
        compiler_params=_cparams("arbitrary"),
        name="stage_b",
    )(x1, y_fh[0], y_fh[1], y_att[0], y_att[1], mod, norm_g, w_out, wg, wu, wd)


def _filter_kernel(feats_ref, t_ref, w1_ref, b1_ref, w2_ref, b2_ref, w3_ref, fr_ref, decay_ref,
                   fw_ref, o_ref, *, n):
    fr = fr_ref[...]
    h = jnp.sin(fr * (_dot(feats_ref[...], w1_ref[...]) + b1_ref[...]))
    h = jnp.sin(fr * (_dot(h, w2_ref[...]) + b2_ref[...]))
    h = _dot(h, w3_ref[...])
    window = jnp.exp(-t_ref[...] * jnp.abs(decay_ref[...]))
    width = HYENA_ORDER * HYENA_WIDTH
    row = lax.broadcasted_iota(jnp.int32, (n, width), 0)
    fwd = h[:, :width] * window
    bwd = jnp.where(row == 0, 0.0, h[:, width:] * window)
    even = fwd + bwd
    odd = fwd - bwd
    k_re = _dot(fw_ref[:n], even.astype(BF16))
    k_sn = _dot(fw_ref[n:], odd.astype(BF16))
    sign = (1 - 2 * (row % 2)).astype(F32)
    k_ny = jnp.sum(even * sign, axis=0, keepdims=True)
    s0 = 1.0 / (4.0 * n * n)
    scale = jnp.where(row == 0, s0, 2.0 * s0)
    a = scale * k_re
    o_ref[0] = a
    o_ref[1] = jnp.where(row == 0, 0.0, scale * k_sn)
    o_ref[2] = jnp.where(row == 0, s0 * k_ny, a)


def _filter_spectrum(n, fw, w1, b1, w2, b2, w3, freq, decay):
    feats, t = _hyena_feats(n)
    pad_w = LANES - HYENA_FILTER_WIDTH
    w1p = jnp.pad(w1, ((0, LANES - HYENA_EMB_DIM), (0, pad_w)))
    w2p = jnp.pad(w2, ((0, pad_w), (0, pad_w)))
    w3p = jnp.pad(w3, ((0, pad_w), (0, 0)))
    padv = lambda v: jnp.pad(v, (0, pad_w)).reshape(1, LANES)
    width = HYENA_ORDER * HYENA_WIDTH
    args = (feats, t, w1p, padv(b1), w2p, padv(b2), w3p, padv(freq), decay.reshape(1, width), fw)
    return pl.pallas_call(
        functools.partial(_filter_kernel, n=n),
        out_shape=jax.ShapeDtypeStruct((3, n, width), F32),
        compiler_params=pltpu.CompilerParams(vmem_limit_bytes=VMEM_LIMIT),
        name="hyena_filter",
    )(*args)


def _fh_kernel(p_ref, pos_ref, chan_ref, fw_ref, iw_ref, coef_ref, convw_ref, bias_ref, o_ref,
               st_scr, *, n, group):
    c = HYENA_WIDTH
    cols = [slice(b * c, (b + 1) * c) for b in range(group)]

    for b in range(group):
        t = _dot(p_ref[b, :, :FOURIER_WIDTH], chan_ref[...])
        st_scr[:n, cols[b]] = t[:, :FOURIER_WIDTH].astype(BF16)
        st_scr[n:, cols[b]] = t[:, FOURIER_WIDTH:].astype(BF16)
    y_f = _dot(pos_ref[...], st_scr[...])
    for b in range(group):
        o_ref[b, :, :FOURIER_WIDTH] = y_f[:, cols[b]].astype(o_ref.dtype)

    row = lax.broadcasted_iota(jnp.int32, (n, HYENA_PROJ), 0)
    zs = []
    for b in range(group):
        z = p_ref[b, :, FOURIER_WIDTH:].astype(F32)
        z_prev = jnp.where(row == 0, 0.0, pltpu.roll(z, 1, axis=0))
        z_next = jnp.where(row == n - 1, 0.0, pltpu.roll(z, n - 1, axis=0))
        zs.append(z_prev * convw_ref[0:1] + z * convw_ref[1:2] + z_next * convw_ref[2:3])

    def long_conv(vs, order):
        oc = slice(order * c, (order + 1) * c)
        uf = _dot(fw_ref[...], jnp.concatenate([v.astype(BF16) for v in vs], axis=1))
        a, bb, a2 = coef_ref[0, :, oc], coef_ref[1, :, oc], coef_ref[2, :, oc]
        for b in range(group):
            u_re, u_sn = uf[:n, cols[b]], uf[n:, cols[b]]
            st_scr[:n, cols[b]] = (u_re * a - u_sn * bb).astype(BF16)
            st_scr[n:, cols[b]] = (u_re * bb + u_sn * a2).astype(BF16)
        y = _dot(iw_ref[...], st_scr[...])
        return [y[:, cols[b]] + vs[b] * bias_ref[order:order + 1] for b in range(group)]

    ys = long_conv([z[:, :c] for z in zs], 0)
    ys = long_conv([z[:, c:2 * c] * y for z, y in zip(zs, ys)], 1)
    for b in range(group):
        o_ref[b, :, FOURIER_WIDTH:] = (zs[b][:, 2 * c:] * ys[b]).astype(o_ref.dtype)


def _fh_mix(p_fh, b, first, n, group, pos, chan, fw, iw, coef, conv_w, hbias):
    width = FOURIER_WIDTH + HYENA_WIDTH
    const = lambda shape: _resident(shape, lambda i: (0,) * len(shape))
    return pl.pallas_call(
        functools.partial(_fh_kernel, n=n, group=group),
        grid=(b // group,),
        in_specs=[
            pl.BlockSpec((group, n, FH_WIDTH), lambda i: (i + first // group, 0, 0)),
            const((n, 2 * n)), const((FOURIER_WIDTH, 2 * FOURIER_WIDTH)),
            const((2 * n, n)), const((n, 2 * n)),
            const((3, n, HYENA_ORDER * HYENA_WIDTH)),
            const((3, HYENA_PROJ)), const((HYENA_ORDER, HYENA_WIDTH)),
        ],
        out_specs=pl.BlockSpec((group, n, width), lambda i: (i, 0, 0)),
        out_shape=jax.ShapeDtypeStruct((b, n, width), BF16),
        scratch_shapes=[pltpu.VMEM((2 * n, group * HYENA_WIDTH), BF16)],
        compiler_params=_cparams("parallel"),
        name="fourier_hyena",
    )(p_fh, pos, chan, fw, iw, coef, conv_w, hbias)


def _head_slots(x, kv_head, lo):
    xr = pltpu.roll(x, HEAD_DIM, axis=1)
    if kv_head == 0:
        return jnp.where(lo, x, 0.0), jnp.where(lo, 0.0, xr)
    return jnp.where(lo, xr, 0.0), jnp.where(lo, 0.0, x)


def _slot_ones(rows, slot):
    lo = lax.broadcasted_iota(jnp.int32, (rows, LANES), 1) < HEAD_DIM
    return jnp.where(lo if slot == 0 else jnp.logical_not(lo), 1.0, 0.0).astype(BF16)


def _ctx_attn_kernel(sink_ref, q_ref, k_ref, v_ref, o_ref, kt_ref, vt_ref, *, n, group):
    lo = lax.broadcasted_iota(jnp.int32, (n, LANES), 1) < HEAD_DIM
    top = lax.broadcasted_iota(jnp.int32, (2 * n, 1), 0) < n
    lo_g = lax.broadcasted_iota(jnp.int32, (2 * n, LANES), 1) < HEAD_DIM
    ones = jnp.concatenate([_slot_ones(n, 0), _slot_ones(n, 1)], axis=0)
    for b in range(group):
        kt_ref[b] = k_ref[b].T
        vt_ref[b] = v_ref[b].T
        k = k_ref[b] * LOG2E
        v = v_ref[b]
        for kv_head in range(N_KV_HEADS):
            h0 = 4 * kv_head
            k_cat = jnp.concatenate(_head_slots(k, kv_head, lo), axis=0).astype(BF16)
            v_cat = jnp.concatenate(_head_slots(v, kv_head, lo), axis=0).astype(BF16)
            v_cat = jnp.concatenate([v_cat, ones], axis=1)
            q2 = jnp.concatenate([q_ref[b, :, (2 * kv_head + t) * LANES:(2 * kv_head + t + 1) * LANES]
                                  for t in range(2)], axis=0) * HEAD_DIM ** -0.5
            s = _dot_nt(q2, k_cat)
            e, sink_term = [], []
            for slot in range(2):
                sink = jnp.where(top, sink_ref[h0 + slot], sink_ref[h0 + 2 + slot]) * LOG2E
                cols = s[:, slot * n:(slot + 1) * n]
                m = jnp.maximum(jnp.max(cols, axis=-1, keepdims=True), sink)
                e.append(jnp.exp2(cols - m).astype(BF16))
                sink_term.append(jnp.exp2(sink - m))
            o = _dot(jnp.concatenate(e, axis=1), v_cat)
            o = o[:, :LANES] / (o[:, LANES:] + jnp.where(lo_g, sink_term[0], sink_term[1]))
            for t in range(2):
                tile = 2 * kv_head + t
                o_ref[b, :, tile * LANES:(tile + 1) * LANES] = o[t * n:(t + 1) * n].astype(o_ref.dtype)


def _ctx_attention(q, k, v, b, n, sink):
    group = CTX_ATTN_GROUP
    return pl.pallas_call(
        functools.partial(_ctx_attn_kernel, n=n, group=group),
        grid=(b // group,),
        in_specs=[
            pl.BlockSpec(memory_space=pltpu.SMEM),
            pl.BlockSpec((group, n, ATTN_WIDTH), lambda i: (i, 0, 0)),
            pl.BlockSpec((group, n, KV_WIDTH), lambda i: (i, 0, 0)),
            pl.BlockSpec((group, n, KV_WIDTH), lambda i: (i, 0, 0)),
        ],
        out_specs=[pl.BlockSpec((group, n, ATTN_WIDTH), lambda i: (i, 0, 0)),
                   pl.BlockSpec((group, KV_WIDTH, n), lambda i: (i, 0, 0)),
                   pl.BlockSpec((group, KV_WIDTH, n), lambda i: (i, 0, 0))],
        out_shape=[jax.ShapeDtypeStruct((b, n, ATTN_WIDTH), BF16),
                   jax.ShapeDtypeStruct((b, KV_WIDTH, n), F32),
                   jax.ShapeDtypeStruct((b, KV_WIDTH, n), F32)],
        compiler_params=_cparams("parallel"),
        name="ctx_attention",
    )(sink, q, k, v)


def _rope(x, cos, sin_signed):
    lane = lax.broadcasted_iota(jnp.int32, x.shape, 1)
    first = (lane % (HEAD_DIM // 2)) < HEAD_DIM // 4
    partner = jnp.where(first, pltpu.roll(x, LANES - HEAD_DIM // 4, axis=1),
                        pltpu.roll(x, HEAD_DIM // 4, axis=1))
    return x * cos + partner * sin_signed


def _lat_attn_kernel(sink_ref, q_ref, k_ref, v_ref, ck_ref, cv_ref, cos_ref, sin_ref, o_ref,
                     q_scr, k_scr, v_scr, ck_scr, cv_scr, *, n, c_len):
    nb = n // BLOCK
    cb = c_len // BLOCK
    grp = 2 * BLOCK
    cos, sin = cos_ref[...], sin_ref[...]
    lo = lax.broadcasted_iota(jnp.int32, (n, LANES), 1) < HEAD_DIM
    lo_c = lax.broadcasted_iota(jnp.int32, (c_len, LANES), 1) < HEAD_DIM

    k = _rope(k_ref[0], cos, sin) * LOG2E
    v = v_ref[0]
    ck = ck_ref[0, 0] * LOG2E
    for kv_head in range(N_KV_HEADS):
        for t in range(2):
            tile = 2 * kv_head + t
            q = q_ref[0, :, tile * LANES:(tile + 1) * LANES].astype(F32)
            q = (_rope(q, cos, sin) * HEAD_DIM ** -0.5).astype(BF16)
            for i in range(nb):
                q_scr[kv_head, i * grp + t * BLOCK:i * grp + (t + 1) * BLOCK] = q[i * BLOCK:(i + 1) * BLOCK]
        for slot, (ks, vs, cks, cvs) in enumerate(zip(
                _head_slots(k, kv_head, lo), _head_slots(v, kv_head, lo),
                _head_slots(ck, kv_head, lo_c), _head_slots(cv_ref[0, 0], kv_head, lo_c))):
            ks, vs, cks, cvs = (a.astype(BF16) for a in (ks, vs, cks, cvs))
            ones = _slot_ones(BLOCK, slot)
            for j in range(nb):
                rows = slice(j * grp + slot * BLOCK, j * grp + (slot + 1) * BLOCK)
                k_scr[kv_head, rows] = ks[j * BLOCK:(j + 1) * BLOCK]
                v_scr[kv_head, rows, :LANES] = vs[j * BLOCK:(j + 1) * BLOCK]
                v_scr[kv_head, rows, LANES:] = ones
            for j in range(cb):
                rows = slice(j * grp + slot * BLOCK, j * grp + (slot + 1) * BLOCK)
                ck_scr[kv_head, rows] = cks[j * BLOCK:(j + 1) * BLOCK]
                cv_scr[kv_head, rows, :LANES] = cvs[j * BLOCK:(j + 1) * BLOCK]
                cv_scr[kv_head, rows, LANES:] = ones

    qi = lax.broadcasted_iota(jnp.int32, (grp, BLOCK), 0) % BLOCK
    kj = lax.broadcasted_iota(jnp.int32, (grp, BLOCK), 1)
    keep_prev = kj >= qi
    keep_next = kj <= qi
    top = lax.broadcasted_iota(jnp.int32, (grp, 1), 0) < BLOCK
    lo_g = lax.broadcasted_iota(jnp.int32, (grp, LANES), 1) < HEAD_DIM

    def lane_tiles(s):
        return [s[:, c * BLOCK:(c + 1) * BLOCK] for c in range(s.shape[1] // BLOCK)]

    for kv_head in range(N_KV_HEADS):
        h0 = 4 * kv_head
        sinks = [jnp.where(top, sink_ref[h0 + slot], sink_ref[h0 + 2 + slot]) * LOG2E for slot in range(2)]
        for i in range(nb):
            j0, j1 = max(i - 1, 0), min(i + 2, nb)
            q2 = q_scr[kv_head, i * grp:(i + 1) * grp]
            tiles = lane_tiles(_dot_nt(q2, k_scr[kv_head, j0 * grp:j1 * grp]))
            for b, j in enumerate(range(j0, j1)):
                for slot in range(2):
                    if j == i - 1:
                        tiles[2 * b + slot] = jnp.where(keep_prev, tiles[2 * b + slot], NEG_BIG)
                    elif j == i + 1:
                        tiles[2 * b + slot] = jnp.where(keep_next, tiles[2 * b + slot], NEG_BIG)
            tiles += lane_tiles(_dot_nt(q2, ck_scr[kv_head]))
            sink_term = []
            for slot in range(2):
                mine = tiles[slot::2]
                m = jnp.maximum(jnp.max(functools.reduce(jnp.maximum, mine), axis=-1, keepdims=True),
                                sinks[slot])
                tiles[slot::2] = [jnp.exp2(tl - m).astype(BF16) for tl in mine]
                sink_term.append(jnp.exp2(sinks[slot] - m))
            n_loc = 2 * (j1 - j0)
            e_loc = jnp.concatenate(tiles[:n_loc], axis=1)
            e_ctx = jnp.concatenate(tiles[n_loc:], axis=1)
            o = _dot(e_loc, v_scr[kv_head, j0 * grp:j1 * grp]) + _dot(e_ctx, cv_scr[kv_head])
            o = o[:, :LANES] / (o[:, LANES:] + jnp.where(lo_g, sink_term[0], sink_term[1]))
            o = o.astype(o_ref.dtype)
            for t in range(2):
                tile = 2 * kv_head + t
                o_ref[0, i * BLOCK:(i + 1) * BLOCK, tile * LANES:(tile + 1) * LANES] = o[t * BLOCK:(t + 1) * BLOCK]


def _lat_attention(q, k, v, b, first, n, sink, cache_k, cache_v, layer, cos, sin):
    c_len = cache_k.shape[2]
    return pl.pallas_call(
        functools.partial(_lat_attn_kernel, n=n, c_len=c_len),
        grid=(b,),
        in_specs=[
            pl.BlockSpec(memory_space=pltpu.SMEM),
            pl.BlockSpec((1, n, ATTN_WIDTH), lambda i: (i + first, 0, 0)),
            pl.BlockSpec((1, n, KV_WIDTH), lambda i: (i + first, 0, 0)),
            pl.BlockSpec((1, n, KV_WIDTH), lambda i: (i + first, 0, 0)),
            pl.BlockSpec((1, 1, c_len, KV_WIDTH), lambda i: (i, layer, 0, 0)),
            pl.BlockSpec((1, 1, c_len, KV_WIDTH), lambda i: (i, layer, 0, 0)),
            _resident((n, LANES), lambda i: (0, 0)),
            _resident((n, LANES), lambda i: (0, 0)),
        ],
        out_specs=pl.BlockSpec((1, n, ATTN_WIDTH), lambda i: (i, 0, 0)),
        out_shape=jax.ShapeDtypeStruct((b, n, ATTN_WIDTH), BF16),
        scratch_shapes=[
            pltpu.VMEM((N_KV_HEADS, 2 * n, LANES), BF16),
            pltpu.VMEM((N_KV_HEADS, 2 * n, LANES), BF16),
            pltpu.VMEM((N_KV_HEADS, 2 * n, 2 * LANES), BF16),
            pltpu.VMEM((N_KV_HEADS, 2 * c_len, LANES), BF16),
            pltpu.VMEM((N_KV_HEADS, 2 * c_len, 2 * LANES), BF16),
        ],
        compiler_params=_cparams("parallel"),
        name="latent_attention",
    )(sink, q, k, v, cache_k, cache_v, cos, sin)


def kernel(x_prompt, x_sample, cache_k, cache_v, c, c_ctx, w_mod, b_mod, norm_g, ffn_w_gate, ffn_w_up,
           ffn_w_down, w_in, w_out, hyena_conv_w, hyena_f_w1, hyena_f_b1, hyena_f_w2, hyena_f_b2,
           hyena_f_w3, hyena_f_freq, hyena_decay, hyena_bias, attn_sink):
    batch, seq, d = x_prompt.shape
    dec_batch, dec_seq, _ = x_sample.shape
    past_len = cache_k.shape[2]

    cvecs = jnp.concatenate([c_ctx[None], c], axis=0)
    cvecs = jnp.pad(cvecs, ((0, MOD_ROWS - cvecs.shape[0]), (0, 0)))
    mod = _modulation(cvecs, w_mod, b_mod).reshape(DEPTH, MOD_ROWS, N_SUB, 3, d)

    wg, wu, wd = ffn_w_gate, ffn_w_up, ffn_w_down
    w_in_b = w_in.astype(BF16)
    w_out_b = w_out.astype(BF16)
    ck = cache_k.reshape(dec_batch, DEPTH, past_len, KV_WIDTH)
    cv = cache_v.reshape(dec_batch, DEPTH, past_len, KV_WIDTH)
    rope_cos, rope_sin = (jnp.asarray(t) for t in _rope_tables(dec_seq))

    tables = {}
    for n in (seq, dec_seq):
        fw, iw = (jnp.asarray(t).astype(BF16) for t in _hyena_dft(n))
        pos, chan = (jnp.asarray(t).astype(BF16) for t in _fourier_tables(n))
        tables[n] = (fw, iw, pos, chan)

    ctx_rows, lat_rows = batch * seq, dec_batch * dec_seq
    rows = ctx_rows + lat_rows
    sizes = (ctx_rows, lat_rows, dec_seq)
    xs = (x_prompt.reshape(ctx_rows, d), x_sample.reshape(lat_rows, d))
    new_k, new_v = [], []
    for l in range(DEPTH):
        x1, p_fh, q, k, v = _stage_a(xs, mod, norm_g, wg, wu, wd, w_in_b, l, *sizes, tm=TOKEN_TILE)
        y_fh, y_att = [], []
        for n, nseq, first_seq, latent in ((seq, batch, 0, False), (dec_seq, dec_batch, ctx_rows // dec_seq, True)):
            fw, iw, pos, chan = tables[n]
            coef = _filter_spectrum(n, fw, hyena_f_w1[l], hyena_f_b1[l], hyena_f_w2[l], hyena_f_b2[l],
                                    hyena_f_w3[l], hyena_f_freq[l], hyena_decay[l])
            y = _fh_mix(p_fh.reshape(rows // n, n, FH_WIDTH), nseq, first_seq, n, FH_GROUP[latent],
                        pos, chan, fw, iw, coef, hyena_conv_w[l], hyena_bias[l])
            y_fh.append(y.reshape(nseq * n, -1))
            qs, ks, vs = (a.reshape(rows // n, n, -1) for a in (q, k, v))
            if latent:
                y = _lat_attention(qs, ks, vs, nseq, first_seq, n, attn_sink[l], ck, cv, l, rope_cos, rope_sin)
            else:
                y, k_t, v_t = _ctx_attention(qs, ks, vs, nseq, n, attn_sink[l])
                new_k.append(k_t)
                new_v.append(v_t)
            y_att.append(y.reshape(nseq * n, -1))
        xs = _stage_b(x1, y_fh, y_att, mod, norm_g, w_out_b, wg, wu, wd, l, *sizes, tm=TOKEN_TILE,
                      split=l == DEPTH - 1)

    def cache_layout(per_layer):
        t = jnp.stack(per_layer, axis=0).reshape(DEPTH, batch, N_KV_HEADS, HEAD_DIM, seq)
        return jnp.transpose(t, (1, 0, 4, 2, 3))

    return (xs[0].reshape(batch, seq, d), xs[1].reshape(dec_batch, dec_seq, d),
            cache_layout(new_k), cache_layout(new_v))
```

```python
import functools
import math

import numpy as np
import jax
import jax.numpy as jnp
from jax import lax
from jax.experimental import pallas as pl
from jax.experimental.pallas import tpu as pltpu

F32 = jnp.float32
BF16 = jnp.bfloat16

D_MODEL = 1024
DEPTH = 2
GRID_W = 64
HEAD_DIM = 64
N_Q_HEADS = 8
N_KV_HEADS = 2
ATTN_WIDTH = N_Q_HEADS * HEAD_DIM
KV_WIDTH = N_KV_HEADS * HEAD_DIM
FOURIER_WIDTH = 256
FOURIER_GROUPS = 4
FOURIER_GROUP_DIM = 64
HYENA_WIDTH = 256
HYENA_ORDER = 2
HYENA_PROJ = 3 * HYENA_WIDTH
HYENA_EMB_DIM = 33
HYENA_BANDS = 16
HYENA_FILTER_WIDTH = 64
FH_WIDTH = FOURIER_WIDTH + HYENA_PROJ
ATT_WIDTH = ATTN_WIDTH + 2 * KV_WIDTH
IN_WIDTH = FH_WIDTH + ATT_WIDTH
BLOCK = 128
WINDOW = 128
ROPE_BASE = 10000.0
D_FF = 2816
N_SUB = 3
RMS_EPS = 1e-6

LANES = 128
MOD_ROWS = 16
VMEM_LIMIT = 56 * 1024 * 1024
NEG_BIG = -1e30
LOG2E = 1.4426950408889634

TOKEN_TILE = 512
ROW_GROUP = 256
MOD_TILE = 3072
FH_GROUP = {False: 8, True: 2}
CTX_ATTN_GROUP = 4


def _cparams(*sem):
    return pltpu.CompilerParams(dimension_semantics=sem, vmem_limit_bytes=VMEM_LIMIT)


def _rms(x, g):
    return x * lax.rsqrt(jnp.mean(x * x, axis=-1, keepdims=True) + RMS_EPS) * g


def _dot(a, b):
    return jnp.dot(a, b, preferred_element_type=F32)


def _dot_nt(a, b):
    return lax.dot_general(a, b, (((1,), (1,)), ((), ())), preferred_element_type=F32)


@functools.lru_cache(maxsize=None)
def _hyena_dft(n):
    f = np.arange(n, dtype=np.int64)[:, None]
    s = np.arange(n, dtype=np.int64)[None, :]
    ang = np.pi * ((f * s) % (2 * n)).astype(np.float64) / n
    c = np.cos(ang)
    sn = np.sin(ang)
    sn[0, :] = 1.0 - 2.0 * (np.arange(n) % 2)
    fw = np.concatenate([c, sn], axis=0).astype(np.float32)
    iw = np.concatenate([c, sn.T], axis=1).astype(np.float32)
    return fw, iw


@functools.lru_cache(maxsize=None)
def _fourier_tables(n):
    f = np.arange(n, dtype=np.int64)[:, None]
    s = np.arange(n, dtype=np.int64)[None, :]
    ang = 2.0 * np.pi * ((f * s) % n).astype(np.float64) / n
    pos = np.concatenate([np.cos(ang), -np.sin(ang)], axis=1) / math.sqrt(n)
    gd = FOURIER_GROUP_DIM
    a = np.arange(gd, dtype=np.int64)
    ang_c = 2.0 * np.pi * ((a[:, None] * a[None, :]) % gd).astype(np.float64) / gd
    bc = np.kron(np.eye(FOURIER_GROUPS), np.cos(ang_c)) / math.sqrt(gd)
    bs = np.kron(np.eye(FOURIER_GROUPS), np.sin(ang_c)) / math.sqrt(gd)
    chan = np.concatenate([bc, bs], axis=1)
    return pos.astype(np.float32), chan.astype(np.float32)


@functools.lru_cache(maxsize=None)
def _rope_tables(n):
    half = HEAD_DIM // 2
    inv = ROPE_BASE ** (-np.arange(0, half, 2, dtype=np.float64) / half)
    t = np.arange(n)
    row = (t // GRID_W).astype(np.float64)
    col = (t % GRID_W).astype(np.float64)
    lane = np.arange(LANES)
    d = lane % HEAD_DIM
    pos = np.where((d // half)[None, :] == 0, row[:, None], col[:, None])
    ang = pos * inv[d % (half // 2)][None, :]
    sign = np.where((d % half) < half // 2, -1.0, 1.0)[None, :]
    return np.cos(ang).astype(np.float32), (np.sin(ang) * sign).astype(np.float32)


def _hyena_feats(n):
    d = jnp.arange(n, dtype=F32)
    t = jnp.linspace(0.0, 1.0, n, dtype=F32)[:, None]
    f = jnp.linspace(1e-4, HYENA_BANDS - 1, HYENA_BANDS, dtype=F32)
    ang = (2.0 * math.pi / n) * d[:, None] * f[None, :]
    feats = jnp.concatenate([t, jnp.cos(ang), -jnp.sin(ang)], axis=-1)
    return jnp.pad(feats, ((0, 0), (0, LANES - HYENA_EMB_DIM))), t


def _mod_kernel(c_ref, w_ref, b_ref, o_ref):
    c = c_ref[...]
    s = c / (1.0 + jnp.exp(-c))
    o_ref[0] = _dot(s.astype(BF16), w_ref[0].astype(BF16)) + b_ref[0]


def _modulation(cvecs, w_mod, b_mod):
    depth, _, width = w_mod.shape
    tn = MOD_TILE
    return pl.pallas_call(
        _mod_kernel,
        grid=(depth, width // tn),
        in_specs=[
            pl.BlockSpec((MOD_ROWS, D_MODEL), lambda l, j: (0, 0)),
            pl.BlockSpec((1, D_MODEL, tn), lambda l, j: (l, 0, j)),
            pl.BlockSpec((1, 1, tn), lambda l, j: (l, 0, j)),
        ],
        out_specs=pl.BlockSpec((1, MOD_ROWS, tn), lambda l, j: (l, 0, j)),
        out_shape=jax.ShapeDtypeStruct((depth, MOD_ROWS, width), F32),
        compiler_params=_cparams("arbitrary", "arbitrary"),
        name="modulation",
    )(cvecs, w_mod, b_mod.reshape(depth, 1, width))


def _resident(block_shape, index_map):
    return pl.BlockSpec(block_shape, index_map, pipeline_mode=pl.Buffered(1))


def _modulate(x, g_row, mod_ref, sub):
    return (_rms(x, g_row) * (1.0 + mod_ref[0, sub, 1:2]) + mod_ref[0, sub, 0:1]).astype(BF16)


def _swiglu_rows(x, h, mod_ref, sub, g_ref, wg_ref, wu_ref, wd_ref):
    gate = _dot(h, wg_ref[...])
    up = _dot(h, wu_ref[...])
    act = (gate / (1.0 + jnp.exp(-gate))) * up
    y = _dot(act.astype(BF16), wd_ref[...])
    return x + (0.5 * mod_ref[0, sub, 2:3]) * _rms(y, g_ref[2 * sub + 1:2 * sub + 2])


def _software_pipeline(rows, phases):
    groups = [slice(r, r + ROW_GROUP) for r in range(0, rows, ROW_GROUP)]
    state = {}
    for turn in range(len(groups) + len(phases) - 1):
        for p, phase in enumerate(phases):
            g = turn - p
            if 0 <= g < len(groups):
                state[g] = phase(groups[g], state.get(g))


def _stage_weight(src_hbm, dst_ref, stage_ref, sem):
    depth, rows = stage_ref.shape[0], stage_ref.shape[1]
    n_chunks = dst_ref.shape[0] // rows

    def chunk_copy(c):
        slot = c % depth
        return pltpu.make_async_copy(src_hbm.at[pl.ds(c * rows, rows)], stage_ref.at[slot], sem.at[slot])

    for c in range(depth - 1):
        chunk_copy(c).start(priority=c % 2)
    for c in range(n_chunks):
        if c + depth - 1 < n_chunks:
            chunk_copy(c + depth - 1).start(priority=(c + depth - 1) % 2)
        chunk_copy(c).wait()
        dst_ref[c * rows:(c + 1) * rows] = stage_ref[c % depth].astype(BF16)


def _stage_ffn_weights(hbm_refs, layer, which, vmem_refs, up_stage, down_stage, sem):
    @pl.when(pl.program_id(0) == 0)
    def _():
        wg_hbm, wu_hbm, wd_hbm = hbm_refs
        wg_ref, wu_ref, wd_ref = vmem_refs
        _stage_weight(wg_hbm.at[layer, which], wg_ref, up_stage, sem)
        _stage_weight(wu_hbm.at[layer, which], wu_ref, up_stage, sem)
        _stage_weight(wd_hbm.at[layer, which], wd_ref, down_stage, sem)


STAGE_DEPTH = 16
UP_CHUNK = 32
DOWN_CHUNK = 64
_FFN_WEIGHT_SPECS = [pl.BlockSpec(memory_space=pl.ANY)] * 3
_FFN_WEIGHT_SCRATCH = [
    pltpu.VMEM((D_MODEL, D_FF), BF16), pltpu.VMEM((D_MODEL, D_FF), BF16), pltpu.VMEM((D_FF, D_MODEL), BF16),
    pltpu.VMEM((STAGE_DEPTH, UP_CHUNK, D_FF), F32), pltpu.VMEM((STAGE_DEPTH, DOWN_CHUNK, D_MODEL), F32),
    pltpu.SemaphoreType.DMA((STAGE_DEPTH,)),
]


def _stage_a_kernel(*refs, n_in, ctx_tiles, layer):
    x_refs = refs[:n_in]
    (mod_ref, g_ref, wg_hbm, wu_hbm, wd_hbm, win_ref, x1_ref, fh_ref, q_ref, k_ref, v_ref,
     wg_ref, wu_ref, wd_ref, up_stage, down_stage, sem) = refs[n_in:]
    _stage_ffn_weights((wg_hbm, wu_hbm, wd_hbm), layer, 0, (wg_ref, wu_ref, wd_ref), up_stage, down_stage, sem)
    is_ctx = pl.program_id(0) < ctx_tiles
    k0 = FH_WIDTH + ATTN_WIDTH

    def modulated_input(sl, _):
        x = x_refs[0][sl] if n_in == 1 else jnp.where(is_ctx, x_refs[0][sl], x_refs[1][sl])
        return x, _modulate(x, g_ref[0:1], mod_ref, 0)

    def half_step(sl, xh):
        x1 = _swiglu_rows(*xh, mod_ref, 0, g_ref, wg_ref, wu_ref, wd_ref)
        x1_ref[sl] = x1
        return _modulate(x1, g_ref[2:3], mod_ref, 1)

    def in_projection(sl, h):
        p = _dot(h, win_ref[...])
        fh_ref[sl] = p[:, :FH_WIDTH].astype(fh_ref.dtype)
        q_ref[sl] = p[:, FH_WIDTH:k0].astype(q_ref.dtype)
        k_ref[sl] = p[:, k0:k0 + KV_WIDTH]
        v_ref[sl] = p[:, k0 + KV_WIDTH:]

    _software_pipeline(x1_ref.shape[0], [modulated_input, half_step, in_projection])


def _stage_b_kernel(x_ref, fhc_ref, fhl_ref, atc_ref, atl_ref, mod_ref, g_ref, wout_ref, wg_hbm, wu_hbm,
                    wd_hbm, *refs, ctx_tiles, layer):
    *o_refs, wg_ref, wu_ref, wd_ref, up_stage, down_stage, sem = refs
    _stage_ffn_weights((wg_hbm, wu_hbm, wd_hbm), layer, 1, (wg_ref, wu_ref, wd_ref), up_stage, down_stage, sem)
    is_ctx = pl.program_id(0) < ctx_tiles
    half = FOURIER_WIDTH + HYENA_WIDTH

    def body(mixed, o_ref):
        def out_projection(sl, _):
            y_fh, y_at = mixed(sl)
            y = _dot(y_fh, wout_ref[:half]) + _dot(y_at, wout_ref[half:])
            x2 = x_ref[sl] + mod_ref[0, 1, 2:3] * _rms(y, g_ref[3:4])
            return x2, _modulate(x2, g_ref[4:5], mod_ref, 2)

        def half_step(sl, xh):
            o_ref[sl] = _swiglu_rows(*xh, mod_ref, 2, g_ref, wg_ref, wu_ref, wd_ref)

        _software_pipeline(x_ref.shape[0], [out_projection, half_step])

    if len(o_refs) == 1:
        body(lambda sl: (jnp.where(is_ctx, fhc_ref[sl], fhl_ref[sl]),
                         jnp.where(is_ctx, atc_ref[sl], atl_ref[sl])), o_refs[0])
    else:
        pl.when(is_ctx)(lambda: body(lambda sl: (fhc_ref[sl], atc_ref[sl]), o_refs[0]))
        pl.when(jnp.logical_not(is_ctx))(lambda: body(lambda sl: (fhl_ref[sl], atl_ref[sl]), o_refs[1]))


def _token_specs(tm, ctx_rows, lat_rows, lat_seq):
    ctx_tiles = ctx_rows // tm
    tiles_per_seq = lat_seq // tm
    tile = lambda width: pl.BlockSpec((tm, width), lambda i: (i, 0))
    ctx_tile = lambda width: pl.BlockSpec((tm, width), lambda i: (jnp.minimum(i, ctx_tiles - 1), 0))
    lat_tile = lambda width: pl.BlockSpec((tm, width), lambda i: (jnp.maximum(i - ctx_tiles, 0), 0))
    request = lambda i: jnp.where(i < ctx_tiles, 0, 1 + (i - ctx_tiles) // tiles_per_seq)
    return ctx_tiles, (ctx_rows + lat_rows) // tm, tile, ctx_tile, lat_tile, request


def _stage_a(xs, mod, norm_g, wg, wu, wd, w_in, layer, ctx_rows, lat_rows, lat_seq, tm):
    ctx_tiles, tiles, tile, ctx_tile, lat_tile, request = _token_specs(tm, ctx_rows, lat_rows, lat_seq)
    t = ctx_rows + lat_rows
    widths = (D_MODEL, FH_WIDTH, ATTN_WIDTH, KV_WIDTH, KV_WIDTH)
    dtypes = (F32, BF16, BF16, F32, F32)
    x_specs = [tile(D_MODEL)] if len(xs) == 1 else [ctx_tile(D_MODEL), lat_tile(D_MODEL)]
    return pl.pallas_call(
        functools.partial(_stage_a_kernel, n_in=len(xs), ctx_tiles=ctx_tiles, layer=layer),
        grid=(tiles,),
        in_specs=x_specs + [
            pl.BlockSpec((None, 1, N_SUB, 3, D_MODEL), lambda i: (layer, request(i), 0, 0, 0)),
            _resident((None, 2 * N_SUB, D_MODEL), lambda i: (layer, 0, 0)),
            *_FFN_WEIGHT_SPECS,
            _resident((None, D_MODEL, IN_WIDTH), lambda i: (layer, 0, 0)),
        ],
        out_specs=[tile(w) for w in widths],
        out_shape=[jax.ShapeDtypeStruct((t, w), dt) for w, dt in zip(widths, dtypes)],
        scratch_shapes=_FFN_WEIGHT_SCRATCH,
        compiler_params=_cparams("arbitrary"),
        name="stage_a",
    )(*xs, mod, norm_g, wg, wu, wd, w_in)


def _stage_b(x1, y_fh, y_att, mod, norm_g, w_out, wg, wu, wd, layer, ctx_rows, lat_rows, lat_seq, tm, split):
    ctx_tiles, tiles, tile, ctx_tile, lat_tile, request = _token_specs(tm, ctx_rows, lat_rows, lat_seq)
    half = FOURIER_WIDTH + HYENA_WIDTH
    if split:
        out_specs = [ctx_tile(D_MODEL), lat_tile(D_MODEL)]
        out_shape = [jax.ShapeDtypeStruct((r, D_MODEL), F32) for r in (ctx_rows, lat_rows)]
    else:
        out_specs = [tile(D_MODEL)]
        out_shape = [jax.ShapeDtypeStruct((ctx_rows + lat_rows, D_MODEL), F32)]
    return pl.pallas_call(
        functools.partial(_stage_b_kernel, ctx_tiles=ctx_tiles, layer=layer),
        grid=(tiles,),
        in_specs=[
            tile(D_MODEL), ctx_tile(half), lat_tile(half), ctx_tile(ATTN_WIDTH), lat_tile(ATTN_WIDTH),
            pl.BlockSpec((None, 1, N_SUB, 3, D_MODEL), lambda i: (layer, request(i), 0, 0, 0)),
            _resident((None, 2 * N_SUB, D_MODEL), lambda i: (layer, 0, 0)),
            _resident((None, D_MODEL, D_MODEL), lambda i: (layer, 0, 0)),
            *_FFN_WEIGHT_SPECS,
        ],
        out_specs=out_specs,
        out_shape=out_shape,
        scratch_shapes=_FFN_WEIGHT_SCRATCH,
        compiler_params=_cparams("arbitrary"),
        name="stage_b",
    )(x1, y_fh[0], y_fh[1], y_att[0], y_att[1], mod, norm_g, w_out, wg, wu, wd)


def _filter_kernel(feats_ref, t_ref, w1_ref, b1_ref, w2_ref, b2_ref, w3_ref, fr_ref, decay_ref,
                   fw_ref, o_ref, *, n):
    fr = fr_ref[...]
    h = jnp.sin(fr * (_dot(feats_ref[...], w1_ref[...]) + b1_ref[...]))
    h = jnp.sin(fr * (_dot(h, w2_ref[...]) + b2_ref[...]))
    h = _dot(h, w3_ref[...])
    window = jnp.exp(-t_ref[...] * jnp.abs(decay_ref[...]))
    width = HYENA_ORDER * HYENA_WIDTH
    row = lax.broadcasted_iota(jnp.int32, (n, width), 0)
    fwd = h[:, :width] * window
    bwd = jnp.where(row == 0, 0.0, h[:, width:] * window)
    even = fwd + bwd
    odd = fwd - bwd
    k_re = _dot(fw_ref[:n], even.astype(BF16))
    k_sn = _dot(fw_ref[n:], odd.astype(BF16))
    sign = (1 - 2 * (row % 2)).astype(F32)
    k_ny = jnp.sum(even * sign, axis=0, keepdims=True)
    s0 = 1.0 / (4.0 * n * n)
    scale = jnp.where(row == 0, s0, 2.0 * s0)
    a = scale * k_re
    o_ref[0] = a
    o_ref[1] = jnp.where(row == 0, 0.0, scale * k_sn)
    o_ref[2] = jnp.where(row == 0, s0 * k_ny, a)


def _filter_spectrum(n, fw, w1, b1, w2, b2, w3, freq, decay):
    feats, t = _hyena_feats(n)
    pad_w = LANES - HYENA_FILTER_WIDTH
    w1p = jnp.pad(w1, ((0, LANES - HYENA_EMB_DIM), (0, pad_w)))
    w2p = jnp.pad(w2, ((0, pad_w), (0, pad_w)))
    w3p = jnp.pad(w3, ((0, pad_w), (0, 0)))
    padv = lambda v: jnp.pad(v, (0, pad_w)).reshape(1, LANES)
    width = HYENA_ORDER * HYENA_WIDTH
    args = (feats, t, w1p, padv(b1), w2p, padv(b2), w3p, padv(freq), decay.reshape(1, width), fw)
    return pl.pallas_call(
        functools.partial(_filter_kernel, n=n),
        out_shape=jax.ShapeDtypeStruct((3, n, width), F32),
        compiler_params=pltpu.CompilerParams(vmem_limit_bytes=VMEM_LIMIT),
        name="hyena_filter",
    )(*args)


def _fh_kernel(p_ref, pos_ref, chan_ref, fw_ref, iw_ref, coef_ref, convw_ref, bias_ref, o_ref,
               st_scr, *, n, group):
    c = HYENA_WIDTH
    cols = [slice(b * c, (b + 1) * c) for b in range(group)]

    for b in range(group):
        t = _dot(p_ref[b, :, :FOURIER_WIDTH], chan_ref[...])
        st_scr[:n, cols[b]] = t[:, :FOURIER_WIDTH].astype(BF16)
        st_scr[n:, cols[b]] = t[:, FOURIER_WIDTH:].astype(BF16)
    y_f = _dot(pos_ref[...], st_scr[...])
    for b in range(group):
        o_ref[b, :, :FOURIER_WIDTH] = y_f[:, cols[b]].astype(o_ref.dtype)

    row = lax.broadcasted_iota(jnp.int32, (n, HYENA_PROJ), 0)
    zs = []
    for b in range(group):
        z = p_ref[b, :, FOURIER_WIDTH:].astype(F32)
        z_prev = jnp.where(row == 0, 0.0, pltpu.roll(z, 1, axis=0))
        z_next = jnp.where(row == n - 1, 0.0, pltpu.roll(z, n - 1, axis=0))
        zs.append(z_prev * convw_ref[0:1] + z * convw_ref[1:2] + z_next * convw_ref[2:3])

    def long_conv(vs, order):
        oc = slice(order * c, (order + 1) * c)
        uf = _dot(fw_ref[...], jnp.concatenate([v.astype(BF16) for v in vs], axis=1))
        a, bb, a2 = coef_ref[0, :, oc], coef_ref[1, :, oc], coef_ref[2, :, oc]
        for b in range(group):
            u_re, u_sn = uf[:n, cols[b]], uf[n:, cols[b]]
            st_scr[:n, cols[b]] = (u_re * a - u_sn * bb).astype(BF16)
            st_scr[n:, cols[b]] = (u_re * bb + u_sn * a2).astype(BF16)
        y = _dot(iw_ref[...], st_scr[...])
        return [y[:, cols[b]] + vs[b] * bias_ref[order:order + 1] for b in range(group)]

    ys = long_conv([z[:, :c] for z in zs], 0)
    ys = long_conv([z[:, c:2 * c] * y for z, y in zip(zs, ys)], 1)
    for b in range(group):
        o_ref[b, :, FOURIER_WIDTH:] = (zs[b][:, 2 * c:] * ys[b]).astype(o_ref.dtype)


def _fh_mix(p_fh, b, first, n, group, pos, chan, fw, iw, coef, conv_w, hbias):
    width = FOURIER_WIDTH + HYENA_WIDTH
    const = lambda shape: _resident(shape, lambda i: (0,) * len(shape))
    return pl.pallas_call(
        functools.partial(_fh_kernel, n=n, group=group),
        grid=(b // group,),
        in_specs=[
            pl.BlockSpec((group, n, FH_WIDTH), lambda i: (i + first // group, 0, 0)),
            const((n, 2 * n)), const((FOURIER_WIDTH, 2 * FOURIER_WIDTH)),
            const((2 * n, n)), const((n, 2 * n)),
            const((3, n, HYENA_ORDER * HYENA_WIDTH)),
            const((3, HYENA_PROJ)), const((HYENA_ORDER, HYENA_WIDTH)),
        ],
        out_specs=pl.BlockSpec((group, n, width), lambda i: (i, 0, 0)),
        out_shape=jax.ShapeDtypeStruct((b, n, width), BF16),
        scratch_shapes=[pltpu.VMEM((2 * n, group * HYENA_WIDTH), BF16)],
        compiler_params=_cparams("parallel"),
        name="fourier_hyena",
    )(p_fh, pos, chan, fw, iw, coef, conv_w, hbias)


def _head_slots(x, kv_head, lo):
    xr = pltpu.roll(x, HEAD_DIM, axis=1)
    if kv_head == 0:
        return jnp.where(lo, x, 0.0), jnp.where(lo, 0.0, xr)
    return jnp.where(lo, xr, 0.0), jnp.where(lo, 0.0, x)


def _slot_ones(rows, slot):
    lo = lax.broadcasted_iota(jnp.int32, (rows, LANES), 1) < HEAD_DIM
    return jnp.where(lo if slot == 0 else jnp.logical_not(lo), 1.0, 0.0).astype(BF16)


def _ctx_attn_kernel(sink_ref, q_ref, k_ref, v_ref, o_ref, kt_ref, vt_ref, *, n, group):
    lo = lax.broadcasted_iota(jnp.int32, (n, LANES), 1) < HEAD_DIM
    top = lax.broadcasted_iota(jnp.int32, (2 * n, 1), 0) < n
    lo_g = lax.broadcasted_iota(jnp.int32, (2 * n, LANES), 1) < HEAD_DIM
    ones = jnp.concatenate([_slot_ones(n, 0), _slot_ones(n, 1)], axis=0)
    for b in range(group):
        kt_ref[b] = k_ref[b].T
        vt_ref[b] = v_ref[b].T
        k = k_ref[b] * LOG2E
        v = v_ref[b]
        for kv_head in range(N_KV_HEADS):
            h0 = 4 * kv_head
            k_cat = jnp.concatenate(_head_slots(k, kv_head, lo), axis=0).astype(BF16)
            v_cat = jnp.concatenate(_head_slots(v, kv_head, lo), axis=0).astype(BF16)
            v_cat = jnp.concatenate([v_cat, ones], axis=1)
            q2 = jnp.concatenate([q_ref[b, :, (2 * kv_head + t) * LANES:(2 * kv_head + t + 1) * LANES]
                                  for t in range(2)], axis=0) * HEAD_DIM ** -0.5
            s = _dot_nt(q2, k_cat)
            e, sink_term = [], []
            for slot in range(2):
                sink = jnp.where(top, sink_ref[h0 + slot], sink_ref[h0 + 2 + slot]) * LOG2E
                cols = s[:, slot * n:(slot + 1) * n]
                m = jnp.maximum(jnp.max(cols, axis=-1, keepdims=True), sink)
                e.append(jnp.exp2(cols - m).astype(BF16))
                sink_term.append(jnp.exp2(sink - m))
            o = _dot(jnp.concatenate(e, axis=1), v_cat)
            o = o[:, :LANES] / (o[:, LANES:] + jnp.where(lo_g, sink_term[0], sink_term[1]))
            for t in range(2):
                tile = 2 * kv_head + t
                o_ref[b, :, tile * LANES:(tile + 1) * LANES] = o[t * n:(t + 1) * n].astype(o_ref.dtype)


def _ctx_attention(q, k, v, b, n, sink):
    group = CTX_ATTN_GROUP
    return pl.pallas_call(
        functools.partial(_ctx_attn_kernel, n=n, group=group),
        grid=(b // group,),
        in_specs=[
            pl.BlockSpec(memory_space=pltpu.SMEM),
            pl.BlockSpec((group, n, ATTN_WIDTH), lambda i: (i, 0, 0)),
            pl.BlockSpec((group, n, KV_WIDTH), lambda i: (i, 0, 0)),
            pl.BlockSpec((group, n, KV_WIDTH), lambda i: (i, 0, 0)),
        ],
        out_specs=[pl.BlockSpec((group, n, ATTN_WIDTH), lambda i: (i, 0, 0)),
                   pl.BlockSpec((group, KV_WIDTH, n), lambda i: (i, 0, 0)),
                   pl.BlockSpec((group, KV_WIDTH, n), lambda i: (i, 0, 0))],
        out_shape=[jax.ShapeDtypeStruct((b, n, ATTN_WIDTH), BF16),
                   jax.ShapeDtypeStruct((b, KV_WIDTH, n), F32),
                   jax.ShapeDtypeStruct((b, KV_WIDTH, n), F32)],
        compiler_params=_cparams("parallel"),
        name="ctx_attention",
    )(sink, q, k, v)


def _rope(x, cos, sin_signed):
    lane = lax.broadcasted_iota(jnp.int32, x.shape, 1)
    first = (lane % (HEAD_DIM // 2)) < HEAD_DIM // 4
    partner = jnp.where(first, pltpu.roll(x, LANES - HEAD_DIM // 4, axis=1),
                        pltpu.roll(x, HEAD_DIM // 4, axis=1))
    return x * cos + partner * sin_signed


def _lat_attn_kernel(sink_ref, q_ref, k_ref, v_ref, ck_ref, cv_ref, cos_ref, sin_ref, o_ref,
                     q_scr, k_scr, v_scr, ck_scr, cv_scr, *, n, c_len):
    nb = n // BLOCK
    cb = c_len // BLOCK
    grp = 2 * BLOCK
    cos, sin = cos_ref[...], sin_ref[...]
    lo = lax.broadcasted_iota(jnp.int32, (n, LANES), 1) < HEAD_DIM
    lo_c = lax.broadcasted_iota(jnp.int32, (c_len, LANES), 1) < HEAD_DIM

    k = _rope(k_ref[0], cos, sin) * LOG2E
    v = v_ref[0]
    ck = ck_ref[0, 0] * LOG2E
    for kv_head in range(N_KV_HEADS):
        for t in range(2):
            tile = 2 * kv_head + t
            q = q_ref[0, :, tile * LANES:(tile + 1) * LANES].astype(F32)
            q = (_rope(q, cos, sin) * HEAD_DIM ** -0.5).astype(BF16)
            for i in range(nb):
                q_scr[kv_head, i * grp + t * BLOCK:i * grp + (t + 1) * BLOCK] = q[i * BLOCK:(i + 1) * BLOCK]
        for slot, (ks, vs, cks, cvs) in enumerate(zip(
                _head_slots(k, kv_head, lo), _head_slots(v, kv_head, lo),
                _head_slots(ck, kv_head, lo_c), _head_slots(cv_ref[0, 0], kv_head, lo_c))):
            ks, vs, cks, cvs = (a.astype(BF16) for a in (ks, vs, cks, cvs))
            ones = _slot_ones(BLOCK, slot)
            for j in range(nb):
                rows = slice(j * grp + slot * BLOCK, j * grp + (slot + 1) * BLOCK)
                k_scr[kv_head, rows] = ks[j * BLOCK:(j + 1) * BLOCK]
                v_scr[kv_head, rows, :LANES] = vs[j * BLOCK:(j + 1) * BLOCK]
                v_scr[kv_head, rows, LANES:] = ones
            for j in range(cb):
                rows = slice(j * grp + slot * BLOCK, j * grp + (slot + 1) * BLOCK)
                ck_scr[kv_head, rows] = cks[j * BLOCK:(j + 1) * BLOCK]
                cv_scr[kv_head, rows, :LANES] = cvs[j * BLOCK:(j + 1) * BLOCK]
                cv_scr[kv_head, rows, LANES:] = ones

    qi = lax.broadcasted_iota(jnp.int32, (grp, BLOCK), 0) % BLOCK
    kj = lax.broadcasted_iota(jnp.int32, (grp, BLOCK), 1)
    keep_prev = kj >= qi
    keep_next = kj <= qi
    top = lax.broadcasted_iota(jnp.int32, (grp, 1), 0) < BLOCK
    lo_g = lax.broadcasted_iota(jnp.int32, (grp, LANES), 1) < HEAD_DIM

    def lane_tiles(s):
        return [s[:, c * BLOCK:(c + 1) * BLOCK] for c in range(s.shape[1] // BLOCK)]

    for kv_head in range(N_KV_HEADS):
        h0 = 4 * kv_head
        sinks = [jnp.where(top, sink_ref[h0 + slot], sink_ref[h0 + 2 + slot]) * LOG2E for slot in range(2)]
        for i in range(nb):
            j0, j1 = max(i - 1, 0), min(i + 2, nb)
            q2 = q_scr[kv_head, i * grp:(i + 1) * grp]
            tiles = lane_tiles(_dot_nt(q2, k_scr[kv_head, j0 * grp:j1 * grp]))
            for b, j in enumerate(range(j0, j1)):
                for slot in range(2):
                    if j == i - 1:
                        tiles[2 * b + slot] = jnp.where(keep_prev, tiles[2 * b + slot], NEG_BIG)
                    elif j == i + 1:
                        tiles[2 * b + slot] = jnp.where(keep_next, tiles[2 * b + slot], NEG_BIG)
            tiles += lane_tiles(_dot_nt(q2, ck_scr[kv_head]))
            sink_term = []
            for slot in range(2):
                mine = tiles[slot::2]
                m = jnp.maximum(jnp.max(functools.reduce(jnp.maximum, mine), axis=-1, keepdims=True),
                                sinks[slot])
                tiles[slot::2] = [jnp.exp2(tl - m).astype(BF16) for tl in mine]
                sink_term.append(jnp.exp2(sinks[slot] - m))
            n_loc = 2 * (j1 - j0)
            e_loc = jnp.concatenate(tiles[:n_loc], axis=1)
            e_ctx = jnp.concatenate(tiles[n_loc:], axis=1)
            o = _dot(e_loc, v_scr[kv_head, j0 * grp:j1 * grp]) + _dot(e_ctx, cv_scr[kv_head])
            o = o[:, :LANES] / (o[:, LANES:] + jnp.where(lo_g, sink_term[0], sink_term[1]))
            o = o.astype(o_ref.dtype)
            for t in range(2):
                tile = 2 * kv_head + t
                o_ref[0, i * BLOCK:(i + 1) * BLOCK, tile * LANES:(tile + 1) * LANES] = o[t * BLOCK:(t + 1) * BLOCK]


def _lat_attention(q, k, v, b, first, n, sink, cache_k, cache_v, layer, cos, sin):
    c_len = cache_k.shape[2]
    return pl.pallas_call(
        functools.partial(_lat_attn_kernel, n=n, c_len=c_len),
        grid=(b,),
        in_specs=[
            pl.BlockSpec(memory_space=pltpu.SMEM),
            pl.BlockSpec((1, n, ATTN_WIDTH), lambda i: (i + first, 0, 0)),
            pl.BlockSpec((1, n, KV_WIDTH), lambda i: (i + first, 0, 0)),
            pl.BlockSpec((1, n, KV_WIDTH), lambda i: (i + first, 0, 0)),
            pl.BlockSpec((1, 1, c_len, KV_WIDTH), lambda i: (i, layer, 0, 0)),
            pl.BlockSpec((1, 1, c_len, KV_WIDTH), lambda i: (i, layer, 0, 0)),
            _resident((n, LANES), lambda i: (0, 0)),
            _resident((n, LANES), lambda i: (0, 0)),
        ],
        out_specs=pl.BlockSpec((1, n, ATTN_WIDTH), lambda i: (i, 0, 0)),
        out_shape=jax.ShapeDtypeStruct((b, n, ATTN_WIDTH), BF16),
        scratch_shapes=[
            pltpu.VMEM((N_KV_HEADS, 2 * n, LANES), BF16),
            pltpu.VMEM((N_KV_HEADS, 2 * n, LANES), BF16),
            pltpu.VMEM((N_KV_HEADS, 2 * n, 2 * LANES), BF16),
            pltpu.VMEM((N_KV_HEADS, 2 * c_len, LANES), BF16),
            pltpu.VMEM((N_KV_HEADS, 2 * c_len, 2 * LANES), BF16),
        ],
        compiler_params=_cparams("parallel"),
        name="latent_attention",
    )(sink, q, k, v, cache_k, cache_v, cos, sin)


def kernel(x_prompt, x_sample, cache_k, cache_v, c, c_ctx, w_mod, b_mod, norm_g, ffn_w_gate, ffn_w_up,
           ffn_w_down, w_in, w_out, hyena_conv_w, hyena_f_w1, hyena_f_b1, hyena_f_w2, hyena_f_b2,
           hyena_f_w3, hyena_f_freq, hyena_decay, hyena_bias, attn_sink):
    batch, seq, d = x_prompt.shape
    dec_batch, dec_seq, _ = x_sample.shape
    past_len = cache_k.shape[2]

    cvecs = jnp.concatenate([c_ctx[None], c], axis=0)
    cvecs = jnp.pad(cvecs, ((0, MOD_ROWS - cvecs.shape[0]), (0, 0)))
    mod = _modulation(cvecs, w_mod, b_mod).reshape(DEPTH, MOD_ROWS, N_SUB, 3, d)

    wg, wu, wd = ffn_w_gate, ffn_w_up, ffn_w_down
    w_in_b = w_in.astype(BF16)
    w_out_b = w_out.astype(BF16)
    ck = cache_k.reshape(dec_batch, DEPTH, past_len, KV_WIDTH)
    cv = cache_v.reshape(dec_batch, DEPTH, past_len, KV_WIDTH)
    rope_cos, rope_sin = (jnp.asarray(t) for t in _rope_tables(dec_seq))

    tables = {}
    for n in (seq, dec_seq):
        fw, iw = (jnp.asarray(t).astype(BF16) for t in _hyena_dft(n))
        pos, chan = (jnp.asarray(t).astype(BF16) for t in _fourier_tables(n))
        tables[n] = (fw, iw, pos, chan)

    ctx_rows, lat_rows = batch * seq, dec_batch * dec_seq
    rows = ctx_rows + lat_rows
    sizes = (ctx_rows, lat_rows, dec_seq)
    xs = (x_prompt.reshape(ctx_rows, d), x_sample.reshape(lat_rows, d))
    new_k, new_v = [], []
    for l in range(DEPTH):
        x1, p_fh, q, k, v = _stage_a(xs, mod, norm_g, wg, wu, wd, w_in_b, l, *sizes, tm=TOKEN_TILE)
        y_fh, y_att = [], []
        for n, nseq, first_seq, latent in ((seq, batch, 0, False), (dec_seq, dec_batch, ctx_rows // dec_seq, True)):
            fw, iw, pos, chan = tables[n]
            coef = _filter_spectrum(n, fw, hyena_f_w1[l], hyena_f_b1[l], hyena_f_w2[l], hyena_f_b2[l],
                                    hyena_f_w3[l], hyena_f_freq[l], hyena_decay[l])
            y = _fh_mix(p_fh.reshape(rows // n, n, FH_WIDTH), nseq, first_seq, n, FH_GROUP[latent],
                        pos, chan, fw, iw, coef, hyena_conv_w[l], hyena_bias[l])
            y_fh.append(y.reshape(nseq * n, -1))
            qs, ks, vs = (a.reshape(rows // n, n, -1) for a in (q, k, v))
            if latent:
                y = _lat_attention(qs, ks, vs, nseq, first_seq, n, attn_sink[l], ck, cv, l, rope_cos, rope_sin)
            else:
                y, k_t, v_t = _ctx_attention(qs, ks, vs, nseq, n, attn_sink[l])
                new_k.append(k_t)
                new_v.append(v_t)
            y_att.append(y.reshape(nseq * n, -1))
        xs = _stage_b(x1, y_fh, y_att, mod, norm_g, w_out_b, wg, wu, wd, l, *sizes, tm=TOKEN_TILE,
                      split=l == DEPTH - 1)

    def cache_layout(per_layer):
        t = jnp.stack(per_layer, axis=0).reshape(DEPTH, batch, N_KV_HEADS, HEAD_DIM, seq)
        return jnp.transpose(t, (1, 0, 4, 2, 3))

    return (xs[0].reshape(batch, seq, d), xs[1].reshape(dec_batch, dec_seq, d),
            cache_layout(new_k), cache_layout(new_v))
```
